```python
import jax, jax.numpy as jnp
from jax import lax
import numpy as np

D_MODEL = 4096
BATCH = 2
SEQ = 8192
DEPTH = 2

GRID_W = 64
CTX_LEN = 256
HEAD_DIM = 128
ROPE_THETA = 10000.0
EPS = 1e-6
Q_BLOCK = 128
NEG_INF = -1e30

MLA_HEADS = 3 * D_MODEL // (8 * HEAD_DIM)
MLA_Q_LORA = D_MODEL // 4
MLA_KV_LORA = 512
MLA_NOPE = 128
MLA_ROPE = 64
MLA_V = 128
GQA_HEADS = 3 * D_MODEL // (8 * HEAD_DIM)
GQA_KV_HEADS = GQA_HEADS // 3
NA_HEADS = D_MODEL // (4 * HEAD_DIM)
NA_KH = 8
NA_KW = 16

MLA_WIDTH = MLA_HEADS * MLA_V
GQA_WIDTH = GQA_HEADS * HEAD_DIM
NA_WIDTH = NA_HEADS * HEAD_DIM
MIX_WIDTH = MLA_WIDTH + GQA_WIDTH + NA_WIDTH

IN_SIZES = (MLA_Q_LORA, MLA_KV_LORA, MLA_ROPE,
            GQA_HEADS * HEAD_DIM, 2 * GQA_KV_HEADS * HEAD_DIM,
            3 * NA_HEADS * HEAD_DIM)
IN_COLS = sum(IN_SIZES)
IN_SPLITS = tuple(int(s) for s in np.cumsum(IN_SIZES)[:-1])

D_FF = 256 * (-(-8 * D_MODEL // (3 * 256)))
N_EXPERTS = 8
TOP_K = 2
MOE_D_FF = D_MODEL
N_DENSE = (DEPTH + 1) // 2
N_MOE = DEPTH // 2

kernel_name = "hybrid_mla_gqa_natten_moe_dit"


def rmsnorm(x, g):
    xf = x.astype(jnp.float32)
    y = xf * lax.rsqrt(jnp.mean(xf * xf, axis=-1, keepdims=True) + EPS)
    return (y * g.astype(jnp.float32)).astype(x.dtype)


def modulate(h, shift, scale):
    return h * (1 + scale) + shift


def rope_1d(x, pos):
    d = x.shape[-1]
    inv = ROPE_THETA ** (-jnp.arange(0, d, 2, dtype=jnp.float32) / d)
    ang = pos.astype(jnp.float32)[:, None] * inv[None, :]
    cos = jnp.cos(ang)[None, :, None, :]
    sin = jnp.sin(ang)[None, :, None, :]
    xf = x.astype(jnp.float32)
    x1, x2 = xf[..., : d // 2], xf[..., d // 2:]
    return jnp.concatenate([x1 * cos - x2 * sin, x2 * cos + x1 * sin], axis=-1).astype(x.dtype)


def axial_rope(x, rows, cols):
    h = x.shape[-1] // 2
    return jnp.concatenate([rope_1d(x[..., :h], rows), rope_1d(x[..., h:], cols)], axis=-1)


def attention(q, k, v):
    B, Tq, Hq, dk = q.shape
    Hk, dv = k.shape[2], v.shape[-1]
    G = Hq // Hk
    block = min(Q_BLOCK, Tq)
    nb = Tq // block
    scale = dk ** -0.5
    qb = q.reshape(B, nb, block, Hk, G, dk).transpose(1, 0, 2, 3, 4, 5)

    def one_block(qblk):
        s = jnp.einsum('bqkgd,bskd->bkgqs', qblk, k).astype(jnp.float32) * scale
        p = jax.nn.softmax(s, axis=-1).astype(v.dtype)
        return jnp.einsum('bkgqs,bskd->bqkgd', p, v)

    o = lax.map(one_block, qb)
    return o.transpose(1, 0, 2, 3, 4, 5).reshape(B, Tq, Hq, dv)


def neighborhood_attention(q, k, v, k_ctx, v_ctx, rpb):
    B, S, H, d = q.shape
    R = S // GRID_W
    KH = min(NA_KH, R)
    scale = d ** -0.5
    qg = q.reshape(B, R, GRID_W, H, d)
    kg = k.reshape(B, R, GRID_W, H, d)
    vg = v.reshape(B, R, GRID_W, H, d)
    col = jnp.arange(GRID_W)
    start_c = jnp.clip(col - NA_KW // 2, 0, GRID_W - NA_KW)
    col_mask = (col[None, :] >= start_c[:, None]) & (col[None, :] < start_c[:, None] + NA_KW)
    dc_idx = jnp.clip(col[None, :] - col[:, None] + NA_KW - 1, 0, 2 * NA_KW - 2)
    rpb_f = rpb.astype(jnp.float32)

    def one_row(r):
        start = jnp.clip(r - KH // 2, 0, R - KH)
        qr = lax.dynamic_index_in_dim(qg, r, axis=1, keepdims=False)
        kr = lax.dynamic_slice_in_dim(kg, start, KH, axis=1).reshape(B, KH * GRID_W, H, d)
        vr = lax.dynamic_slice_in_dim(vg, start, KH, axis=1).reshape(B, KH * GRID_W, H, d)
        dr_idx = start + jnp.arange(KH) - r + NA_KH - 1
        bias = rpb_f[:, dr_idx[None, :, None], dc_idx[:, None, :]]
        bias = jnp.where(col_mask[None, :, None, :], bias, NEG_INF).reshape(H, GRID_W, KH * GRID_W)
        s_loc = jnp.einsum('bqhd,bkhd->bhqk', qr, kr).astype(jnp.float32) * scale + bias[None]
        s_ctx = jnp.einsum('bqhd,bkhd->bhqk', qr, k_ctx).astype(jnp.float32) * scale
        p = jax.nn.softmax(jnp.concatenate([s_loc, s_ctx], axis=-1), axis=-1).astype(v.dtype)
        n_loc = KH * GRID_W
        return (jnp.einsum('bhqk,bkhd->bqhd', p[..., :n_loc], vr)
                + jnp.einsum('bhqk,bkhd->bqhd', p[..., n_loc:], v_ctx))

    o = lax.map(one_row, jnp.arange(R))
    return o.transpose(1, 0, 2, 3, 4).reshape(B, S, H, d)


def mixer(hx, hc, rows, cols, ctx_queries, w_in, g_q_a, g_kv_a, w_q_up, w_kv_up,
          g_q_gqa, g_k_gqa, rpb, g_grp, w_out):
    def heads(h, latent):
        B, T, _ = h.shape
        cq, ckv, kpe, q_g, kv_g, qkv_n = jnp.split(h @ w_in, IN_SPLITS, axis=-1)
        q_m = (rmsnorm(cq, g_q_a) @ w_q_up).reshape(B, T, MLA_HEADS, MLA_NOPE + MLA_ROPE)
        kv_m = (rmsnorm(ckv, g_kv_a) @ w_kv_up).reshape(B, T, MLA_HEADS, MLA_NOPE + MLA_V)
        q_pe = q_m[..., MLA_NOPE:]
        k_pe = kpe[:, :, None, :]
        q_g = rmsnorm(q_g.reshape(B, T, GQA_HEADS, HEAD_DIM), g_q_gqa)
        kv_g = kv_g.reshape(B, T, 2, GQA_KV_HEADS, HEAD_DIM)
        k_g = rmsnorm(kv_g[:, :, 0], g_k_gqa)
        v_g = kv_g[:, :, 1]
        if latent:
            q_pe = axial_rope(q_pe, rows, cols)
            k_pe = axial_rope(k_pe, rows, cols)
            q_g = axial_rope(q_g, rows, cols)
            k_g = axial_rope(k_g, rows, cols)
        q_m = jnp.concatenate([q_m[..., :MLA_NOPE], q_pe], axis=-1)
        k_m = jnp.concatenate([kv_m[..., :MLA_NOPE],
                               jnp.broadcast_to(k_pe, (B, T, MLA_HEADS, MLA_ROPE))], axis=-1)
        v_m = kv_m[..., MLA_NOPE:]
        qkv_n = qkv_n.reshape(B, T, 3, NA_HEADS, HEAD_DIM)
        return (q_m, k_m, v_m), (q_g, k_g, v_g), (qkv_n[:, :, 0], qkv_n[:, :, 1], qkv_n[:, :, 2])

    def merge(o_m, o_g, o_n):
        B, T = o_m.shape[:2]
        o = jnp.concatenate([
            rmsnorm(o_m.reshape(B, T, MLA_WIDTH), g_grp[:MLA_WIDTH]),
            rmsnorm(o_g.reshape(B, T, GQA_WIDTH), g_grp[MLA_WIDTH:MLA_WIDTH + GQA_WIDTH]),
            rmsnorm(o_n.reshape(B, T, NA_WIDTH), g_grp[MLA_WIDTH + GQA_WIDTH:]),
        ], axis=-1)
        return o @ w_out

    (qm_c, km_c, vm_c), (qg_c, kg_c, vg_c), (qn_c, kn_c, vn_c) = heads(hc, False)
    (qm_x, km_x, vm_x), (qg_x, kg_x, vg_x), (qn_x, kn_x, vn_x) = heads(hx, True)
    o_x = merge(
        attention(qm_x, jnp.concatenate([km_c, km_x], axis=1), jnp.concatenate([vm_c, vm_x], axis=1)),
        attention(qg_x, jnp.concatenate([kg_c, kg_x], axis=1), jnp.concatenate([vg_c, vg_x], axis=1)),
        neighborhood_attention(qn_x, kn_x, vn_x, kn_c, vn_c, rpb),
    )
    if not ctx_queries:
        return o_x, None
    o_c = merge(attention(qm_c, km_c, vm_c), attention(qg_c, kg_c, vg_c), attention(qn_c, kn_c, vn_c))
    return o_x, o_c


def swiglu(h, w1, w3, w2):
    return (jax.nn.silu(h @ w1) * (h @ w3)) @ w2


def moe_swiglu(h, router, w1, w3, w2):
    logits = (h @ router).astype(jnp.float32)
    vals, idx = lax.top_k(logits, TOP_K)
    gates = jax.nn.softmax(vals, axis=-1)
    combine = jnp.sum(jax.nn.one_hot(idx, N_EXPERTS, dtype=jnp.float32) * gates[..., None], axis=-2)
    out = combine[..., 0:1].astype(h.dtype) * swiglu(h, w1[0], w3[0], w2[0])
    for e in range(1, N_EXPERTS):
        out = out + combine[..., e:e + 1].astype(h.dtype) * swiglu(h, w1[e], w3[e], w2[e])
    return out


def setup_inputs(seed: int = 0) -> dict:
    key = jax.random.key(seed)
    ks = jax.random.split(key, 32)
    f32 = jnp.float32

    def nrm(k, shape, fan_in, mult=1.0):
        return jax.random.normal(k, shape, f32) * (mult * fan_in ** -0.5)

    def gain(k, shape):
        return 1.0 + 0.05 * jax.random.normal(k, shape, f32)

    return {
        "x": jax.random.normal(ks[0], (BATCH, SEQ, D_MODEL), f32),
        "c": jax.random.normal(ks[1], (BATCH, D_MODEL), f32),
        "ctx": jax.random.normal(ks[2], (BATCH, CTX_LEN, D_MODEL), f32),
        "c_ctx": jax.random.normal(ks[3], (D_MODEL,), f32),
        "w_ada": nrm(ks[4], (DEPTH, D_MODEL, 6 * D_MODEL), D_MODEL, 0.5),
        "b_ada": 0.02 * jax.random.normal(ks[5], (DEPTH, 6 * D_MODEL), f32),
        "g_pre_mix": gain(ks[6], (DEPTH, D_MODEL)),
        "g_post_mix": gain(ks[7], (DEPTH, D_MODEL)),
        "g_pre_ffn": gain(ks[8], (DEPTH, D_MODEL)),
        "g_post_ffn": gain(ks[9], (DEPTH, D_MODEL)),
        "w_in": nrm(ks[10], (DEPTH, D_MODEL, IN_COLS), D_MODEL),
        "g_q_a": gain(ks[11], (DEPTH, MLA_Q_LORA)),
        "g_kv_a": gain(ks[12], (DEPTH, MLA_KV_LORA)),
        "w_q_up": nrm(ks[13], (DEPTH, MLA_Q_LORA, MLA_HEADS * (MLA_NOPE + MLA_ROPE)), MLA_Q_LORA),
        "w_kv_up": nrm(ks[14], (DEPTH, MLA_KV_LORA, MLA_HEADS * (MLA_NOPE + MLA_V)), MLA_KV_LORA),
        "g_q_gqa": gain(ks[15], (DEPTH, HEAD_DIM)),
        "g_k_gqa": gain(ks[16], (DEPTH, HEAD_DIM)),
        "na_rpb": 0.5 * jax.random.normal(ks[17], (DEPTH, NA_HEADS, 2 * NA_KH - 1, 2 * NA_KW - 1), f32),
        "g_grp": gain(ks[18], (DEPTH, MIX_WIDTH)),
        "w_out": nrm(ks[19], (DEPTH, MIX_WIDTH, D_MODEL), MIX_WIDTH),
        "ffn_w1": nrm(ks[20], (N_DENSE, D_MODEL, D_FF), D_MODEL),
        "ffn_w3": nrm(ks[21], (N_DENSE, D_MODEL, D_FF), D_MODEL),
        "ffn_w2": nrm(ks[22], (N_DENSE, D_FF, D_MODEL), D_FF),
        "router": nrm(ks[23], (N_MOE, D_MODEL, N_EXPERTS), D_MODEL),
        "moe_w1": nrm(ks[24], (N_MOE, N_EXPERTS, D_MODEL, MOE_D_FF), D_MODEL),
        "moe_w3": nrm(ks[25], (N_MOE, N_EXPERTS, D_MODEL, MOE_D_FF), D_MODEL),
        "moe_w2": nrm(ks[26], (N_MOE, N_EXPERTS, MOE_D_FF, D_MODEL), MOE_D_FF),
    }


def reference(x, c, ctx, c_ctx, w_ada, b_ada, g_pre_mix, g_post_mix, g_pre_ffn, g_post_ffn,
              w_in, g_q_a, g_kv_a, w_q_up, w_kv_up, g_q_gqa, g_k_gqa, na_rpb, g_grp, w_out,
              ffn_w1, ffn_w3, ffn_w2, router, moe_w1, moe_w3, moe_w2):
    S = x.shape[1]
    L = ctx.shape[1]
    t = jnp.arange(S)
    rows, cols = t // GRID_W, t % GRID_W
    xc = ctx
    sc = jax.nn.silu(c)
    scc = jax.nn.silu(c_ctx)
    for i in range(DEPTH):
        last = i == DEPTH - 1
        mod_x = jnp.split((sc @ w_ada[i] + b_ada[i])[:, None, :], 6, axis=-1)
        mod_c = jnp.split((scc @ w_ada[i] + b_ada[i])[None, None, :], 6, axis=-1)

        hx = modulate(rmsnorm(x, g_pre_mix[i]), mod_x[0], mod_x[1])
        hc = modulate(rmsnorm(xc, g_pre_mix[i]), mod_c[0], mod_c[1])
        o_x, o_c = mixer(hx, hc, rows, cols, not last, w_in[i], g_q_a[i], g_kv_a[i], w_q_up[i],
                         w_kv_up[i], g_q_gqa[i], g_k_gqa[i], na_rpb[i], g_grp[i], w_out[i])
        x = x + mod_x[2] * rmsnorm(o_x, g_post_mix[i])
        if not last:
            xc = xc + mod_c[2] * rmsnorm(o_c, g_post_mix[i])

        j = i // 2
        if i % 2 == 0:
            def ffn(h):
                return swiglu(h, ffn_w1[j], ffn_w3[j], ffn_w2[j])
        else:
            def ffn(h):
                return moe_swiglu(h, router[j], moe_w1[j], moe_w3[j], moe_w2[j])
        hx = modulate(rmsnorm(x, g_pre_ffn[i]), mod_x[3], mod_x[4])
        if last:
            f_x = ffn(hx)
        else:
            hc = modulate(rmsnorm(xc, g_pre_ffn[i]), mod_c[3], mod_c[4])
            f = ffn(jnp.concatenate([hc, hx], axis=1))
            f_c, f_x = f[:, :L], f[:, L:]
            xc = xc + mod_c[5] * rmsnorm(f_c, g_post_ffn[i])
        x = x + mod_x[5] * rmsnorm(f_x, g_post_ffn[i])
    return x
```

```python
import functools
import math

import jax
import jax.numpy as jnp
import numpy as np
from jax import lax
from jax.experimental import pallas as pl
from jax.experimental.pallas import tpu as pltpu

F32 = jnp.float32
BF16 = jnp.bfloat16

GRID_W = 64
HEAD_DIM = 128
ROPE_THETA = 10000.0
EPS = 1e-6
NEG_INF = -1e30
MLA_NOPE = 128
MLA_ROPE = 64
MLA_V = 128
NA_KH = 8
NA_KW = 16
TOP_K = 2

LANES = 128
SUBLANES = 8
VMEM_LIMIT_BYTES = 56 * 1024 * 1024

ROW_TILE = 256
MM_TILE_M = 512
MOE_TILE_M = 512
NA_ROWS_PER_STEP = 8


def _cparams(n_axes):
    return pltpu.CompilerParams(dimension_semantics=("arbitrary",) * n_axes,
                                vmem_limit_bytes=VMEM_LIMIT_BYTES)


def _rms(v, g):
    ms = jnp.mean(v * v, axis=-1, keepdims=True)
    return v * lax.rsqrt(ms + EPS) * g


def _silu(v):
    return v * jax.nn.sigmoid(v)


ADA_K_CHUNK = 64


def _ada_body(ct_ref, w_ref, b_ref, o_ref, *, n_rows):
    d, tn = w_ref.shape

    def step(i, accs):
        off = pl.multiple_of(i * ADA_K_CHUNK, ADA_K_CHUNK)
        w = w_ref[pl.ds(off, ADA_K_CHUNK), :]
        s = _silu(ct_ref[pl.ds(off, ADA_K_CHUNK), :])
        out = []
        for r in range(n_rows):
            prod = w * s[:, r:r + 1]
            part = prod[0:SUBLANES]
            for j in range(1, ADA_K_CHUNK // SUBLANES):
                part = part + prod[j * SUBLANES:(j + 1) * SUBLANES]
            out.append(accs[r] + part)
        return tuple(out)

    accs = lax.fori_loop(0, d // ADA_K_CHUNK, step,
                         tuple(jnp.zeros((SUBLANES, tn), F32) for _ in range(n_rows)))
    rows = [jnp.sum(a, axis=0, keepdims=True) for a in accs]
    rows.append(jnp.zeros((SUBLANES - n_rows, tn), F32))
    o_ref[...] = jnp.concatenate(rows, axis=0) + b_ref[...]


def _ada(cond, w_ada, b_ada):
    n_rows, d = cond.shape
    depth, _, n = w_ada.shape
    tn = 512
    ct = jnp.zeros((d, SUBLANES), F32).at[:, :n_rows].set(cond.T)
    return pl.pallas_call(
        functools.partial(_ada_body, n_rows=n_rows),
        grid=(depth, n // tn),
        in_specs=[pl.BlockSpec((d, SUBLANES), lambda l, j: (0, 0)),
                  pl.BlockSpec((None, d, tn), lambda l, j: (l, 0, j)),
                  pl.BlockSpec((None, 1, tn), lambda l, j: (l, 0, j))],
        out_specs=pl.BlockSpec((None, SUBLANES, tn), lambda l, j: (l, 0, j)),
        out_shape=jax.ShapeDtypeStruct((depth, SUBLANES, n), F32),
        compiler_params=_cparams(2),
        name="ada",
    )(ct, w_ada, b_ada.reshape(depth, 1, n))


def _row_body(*refs, has_res, has_next, y_parts, router, n_experts):
    it = iter(refs)
    x_ref = next(it)
    if has_res:
        y_ref, gate_ref, gpost_ref = next(it), next(it), next(it)
    if has_next:
        gpre_ref, shift_ref, scale_ref = next(it), next(it), next(it)
    if router:
        rhi_ref, rlo_ref = next(it), next(it)
    if has_res:
        xo_ref = next(it)
    if has_next:
        h_ref = next(it)
    if router:
        rt_ref = next(it)

    x = x_ref[...]
    d = x.shape[-1]
    if has_res:
        y = y_ref[:, 0:d].astype(F32)
        for p in range(1, y_parts):
            y = y + y_ref[:, p * d:(p + 1) * d].astype(F32)
        x = x + gate_ref[...] * _rms(y, gpost_ref[...])
        xo_ref[...] = x
    if has_next:
        h = _rms(x, gpre_ref[...]) * (1.0 + scale_ref[...]) + shift_ref[...]
        h_ref[...] = h.astype(h_ref.dtype)
    if router:
        hi = h.astype(BF16)
        lo = (h - hi.astype(F32)).astype(BF16)
        logits = (jnp.dot(hi, rhi_ref[...], preferred_element_type=F32)
                  + jnp.dot(hi, rlo_ref[...], preferred_element_type=F32)
                  + jnp.dot(lo, rhi_ref[...], preferred_element_type=F32))
        lane = lax.broadcasted_iota(jnp.int32, logits.shape, 1)
        lg = jnp.where(lane < n_experts, logits, -jnp.inf)
        v1 = jnp.max(lg, axis=-1, keepdims=True)
        i1 = jnp.min(jnp.where(lg == v1, lane, LANES), axis=-1, keepdims=True)
        lg2 = jnp.where(lane == i1, -jnp.inf, lg)
        v2 = jnp.max(lg2, axis=-1, keepdims=True)
        i2 = jnp.min(jnp.where(lg2 == v2, lane, LANES), axis=-1, keepdims=True)
        e = jnp.exp(v2 - v1)
        g1 = 1.0 / (1.0 + e)
        g2 = e / (1.0 + e)
        rt_ref[...] = jnp.where(lane == 0, g1,
                                jnp.where(lane == 1, g2,
                                          jnp.where(lane == 2, i1.astype(F32),
                                                    jnp.where(lane == 3, i2.astype(F32), 0.0))))


def _row_call(x, *, n_tiles, group_fn, modtab, res=None, nxt=None, router=None, h_dtype=BF16, h_prev=None,
              h_rows=None, h_tile0=0, name="row"):
    tm = ROW_TILE
    d = x.shape[-1]
    has_res, has_next = res is not None, nxt is not None

    def modspec(k):
        return pl.BlockSpec((None, 1, d), lambda i: (group_fn(i) * 6 + k, 0, 0))

    vecspec = pl.BlockSpec((1, d), lambda i: (0, 0))
    ins, in_specs = [x], [pl.BlockSpec((tm, d), lambda i: (i, 0))]
    y_parts = 1
    if has_res:
        y, y_tile0, y_parts, gate_k, g_post = res
        ins += [y, modtab, g_post.reshape(1, d)]
        in_specs += [pl.BlockSpec((tm, y_parts * d), lambda i: (i + y_tile0, 0)), modspec(gate_k), vecspec]
    if has_next:
        g_pre, shift_k, scale_k = nxt
        ins += [g_pre.reshape(1, d), modtab, modtab]
        in_specs += [vecspec, modspec(shift_k), modspec(scale_k)]
    n_experts = 0
    if router is not None:
        n_experts = router.shape[-1]
        rp = jnp.zeros((d, LANES), F32).at[:, :n_experts].set(router)
        rhi = rp.astype(BF16)
        rlo = (rp - rhi.astype(F32)).astype(BF16)
        ins += [rhi, rlo]
        in_specs += [pl.BlockSpec((d, LANES), lambda i: (0, 0))] * 2
    out_shape, out_specs = [], []
    if has_res:
        out_shape.append(jax.ShapeDtypeStruct((n_tiles * tm, d), F32))
        out_specs.append(pl.BlockSpec((tm, d), lambda i: (i, 0)))
    aliases = {}
    if has_next:
        if h_rows is None:
            h_rows = n_tiles * tm
        out_shape.append(jax.ShapeDtypeStruct((h_rows, d), h_dtype))
        out_specs.append(pl.BlockSpec((tm, d), lambda i: (i + h_tile0, 0)))
        if h_prev is not None:
            aliases = {len(ins): len(out_shape) - 1}
            ins.append(h_prev)
            in_specs.append(pl.BlockSpec(memory_space=pl.ANY))
    if router is not None:
        out_shape.append(jax.ShapeDtypeStruct((n_tiles * tm, LANES), F32))
        out_specs.append(pl.BlockSpec((tm, LANES), lambda i: (i, 0)))

    def body(*refs):
        if h_prev is not None:
            n_in = len(ins)
            refs = refs[:n_in - 1] + refs[n_in:]
        _row_body(*refs, has_res=has_res, has_next=has_next, y_parts=y_parts,
                  router=router is not None, n_experts=n_experts)

    outs = pl.pallas_call(
        body, grid=(n_tiles,), in_specs=in_specs, out_specs=out_specs, out_shape=out_shape,
        input_output_aliases=aliases, compiler_params=_cparams(1), name=name,
    )(*ins)
    outs = list(outs)
    x_new = outs.pop(0) if has_res else None
    h = outs.pop(0) if has_next else None
    rt = outs.pop(0) if router is not None else None
    return x_new, h, rt


def _mm_body(*refs, a_lo, a_hi, rms):
    if rms:
        a_ref, g_ref, w_ref, o_ref = refs
    else:
        a_ref, w_ref, o_ref = refs
    a = a_ref[:, a_lo:a_hi]
    if rms:
        a = _rms(a.astype(F32), g_ref[...]).astype(BF16)
    o_ref[...] = jnp.dot(a, w_ref[...], preferred_element_type=F32).astype(o_ref.dtype)


def _matmul(a, w, *, m_rows, tn, out_dtype=BF16, a_block_cols=None, a_lo=0, a_hi=None, rms_g=None, name="mm"):
    tm = MM_TILE_M
    k, n = w.shape
    if a_block_cols is None:
        a_block_cols = a.shape[1]
    if a_hi is None:
        a_hi = a_lo + k
    ins, in_specs = [a], [pl.BlockSpec((tm, a_block_cols), lambda j, i: (i, 0))]
    if rms_g is not None:
        ins.append(rms_g.reshape(1, k))
        in_specs.append(pl.BlockSpec((1, k), lambda j, i: (0, 0)))
    ins.append(w)
    in_specs.append(pl.BlockSpec((k, tn), lambda j, i: (0, j)))
    return pl.pallas_call(
        functools.partial(_mm_body, a_lo=a_lo, a_hi=a_hi, rms=rms_g is not None),
        grid=(n // tn, m_rows // tm), in_specs=in_specs,
        out_specs=pl.BlockSpec((tm, tn), lambda j, i: (i, j)),
        out_shape=jax.ShapeDtypeStruct((m_rows, n), out_dtype),
        compiler_params=_cparams(2), name=name,
    )(*ins)


def _mmk_body(a_ref, w_ref, o_ref, acc_ref, *, nk):
    kk = pl.program_id(2)

    @pl.when(kk == 0)
    def _():
        acc_ref[...] = jnp.zeros_like(acc_ref)

    acc_ref[...] += jnp.dot(a_ref[...], w_ref[...], preferred_element_type=F32)

    @pl.when(kk == nk - 1)
    def _():
        o_ref[...] = acc_ref[...].astype(o_ref.dtype)


def _matmul_ktiled(a, w, *, m_rows, tn, tk, out_dtype=BF16, name="mmk"):
    tm = MM_TILE_M
    k, n = w.shape
    nk = k // tk
    return pl.pallas_call(
        functools.partial(_mmk_body, nk=nk),
        grid=(n // tn, m_rows // tm, nk),
        in_specs=[pl.BlockSpec((tm, tk), lambda j, i, kk: (i, kk)),
                  pl.BlockSpec((tk, tn), lambda j, i, kk: (kk, j))],
        out_specs=pl.BlockSpec((tm, tn), lambda j, i, kk: (i, j)),
        out_shape=jax.ShapeDtypeStruct((m_rows, n), out_dtype),
        scratch_shapes=[pltpu.VMEM((tm, tn), F32)],
        compiler_params=_cparams(3), name=name,
    )(a, w)


def _glu_body(a_ref, w1_ref, w3_ref, o_ref):
    a = a_ref[...]
    h1 = jnp.dot(a, w1_ref[...], preferred_element_type=F32)
    h3 = jnp.dot(a, w3_ref[...], preferred_element_type=F32)
    o_ref[...] = (_silu(h1) * h3).astype(o_ref.dtype)


def _glu(a, w1, w3, *, m_rows, tf, name="glu"):
    tm = MM_TILE_M
    k, f = w1.shape
    return pl.pallas_call(
        _glu_body, grid=(f // tf, m_rows // tm),
        in_specs=[pl.BlockSpec((tm, k), lambda j, i: (i, 0)),
                  pl.BlockSpec((k, tf), lambda j, i: (0, j)),
                  pl.BlockSpec((k, tf), lambda j, i: (0, j))],
        out_specs=pl.BlockSpec((tm, tf), lambda j, i: (i, j)),
        out_shape=jax.ShapeDtypeStruct((m_rows, f), BF16),
        compiler_params=_cparams(2), name=name,
    )(a, w1, w3)


def _grpnorm_body(o_ref, g_ref, out_ref, *, bounds):
    for lo, hi in bounds:
        v = o_ref[:, lo:hi].astype(F32)
        out_ref[:, lo:hi] = _rms(v, g_ref[:, lo:hi]).astype(out_ref.dtype)


def _grpnorm(o_all, g_grp, bounds, *, m_rows):
    tm = ROW_TILE
    d = o_all.shape[1]
    return pl.pallas_call(
        functools.partial(_grpnorm_body, bounds=bounds), grid=(m_rows // tm,),
        in_specs=[pl.BlockSpec((tm, d), lambda i: (i, 0)), pl.BlockSpec((1, d), lambda i: (0, 0))],
        out_specs=pl.BlockSpec((tm, d), lambda i: (i, 0)),
        out_shape=jax.ShapeDtypeStruct((m_rows, d), BF16),
        compiler_params=_cparams(1), name="grpnorm",
    )(o_all, g_grp.reshape(1, d))


def _rope(x, c, s, half):
    lane = lax.broadcasted_iota(jnp.int32, x.shape, 1)
    first = (lane % (2 * half)) < half
    swapped = jnp.where(first, pltpu.roll(x, LANES - half, 1), pltpu.roll(x, half, 1))
    return x * c + swapped * s


def _prep_body(u_ref, qraw_ref, kvraw_ref, kpe_ref, cg_ref, sg_ref, cm_ref, sm_ref, gq_ref, gk_ref,
               qm_o, km_o, vm_o, qg_o, kg_o, vg_o, qn_o, kn_o, vn_o,
               *, hm, hg, hkv, hn, off_qg, off_kvg, off_n):
    cg, sg, cm, sm = cg_ref[...], sg_ref[...], cm_ref[...], sm_ref[...]
    scale_m = (MLA_NOPE + MLA_ROPE) ** -0.5
    scale_h = HEAD_DIM ** -0.5
    d = HEAD_DIM
    kpe = _rope(kpe_ref[...].astype(F32), cm, sm, MLA_ROPE // 4)[:, :MLA_ROPE].astype(BF16)
    pe0 = hm * MLA_NOPE
    for h in range(hm):
        qm_o[h, :, 0:MLA_NOPE] = (qraw_ref[:, h * MLA_NOPE:(h + 1) * MLA_NOPE].astype(F32) * scale_m).astype(BF16)
        km_o[h, :, 0:MLA_NOPE] = kvraw_ref[:, h * 2 * d:h * 2 * d + MLA_NOPE]
        km_o[h, :, MLA_NOPE:MLA_NOPE + MLA_ROPE] = kpe
        vm_o[h] = kvraw_ref[:, h * 2 * d + MLA_NOPE:(h + 1) * 2 * d]
    for j in range(hm // 2):
        pe = _rope(qraw_ref[:, pe0 + j * LANES:pe0 + (j + 1) * LANES].astype(F32), cm, sm, MLA_ROPE // 4)
        pe = (pe * scale_m).astype(BF16)
        qm_o[2 * j, :, MLA_NOPE:MLA_NOPE + MLA_ROPE] = pe[:, :MLA_ROPE]
        qm_o[2 * j + 1, :, MLA_NOPE:MLA_NOPE + MLA_ROPE] = pe[:, MLA_ROPE:]
    for h in range(hg):
        q = _rms(u_ref[:, off_qg + h * d:off_qg + (h + 1) * d].astype(F32), gq_ref[...])
        qg_o[h] = (_rope(q, cg, sg, d // 4) * scale_h).astype(BF16)
    for h in range(hkv):
        k = _rms(u_ref[:, off_kvg + h * d:off_kvg + (h + 1) * d].astype(F32), gk_ref[...])
        kg_o[h] = _rope(k, cg, sg, d // 4).astype(BF16)
        vg_o[h] = u_ref[:, off_kvg + (hkv + h) * d:off_kvg + (hkv + h + 1) * d]
    for h in range(hn):
        qn_o[h] = (u_ref[:, off_n + h * d:off_n + (h + 1) * d].astype(F32) * scale_h).astype(BF16)
        kn_o[h] = u_ref[:, off_n + (hn + h) * d:off_n + (hn + h + 1) * d]
        vn_o[h] = u_ref[:, off_n + (2 * hn + h) * d:off_n + (2 * hn + h + 1) * d]


def _prep(u, qraw, kvraw, kpe, tables, g_q, g_k, *, dims):
    tm = ROW_TILE
    t_tot = u.shape[0]
    b, s, l = dims["B"], dims["S"], dims["L"]
    hm, hg, hkv, hn = dims["HM"], dims["HG"], dims["HKV"], dims["HN"]
    tk_len = s + l
    n_lat = b * s // tm
    spb, lpb = s // tm, l // tm

    def kmap(i):
        lat = i < n_lat
        j = i - n_lat
        bb = jnp.where(lat, i // spb, j // lpb)
        pos = jnp.where(lat, i % spb, spb + j % lpb)
        return bb, pos

    def kspec(h, dk):
        def im(i):
            bb, pos = kmap(i)
            return (bb, 0, pos, 0)
        return pl.BlockSpec((None, h, tm, dk), im)

    def qspec(h, dk):
        return pl.BlockSpec((h, tm, dk), lambda i: (0, i, 0))

    def full(arr):
        return pl.BlockSpec((tm, arr.shape[1]), lambda i: (i, 0))

    dk_m = MLA_NOPE + MLA_ROPE
    d = HEAD_DIM
    out_shape = [
        jax.ShapeDtypeStruct((hm, t_tot, dk_m), BF16), jax.ShapeDtypeStruct((b, hm, tk_len, dk_m), BF16),
        jax.ShapeDtypeStruct((b, hm, tk_len, MLA_V), BF16),
        jax.ShapeDtypeStruct((hg, t_tot, d), BF16), jax.ShapeDtypeStruct((b, hkv, tk_len, d), BF16),
        jax.ShapeDtypeStruct((b, hkv, tk_len, d), BF16),
        jax.ShapeDtypeStruct((hn, t_tot, d), BF16), jax.ShapeDtypeStruct((b, hn, tk_len, d), BF16),
        jax.ShapeDtypeStruct((b, hn, tk_len, d), BF16),
    ]
    out_specs = [qspec(hm, dk_m), kspec(hm, dk_m), kspec(hm, MLA_V),
                 qspec(hg, d), kspec(hkv, d), kspec(hkv, d),
                 qspec(hn, d), kspec(hn, d), kspec(hn, d)]
    vec = pl.BlockSpec((1, d), lambda i: (0, 0))
    return pl.pallas_call(
        functools.partial(_prep_body, hm=hm, hg=hg, hkv=hkv, hn=hn,
                          off_qg=dims["OFF_QG"], off_kvg=dims["OFF_KVG"], off_n=dims["OFF_N"]),
        grid=(t_tot // tm,),
        in_specs=[full(u), full(qraw), full(kvraw), full(kpe)] + [full(t) for t in tables] + [vec, vec],
        out_specs=out_specs, out_shape=out_shape,
        compiler_params=_cparams(1), name="prep",
    )(u, qraw, kvraw, kpe, *tables, g_q.reshape(1, d), g_k.reshape(1, d))


def _attn_body(*refs, g, tq, tk, nk, dv, aliased):
    if aliased:
        q_ref, k_ref, v_ref, _, o_ref, m_sc, l_sc, acc_sc = refs
    else:
        q_ref, k_ref, v_ref, o_ref, m_sc, l_sc, acc_sc = refs
    dk = q_ref.shape[-1]
    q = q_ref[...].reshape(g * tq, dk)
    m_sc[...] = jnp.full_like(m_sc, -jnp.inf)
    l_sc[...] = jnp.zeros_like(l_sc)
    acc_sc[...] = jnp.zeros_like(acc_sc)

    def step(c, carry):
        off = pl.multiple_of(c * tk, tk)
        k = k_ref[pl.ds(off, tk), :]
        v = v_ref[pl.ds(off, tk), :]
        s = lax.dot_general(q, k, (((1,), (1,)), ((), ())), preferred_element_type=F32)
        m_prev = m_sc[...]
        m_new = jnp.maximum(m_prev, jnp.max(s, axis=-1, keepdims=True))
        p = jnp.exp(s - m_new)
        alpha = jnp.exp(m_prev - m_new)
        l_sc[...] = alpha * l_sc[...] + jnp.sum(p, axis=-1, keepdims=True)
        acc_sc[...] = alpha * acc_sc[...] + jnp.dot(p.astype(BF16), v, preferred_element_type=F32)
        m_sc[...] = m_new
        return carry

    lax.fori_loop(0, nk, step, 0)
    o = acc_sc[...] / l_sc[...]
    for gi in range(g):
        o_ref[:, gi * dv:(gi + 1) * dv] = o[gi * tq:(gi + 1) * tq].astype(o_ref.dtype)


def _attention(q, k, v, *, n_batch, g, tq, q_tile0, q_tiles_per_b, o_tile0, k_rows, k_blk, tk,
               o_rows, o_cols, o_col0, o_prev=None, name="attn"):
    hk = k.shape[1]
    dk, dv = q.shape[-1], v.shape[-1]
    nk = k_rows // tk
    aliased = o_prev is not None
    ins = [q, k, v]
    in_specs = [pl.BlockSpec((g, tq, dk), lambda b, h, i: (h, q_tile0 + b * q_tiles_per_b + i, 0)),
                pl.BlockSpec((None, None, k_rows, dk), lambda b, h, i: (b, h, k_blk, 0)),
                pl.BlockSpec((None, None, k_rows, dv), lambda b, h, i: (b, h, k_blk, 0))]
    aliases = {}
    if aliased:
        ins.append(o_prev)
        in_specs.append(pl.BlockSpec(memory_space=pl.ANY))
        aliases = {3: 0}
    cb0 = o_col0 // (g * dv)
    return pl.pallas_call(
        functools.partial(_attn_body, g=g, tq=tq, tk=tk, nk=nk, dv=dv, aliased=aliased),
        grid=(n_batch, hk, q_tiles_per_b), in_specs=in_specs,
        out_specs=pl.BlockSpec((tq, g * dv), lambda b, h, i: (o_tile0 + b * q_tiles_per_b + i, cb0 + h)),
        out_shape=jax.ShapeDtypeStruct((o_rows, o_cols), BF16),
        scratch_shapes=[pltpu.VMEM((g * tq, 1), F32), pltpu.VMEM((g * tq, 1), F32), pltpu.VMEM((g * tq, dv), F32)],
        input_output_aliases=aliases, compiler_params=_cparams(3), name=name,
    )(*ins)


def _na_body(q_ref, k_ref, v_ref, bias_ref, _, o_ref, *, rows_per_step, n_grid_rows, kh, s_len, l_len):
    w = GRID_W
    j = pl.program_id(2)
    kc = k_ref[s_len:s_len + l_len, :]
    vc = v_ref[s_len:s_len + l_len, :]
    nt = (((1,), (1,)), ((), ()))
    for rr in range(rows_per_step):
        r = j * rows_per_step + rr
        start = jnp.clip(r - kh // 2, 0, n_grid_rows - kh)
        koff = pl.multiple_of(start * w, w)
        kw = k_ref[pl.ds(koff, kh * w), :]
        vw = v_ref[pl.ds(koff, kh * w), :]
        q = q_ref[rr * w:(rr + 1) * w, :]
        s_loc = lax.dot_general(q, kw, nt, preferred_element_type=F32) + bias_ref[r - start]
        s_ctx = lax.dot_general(q, kc, nt, preferred_element_type=F32)
        m = jnp.maximum(jnp.max(s_loc, axis=-1, keepdims=True), jnp.max(s_ctx, axis=-1, keepdims=True))
        p_loc = jnp.exp(s_loc - m)
        p_ctx = jnp.exp(s_ctx - m)
        denom = jnp.sum(p_loc, axis=-1, keepdims=True) + jnp.sum(p_ctx, axis=-1, keepdims=True)
        o = (jnp.dot(p_loc.astype(BF16), vw, preferred_element_type=F32)
             + jnp.dot(p_ctx.astype(BF16), vc, preferred_element_type=F32))
        o_ref[rr * w:(rr + 1) * w, :] = (o / denom).astype(o_ref.dtype)


def _na_bias(rpb, kh):
    col = np.arange(GRID_W)
    start_c = np.clip(col - NA_KW // 2, 0, GRID_W - NA_KW)
    col_mask = (col[None, :] >= start_c[:, None]) & (col[None, :] < start_c[:, None] + NA_KW)
    dc_idx = np.clip(col[None, :] - col[:, None] + NA_KW - 1, 0, 2 * NA_KW - 2)
    delta = np.arange(kh)
    dr_idx = np.arange(kh)[None, :] - delta[:, None] + NA_KH - 1
    bias = rpb.astype(F32)[:, dr_idx[:, None, :, None], dc_idx[None, :, None, :]]
    bias = jnp.where(col_mask[None, None, :, None, :], bias, NEG_INF)
    return bias.reshape(rpb.shape[0], kh, GRID_W, kh * GRID_W)


def _na(q, k, v, bias, o_prev, *, dims, o_col0):
    b, s, l, hn = dims["B"], dims["S"], dims["L"], dims["HN"]
    r_tot = s // GRID_W
    kh = min(NA_KH, r_tot)
    rps = NA_ROWS_PER_STEP
    tq = rps * GRID_W
    d = HEAD_DIM
    steps_per_b = r_tot // rps
    cb0 = o_col0 // d
    return pl.pallas_call(
        functools.partial(_na_body, rows_per_step=rps, n_grid_rows=r_tot, kh=kh, s_len=s, l_len=l),
        grid=(b, hn, steps_per_b),
        in_specs=[pl.BlockSpec((None, tq, d), lambda bb, h, j: (h, bb * steps_per_b + j, 0)),
                  pl.BlockSpec((None, None, s + l, d), lambda bb, h, j: (bb, h, 0, 0)),
                  pl.BlockSpec((None, None, s + l, d), lambda bb, h, j: (bb, h, 0, 0)),
                  pl.BlockSpec((None, kh, GRID_W, kh * GRID_W), lambda bb, h, j: (h, 0, 0, 0)),
                  pl.BlockSpec(memory_space=pl.ANY)],
        out_specs=pl.BlockSpec((tq, d), lambda bb, h, j: (bb * steps_per_b + j, cb0 + h)),
        out_shape=jax.ShapeDtypeStruct(o_prev.shape, BF16),
        input_output_aliases={4: 0}, compiler_params=_cparams(3), name="na",
    )(q, k, v, bias, o_prev)


GATHER_ROWS = 256


def _gather_body(idx_ref, src_ref, o_ref, buf, sem):
    base = pl.program_id(0) * GATHER_ROWS

    def row_copy(r, row):
        return pltpu.make_async_copy(src_ref.at[pl.ds(row, 1)], buf.at[pl.ds(r, 1)], sem)

    def issue(r, c):
        row_copy(r, idx_ref[base + r]).start()
        return c

    def wait(r, c):
        row_copy(r, 0).wait()
        return c

    lax.fori_loop(0, GATHER_ROWS, issue, 0)
    lax.fori_loop(0, GATHER_ROWS, wait, 0)
    o_ref[...] = buf[...].astype(o_ref.dtype)


def _gather_rows(src, idx, out_dtype, name):
    n = idx.shape[0]
    d = src.shape[1]
    return pl.pallas_call(
        _gather_body,
        grid_spec=pltpu.PrefetchScalarGridSpec(
            num_scalar_prefetch=1, grid=(n // GATHER_ROWS,),
            in_specs=[pl.BlockSpec(memory_space=pl.ANY)],
            out_specs=pl.BlockSpec((GATHER_ROWS, d), lambda i, idx_ref: (i, 0)),
            scratch_shapes=[pltpu.VMEM((GATHER_ROWS, d), src.dtype), pltpu.SemaphoreType.DMA(())]),
        out_shape=jax.ShapeDtypeStruct((n, d), out_dtype),
        compiler_params=_cparams(1), name=name,
    )(idx, src)


def _moe_body(te_ref, tv_ref, x_ref, w1_ref, w3_ref, w2_ref, gate_ref, o_ref, acc_ref, *, nf):
    i, f = pl.program_id(0), pl.program_id(1)
    valid = tv_ref[i] > 0

    @pl.when(jnp.logical_and(valid, f == 0))
    def _():
        acc_ref[...] = jnp.zeros_like(acc_ref)

    @pl.when(valid)
    def _():
        x = x_ref[...]
        h1 = jnp.dot(x, w1_ref[...], preferred_element_type=F32)
        h3 = jnp.dot(x, w3_ref[...], preferred_element_type=F32)
        hm = (_silu(h1) * h3).astype(BF16)
        acc_ref[...] += jnp.dot(hm, w2_ref[...], preferred_element_type=F32)

    @pl.when(jnp.logical_and(valid, f == nf - 1))
    def _():
        o_ref[...] = acc_ref[...] * gate_ref[...]

    @pl.when(jnp.logical_and(jnp.logical_not(valid), f == nf - 1))
    def _():
        o_ref[...] = jnp.zeros_like(o_ref)


def _moe_ffn(xs, w1, w3, w2, gate_sorted, tile_expert, tile_valid, *, tf):
    tm = MOE_TILE_M
    p_tot, d = xs.shape
    f_dim = w1.shape[-1]
    nf = f_dim // tf

    def fidx(i, f, tv):
        return jnp.where(tv[i] > 0, f, nf - 1)

    return pl.pallas_call(
        functools.partial(_moe_body, nf=nf),
        grid_spec=pltpu.PrefetchScalarGridSpec(
            num_scalar_prefetch=2, grid=(p_tot // tm, nf),
            in_specs=[pl.BlockSpec((tm, d), lambda i, f, te, tv: (i, 0)),
                      pl.BlockSpec((None, d, tf), lambda i, f, te, tv: (te[i], 0, fidx(i, f, tv))),
                      pl.BlockSpec((None, d, tf), lambda i, f, te, tv: (te[i], 0, fidx(i, f, tv))),
                      pl.BlockSpec((None, tf, d), lambda i, f, te, tv: (te[i], fidx(i, f, tv), 0)),
                      pl.BlockSpec((tm, 1), lambda i, f, te, tv: (i, 0))],
            out_specs=pl.BlockSpec((tm, d), lambda i, f, te, tv: (i, 0)),
            scratch_shapes=[pltpu.VMEM((tm, d), F32)]),
        out_shape=jax.ShapeDtypeStruct((p_tot, d), F32),
        compiler_params=_cparams(2), name="moe_ffn",
    )(tile_expert, tile_valid, xs, w1, w3, w2, gate_sorted)


def _moe_plan(e_idx, gates, n_experts, tm):
    t = e_idx.shape[0]
    flat_e = e_idx.reshape(-1)
    n_slots = flat_e.shape[0]
    onehot = (flat_e[:, None] == jnp.arange(n_experts)[None, :]).astype(jnp.int32)
    csum = jnp.cumsum(onehot, axis=0)
    rank = jnp.take_along_axis(csum, flat_e[:, None], axis=1)[:, 0] - 1
    counts = csum[-1]
    tiles_per = (counts + tm - 1) // tm
    cum_tiles = jnp.cumsum(tiles_per)
    row_start = (cum_tiles - tiles_per) * tm
    dest = row_start[flat_e] + rank
    n_tiles = n_slots // tm + n_experts
    p_tot = n_tiles * tm
    te = jnp.searchsorted(cum_tiles, jnp.arange(n_tiles), side="right").astype(jnp.int32)
    tile_valid = (te < n_experts).astype(jnp.int32)
    tile_expert = jnp.minimum(te, n_experts - 1)
    src = jnp.zeros((p_tot,), jnp.int32).at[dest].set(jnp.arange(n_slots, dtype=jnp.int32) // TOP_K)
    gate_sorted = jnp.zeros((p_tot,), F32).at[dest].set(gates.reshape(-1)).reshape(p_tot, 1)
    return src, dest.astype(jnp.int32), gate_sorted, tile_expert, tile_valid


def _rope_tables(n_batch, s_len, l_len):
    t = np.arange(s_len)
    rows, cols = t // GRID_W, t % GRID_W

    def table(width):
        half = width // 2
        quarter = half // 2
        lane = np.arange(width)
        pos = np.where((lane // half)[None, :] == 0, rows[:, None], cols[:, None]).astype(np.float64)
        fi = (lane % half) % quarter
        inv = ROPE_THETA ** (-(2.0 * fi) / half)
        ang = (pos.astype(np.float32) * inv.astype(np.float32)[None, :]).astype(np.float32)
        sign = np.where((lane % half) < quarter, -1.0, 1.0)
        return np.cos(ang).astype(np.float32), (np.sin(ang) * sign[None, :]).astype(np.float32)

    cg, sg = table(HEAD_DIM)
    cm, sm = table(MLA_ROPE)
    cm, sm = np.tile(cm, (1, 2)), np.tile(sm, (1, 2))

    def flat(lat, fill):
        return np.concatenate([np.tile(lat, (n_batch, 1)),
                               np.full((n_batch * l_len, LANES), fill, np.float32)], axis=0)

    return tuple(jnp.asarray(a) for a in (flat(cg, 1.0), flat(sg, 0.0), flat(cm, 1.0), flat(sm, 0.0)))


def _round_up(a, m):
    return (a + m - 1) // m * m


def _pick_tile(n, pref):
    for t in pref:
        if n % t == 0:
            return t
    return n


def kernel(x, c, ctx, c_ctx, w_ada, b_ada, g_pre_mix, g_post_mix, g_pre_ffn, g_post_ffn, w_in, g_q_a, g_kv_a, w_q_up, w_kv_up, g_q_gqa, g_k_gqa, na_rpb, g_grp, w_out, ffn_w1, ffn_w3, ffn_w2, router, moe_w1, moe_w3, moe_w2):
    n_batch, s_len, d = x.shape
    l_len = ctx.shape[1]
    depth = w_ada.shape[0]
    q_lora, kv_lora = g_q_a.shape[-1], g_kv_a.shape[-1]
    dk_m = MLA_NOPE + MLA_ROPE
    hm = w_q_up.shape[-1] // dk_m
    hg = 3 * d // (8 * HEAD_DIM)
    hkv = hg // 3
    hn = d // (4 * HEAD_DIM)
    g_gqa = hg // hkv
    n_lat, n_ctx = n_batch * s_len, n_batch * l_len
    t_tot = n_lat + n_ctx
    w_mla, w_gqa, w_na = hm * MLA_V, hg * HEAD_DIM, hn * HEAD_DIM
    assert s_len % MM_TILE_M == 0 and n_ctx % MM_TILE_M == 0 and l_len % ROW_TILE == 0
    assert hm % 2 == 0 and n_batch + 1 <= SUBLANES

    off_qg = q_lora + kv_lora
    off_kvg = off_qg + hg * HEAD_DIM
    off_n = off_kvg + 2 * hkv * HEAD_DIM
    n_main = off_n + 3 * hn * HEAD_DIM
    dims = dict(B=n_batch, S=s_len, L=l_len, HM=hm, HG=hg, HKV=hkv, HN=hn,
                OFF_QG=off_qg, OFF_KVG=off_kvg, OFF_N=off_n)

    tables = _rope_tables(n_batch, s_len, l_len)
    r_tot = s_len // GRID_W
    kh = min(NA_KH, r_tot)

    mods = _ada(jnp.concatenate([c_ctx[None, :], c], axis=0), w_ada, b_ada)

    lat_tiles, ctx_tiles = n_lat // ROW_TILE, n_ctx // ROW_TILE
    tiles_per_b = s_len // ROW_TILE

    def lat_group(i):
        return 1 + i // tiles_per_b

    def ctx_group(i):
        return 0

    x_lat = x.reshape(n_lat, d)
    x_ctx = ctx.reshape(n_ctx, d)
    n_dense = 0
    n_moe = 0

    def modtab_of(i):
        return mods[i, :n_batch + 1].reshape((n_batch + 1) * 6, 1, d)

    modtab = modtab_of(0)
    nxt0 = (g_pre_mix[0], 0, 1)
    _, h, _ = _row_call(x_lat, n_tiles=lat_tiles, group_fn=lat_group, modtab=modtab, nxt=nxt0,
                        h_rows=t_tot, name="row_in_lat")
    _, h, _ = _row_call(x_ctx, n_tiles=ctx_tiles, group_fn=ctx_group, modtab=modtab, nxt=nxt0,
                        h_prev=h, h_rows=t_tot, h_tile0=lat_tiles, name="row_in_ctx")

    for i in range(depth):
        last = i == depth - 1
        modtab = modtab_of(i)
        m_out = n_lat if last else t_tot

        wi = w_in[i]
        c_kpe = q_lora + kv_lora
        w_main = jnp.concatenate([wi[:, :c_kpe], wi[:, c_kpe + MLA_ROPE:]], axis=1).astype(BF16)
        w_kpe = jnp.zeros((d, LANES), BF16).at[:, :MLA_ROPE].set(wi[:, c_kpe:c_kpe + MLA_ROPE].astype(BF16))
        u = _matmul(h, w_main, m_rows=t_tot, tn=_pick_tile(n_main, (1024, 512, 256, 128)), name="mm_in")
        kpe = _matmul(h, w_kpe, m_rows=t_tot, tn=LANES, name="mm_kpe")
        wq = w_q_up[i].reshape(q_lora, hm, dk_m)
        wq = jnp.concatenate([wq[:, :, :MLA_NOPE].reshape(q_lora, hm * MLA_NOPE),
                              wq[:, :, MLA_NOPE:].reshape(q_lora, hm * MLA_ROPE)], axis=1).astype(BF16)
        a_cols = _round_up(off_qg, LANES)
        qraw = _matmul(u, wq, m_rows=t_tot, tn=_pick_tile(hm * dk_m, (768, 384, 128)), a_block_cols=a_cols,
                       a_lo=0, a_hi=q_lora, rms_g=g_q_a[i], name="mm_qup")
        kvraw = _matmul(u, w_kv_up[i].astype(BF16), m_rows=t_tot,
                        tn=_pick_tile(hm * (MLA_NOPE + MLA_V), (1024, 768, 512, 256)), a_block_cols=a_cols,
                        a_lo=q_lora, a_hi=q_lora + kv_lora, rms_g=g_kv_a[i], name="mm_kvup")
        qm, km, vm, qg, kg, vg, qn, kn, vn = _prep(u, qraw, kvraw, kpe, tables, g_q_gqa[i], g_k_gqa[i], dims=dims)

        tk_all = s_len + l_len
        tk = _pick_tile(tk_all, (768, 512, 256))
        o_cols = w_mla + w_gqa + w_na
        tq_m, tq_g = 256, 128
        o_all = _attention(qm, km, vm, n_batch=n_batch, g=1, tq=tq_m, q_tile0=0, q_tiles_per_b=s_len // tq_m,
                           o_tile0=0, k_rows=tk_all, k_blk=0, tk=tk, o_rows=m_out, o_cols=o_cols, o_col0=0,
                           name="attn_mla")
        o_all = _attention(qg, kg, vg, n_batch=n_batch, g=g_gqa, tq=tq_g, q_tile0=0, q_tiles_per_b=s_len // tq_g,
                           o_tile0=0, k_rows=tk_all, k_blk=0, tk=tk, o_rows=m_out, o_cols=o_cols, o_col0=w_mla,
                           o_prev=o_all, name="attn_gqa")
        o_all = _na(qn, kn, vn, _na_bias(na_rpb[i], kh), o_all, dims=dims, o_col0=w_mla + w_gqa)
        if not last:
            tqc = _pick_tile(l_len, (256, 128))
            for (qq, kk, vv, gg, col0, nm) in ((qm, km, vm, 1, 0, "attn_mla_ctx"),
                                               (qg, kg, vg, g_gqa, w_mla, "attn_gqa_ctx"),
                                               (qn, kn, vn, 1, w_mla + w_gqa, "attn_na_ctx")):
                o_all = _attention(qq, kk, vv, n_batch=n_batch, g=gg, tq=tqc, q_tile0=n_lat // tqc,
                                   q_tiles_per_b=l_len // tqc, o_tile0=n_lat // tqc, k_rows=l_len,
                                   k_blk=s_len // l_len, tk=l_len, o_rows=m_out, o_cols=o_cols, o_col0=col0,
                                   o_prev=o_all, name=nm)

        on = _grpnorm(o_all, g_grp[i], ((0, w_mla), (w_mla, w_mla + w_gqa), (w_mla + w_gqa, o_cols)), m_rows=m_out)
        y = _matmul(on, w_out[i].astype(BF16), m_rows=m_out, tn=_pick_tile(d, (1024, 512, 256)), name="mm_out")

        j = i // 2
        moe = i % 2 == 1
        res = (y, 0, 1, 2, g_post_mix[i])
        nxt = (g_pre_ffn[i], 3, 4)
        if moe:
            x_lat, h_lat, rt_lat = _row_call(x_lat, n_tiles=lat_tiles, group_fn=lat_group, modtab=modtab, res=res,
                                             nxt=nxt, router=router[j], h_dtype=F32, name="row_mix_lat")
            hf, rt = h_lat, rt_lat
            if not last:
                x_ctx, h_c, rt_c = _row_call(x_ctx, n_tiles=ctx_tiles, group_fn=ctx_group, modtab=modtab,
                                             res=(y, lat_tiles, 1, 2, g_post_mix[i]), nxt=nxt, router=router[j],
                                             h_dtype=F32, name="row_mix_ctx")
                hf = jnp.concatenate([h_lat, h_c], axis=0)
                rt = jnp.concatenate([rt_lat, rt_c], axis=0)
            n_experts = router.shape[-1]
            gates = rt[:, 0:TOP_K]
            e_idx = rt[:, TOP_K:2 * TOP_K].astype(jnp.int32)
            src, dest, gate_sorted, tile_expert, tile_valid = _moe_plan(e_idx, gates, n_experts, MOE_TILE_M)
            xs = _gather_rows(hf, src, BF16, "moe_dispatch")
            ys = _moe_ffn(xs, moe_w1[j].astype(BF16), moe_w3[j].astype(BF16), moe_w2[j].astype(BF16),
                          gate_sorted, tile_expert, tile_valid, tf=_pick_tile(moe_w1.shape[-1], (256, 128)))
            f_out = _gather_rows(ys, dest, BF16, "moe_combine").reshape(m_out, TOP_K * d)
            f_parts = TOP_K
        else:
            x_lat, h, _ = _row_call(x_lat, n_tiles=lat_tiles, group_fn=lat_group, modtab=modtab, res=res, nxt=nxt,
                                    h_rows=m_out, name="row_mix_lat")
            if not last:
                x_ctx, h, _ = _row_call(x_ctx, n_tiles=ctx_tiles, group_fn=ctx_group, modtab=modtab,
                                        res=(y, lat_tiles, 1, 2, g_post_mix[i]), nxt=nxt, h_prev=h, h_rows=m_out,
                                        h_tile0=lat_tiles, name="row_mix_ctx")
            f_dim = ffn_w1.shape[-1]
            f_pad = _round_up(f_dim, 1024)
            pad = ((0, 0), (0, f_pad - f_dim))
            w1 = jnp.pad(ffn_w1[j].astype(BF16), pad)
            w3 = jnp.pad(ffn_w3[j].astype(BF16), pad)
            w2 = jnp.pad(ffn_w2[j].astype(BF16), (pad[1], pad[0]))
            gact = _glu(h, w1, w3, m_rows=m_out, tf=1024, name="ffn_glu")
            tk2 = f_pad // 4 if (f_pad // 4) % LANES == 0 else f_pad
            f_out = _matmul_ktiled(gact, w2, m_rows=m_out, tn=_pick_tile(d, (1024, 512, 256)), tk=tk2, name="ffn_down")
            f_parts = 1

        res = (f_out, 0, f_parts, 5, g_post_ffn[i])
        if last:
            x_lat, _, _ = _row_call(x_lat, n_tiles=lat_tiles, group_fn=lat_group, modtab=modtab, res=res,
                                    name="row_ffn_lat")
        else:
            x_lat, _, _ = _row_call(x_lat, n_tiles=lat_tiles, group_fn=lat_group, modtab=modtab, res=res,
                                    name="row_ffn_lat")
            x_ctx, _, _ = _row_call(x_ctx, n_tiles=ctx_tiles, group_fn=ctx_group, modtab=modtab,
                                    res=(f_out, lat_tiles, f_parts, 5, g_post_ffn[i]), name="row_ffn_ctx")
            modtab_n = modtab_of(i + 1)
            nxt_n = (g_pre_mix[i + 1], 0, 1)
            _, h, _ = _row_call(x_lat, n_tiles=lat_tiles, group_fn=lat_group, modtab=modtab_n, nxt=nxt_n,
                                h_rows=t_tot, name="row_next_lat")
            _, h, _ = _row_call(x_ctx, n_tiles=ctx_tiles, group_fn=ctx_group, modtab=modtab_n, nxt=nxt_n,
                                h_prev=h, h_rows=t_tot, h_tile0=lat_tiles, name="row_next_ctx")
    return x_lat.reshape(n_batch, s_len, d)
```

```python
import functools
import math

import jax
import jax.numpy as jnp
import numpy as np
from jax import lax
from jax.experimental import pallas as pl
from jax.experimental.pallas import tpu as pltpu

F32 = jnp.float32
BF16 = jnp.bfloat16

GRID_W = 64
HEAD_DIM = 128
ROPE_THETA = 10000.0
EPS = 1e-6
NEG_INF = -1e30
MLA_NOPE = 128
MLA_ROPE = 64
MLA_V = 128
NA_KH = 8
NA_KW = 16
TOP_K = 2
LOG2E = math.log2(math.e)

LANES = 128
SUBLANES = 8
BF16_SUBLANES = 16
VMEM_LIMIT_BYTES = 56 * 1024 * 1024

ROW_TILE = 256
MM_TILE_M = 512
MOE_TILE_M = 512
V_PAD_ROWS = BF16_SUBLANES


def _cparams(n_axes):
    return pltpu.CompilerParams(dimension_semantics=("arbitrary",) * n_axes,
                                vmem_limit_bytes=VMEM_LIMIT_BYTES)


def _rms(v, g):
    ms = jnp.mean(v * v, axis=-1, keepdims=True)
    return v * lax.rsqrt(ms + EPS) * g


def _silu(v):
    return v * jax.nn.sigmoid(v)


ADA_K_CHUNK = 64


def _ada_body(ct_ref, w_ref, b_ref, o_ref, *, n_rows):
    d, tn = w_ref.shape

    def step(i, accs):
        off = pl.multiple_of(i * ADA_K_CHUNK, ADA_K_CHUNK)
        w = w_ref[pl.ds(off, ADA_K_CHUNK), :]
        s = _silu(ct_ref[pl.ds(off, ADA_K_CHUNK), :])
        out = []
        for r in range(n_rows):
            prod = w * s[:, r:r + 1]
            part = prod[0:SUBLANES]
            for j in range(1, ADA_K_CHUNK // SUBLANES):
                part = part + prod[j * SUBLANES:(j + 1) * SUBLANES]
            out.append(accs[r] + part)
        return tuple(out)

    accs = lax.fori_loop(0, d // ADA_K_CHUNK, step,
                         tuple(jnp.zeros((SUBLANES, tn), F32) for _ in range(n_rows)))
    rows = [jnp.sum(a, axis=0, keepdims=True) for a in accs]
    rows.append(jnp.zeros((SUBLANES - n_rows, tn), F32))
    o_ref[...] = jnp.concatenate(rows, axis=0) + b_ref[...]


def _ada(cond, w_ada, b_ada):
    n_rows, d = cond.shape
    depth, _, n = w_ada.shape
    tn = 512
    ct = jnp.zeros((d, SUBLANES), F32).at[:, :n_rows].set(cond.T)
    return pl.pallas_call(
        functools.partial(_ada_body, n_rows=n_rows),
        grid=(depth, n // tn),
        in_specs=[pl.BlockSpec((d, SUBLANES), lambda l, j: (0, 0)),
                  pl.BlockSpec((None, d, tn), lambda l, j: (l, 0, j)),
                  pl.BlockSpec((None, 1, tn), lambda l, j: (l, 0, j))],
        out_specs=pl.BlockSpec((None, SUBLANES, tn), lambda l, j: (l, 0, j)),
        out_shape=jax.ShapeDtypeStruct((depth, SUBLANES, n), F32),
        compiler_params=_cparams(2),
        name="ada",
    )(ct, w_ada, b_ada.reshape(depth, 1, n))


def _row_body(*refs, n_x, x_split, n_y, has_next, router, n_experts):
    it = iter(refs)
    x_refs = [next(it) for _ in range(n_x)]
    has_res = n_y > 0
    if has_res:
        y_refs = [next(it) for _ in range(n_y)]
        gate_ref, gpost_ref = next(it), next(it)
    if has_next:
        gpre_ref, shift_ref, scale_ref = next(it), next(it), next(it)
    if router:
        rhi_ref, rlo_ref = next(it), next(it)
    if has_res:
        xo_ref = next(it)
    if has_next:
        h_ref = next(it)
    if router:
        rt_ref = next(it)

    if n_x == 1:
        x = x_refs[0][...]
    else:
        x = jnp.where(pl.program_id(0) < x_split, x_refs[0][...], x_refs[1][...])
    if has_res:
        y = y_refs[0][...].astype(F32)
        for y_ref in y_refs[1:]:
            y = y + y_ref[...].astype(F32)
        x = x + gate_ref[...] * _rms(y, gpost_ref[...])
        xo_ref[...] = x
    if has_next:
        h = _rms(x, gpre_ref[...]) * (1.0 + scale_ref[...]) + shift_ref[...]
        h_ref[...] = h.astype(h_ref.dtype)
    if router:
        hi = h.astype(BF16)
        lo = (h - hi.astype(F32)).astype(BF16)
        logits = (jnp.dot(hi, rhi_ref[...], preferred_element_type=F32)
                  + jnp.dot(hi, rlo_ref[...], preferred_element_type=F32)
                  + jnp.dot(lo, rhi_ref[...], preferred_element_type=F32))
        lane = lax.broadcasted_iota(jnp.int32, logits.shape, 1)
        lg = jnp.where(lane < n_experts, logits, -jnp.inf)
        v1 = jnp.max(lg, axis=-1, keepdims=True)
        i1 = jnp.min(jnp.where(lg == v1, lane, LANES), axis=-1, keepdims=True)
        lg2 = jnp.where(lane == i1, -jnp.inf, lg)
        v2 = jnp.max(lg2, axis=-1, keepdims=True)
        i2 = jnp.min(jnp.where(lg2 == v2, lane, LANES), axis=-1, keepdims=True)
        e = jnp.exp(v2 - v1)
        g1 = 1.0 / (1.0 + e)
        g2 = e / (1.0 + e)
        rt_ref[...] = jnp.where(lane == 0, g1,
                                jnp.where(lane == 1, g2,
                                          jnp.where(lane == 2, i1.astype(F32),
                                                    jnp.where(lane == 3, i2.astype(F32), 0.0))))


def _row_call(xs, *, n_tiles, group_fn, modtab, res=None, nxt=None, router=None, h_dtype=BF16, name="row"):
    tm = ROW_TILE
    d = xs[0].shape[-1]
    has_res, has_next = res is not None, nxt is not None

    def modspec(k):
        return pl.BlockSpec((None, 1, d), lambda i: (group_fn(i) * 6 + k, 0, 0))

    vecspec = pl.BlockSpec((1, d), lambda i: (0, 0))
    x_split = 0
    if len(xs) == 1:
        ins, in_specs = [xs[0]], [pl.BlockSpec((tm, d), lambda i: (i, 0))]
    else:
        x_split = xs[0].shape[0] // tm
        ins = list(xs)
        in_specs = [pl.BlockSpec((tm, d), lambda i: (jnp.minimum(i, x_split - 1), 0)),
                    pl.BlockSpec((tm, d), lambda i: (jnp.maximum(i - x_split, 0), 0))]
    n_y = 0
    if has_res:
        y_blocks, gate_k, g_post = res
        n_y = len(y_blocks)
        for arr, tile0 in y_blocks:
            ins.append(arr)
            in_specs.append(pl.BlockSpec((tm, d), lambda i, tile0=tile0: (i + tile0, 0)))
        ins += [modtab, g_post.reshape(1, d)]
        in_specs += [modspec(gate_k), vecspec]
    if has_next:
        g_pre, shift_k, scale_k, modtab_n = nxt
        ins += [g_pre.reshape(1, d), modtab_n, modtab_n]
        in_specs += [vecspec, modspec(shift_k), modspec(scale_k)]
    n_experts = 0
    if router is not None:
        n_experts = router.shape[-1]
        rp = jnp.zeros((d, LANES), F32).at[:, :n_experts].set(router)
        rhi = rp.astype(BF16)
        rlo = (rp - rhi.astype(F32)).astype(BF16)
        ins += [rhi, rlo]
        in_specs += [pl.BlockSpec((d, LANES), lambda i: (0, 0))] * 2
    out_shape, out_specs = [], []
    rowspec = pl.BlockSpec((tm, d), lambda i: (i, 0))
    if has_res:
        out_shape.append(jax.ShapeDtypeStruct((n_tiles * tm, d), F32))
        out_specs.append(rowspec)
    if has_next:
        out_shape.append(jax.ShapeDtypeStruct((n_tiles * tm, d), h_dtype))
        out_specs.append(rowspec)
    if router is not None:
        out_shape.append(jax.ShapeDtypeStruct((n_tiles * tm, LANES), F32))
        out_specs.append(pl.BlockSpec((tm, LANES), lambda i: (i, 0)))
    outs = pl.pallas_call(
        functools.partial(_row_body, n_x=len(xs), x_split=x_split, n_y=n_y, has_next=has_next,
                          router=router is not None, n_experts=n_experts),
        grid=(n_tiles,), in_specs=in_specs, out_specs=out_specs, out_shape=out_shape,
        compiler_params=_cparams(1), name=name,
    )(*ins)
    outs = list(outs)
    x_new = outs.pop(0) if has_res else None
    h = outs.pop(0) if has_next else None
    rt = outs.pop(0) if router is not None else None
    return x_new, h, rt


def _mm_body(*refs, a_lo, a_hi, rms):
    if rms:
        a_ref, g_ref, w_ref, o_ref = refs
    else:
        a_ref, w_ref, o_ref = refs
    a = a_ref[:, a_lo:a_hi]
    if rms:
        a = _rms(a.astype(F32), g_ref[...]).astype(BF16)
    o_ref[...] = jnp.dot(a, w_ref[...], preferred_element_type=F32).astype(o_ref.dtype)


def _matmul(a, w, *, m_rows, tn, out_dtype=BF16, a_block_cols=None, a_lo=0, a_hi=None, rms_g=None, name="mm"):
    tm = MM_TILE_M
    k, n = w.shape
    if a_block_cols is None:
        a_block_cols = a.shape[1]
    if a_hi is None:
        a_hi = a_lo + k
    ins, in_specs = [a], [pl.BlockSpec((tm, a_block_cols), lambda j, i: (i, 0))]
    if rms_g is not None:
        ins.append(rms_g.reshape(1, k))
        in_specs.append(pl.BlockSpec((1, k), lambda j, i: (0, 0)))
    ins.append(w)
    in_specs.append(pl.BlockSpec((k, tn), lambda j, i: (0, j)))
    return pl.pallas_call(
        functools.partial(_mm_body, a_lo=a_lo, a_hi=a_hi, rms=rms_g is not None),
        grid=(n // tn, m_rows // tm), in_specs=in_specs,
        out_specs=pl.BlockSpec((tm, tn), lambda j, i: (i, j)),
        out_shape=jax.ShapeDtypeStruct((m_rows, n), out_dtype),
        compiler_params=_cparams(2), name=name,
    )(*ins)


def _mm_groups_body(*refs, n_groups):
    a_refs = refs[:n_groups]
    g_ref, w_ref, o_ref = refs[n_groups:]
    parts, lo = [], 0
    for a_ref in a_refs:
        width = a_ref.shape[1]
        parts.append(_rms(a_ref[...].astype(F32), g_ref[:, lo:lo + width]).astype(BF16))
        lo += width
    a = jnp.concatenate(parts, axis=1)
    o_ref[...] = jnp.dot(a, w_ref[...], preferred_element_type=F32).astype(o_ref.dtype)


def _matmul_groups(groups, g, w, *, m_rows, tn, name):
    tm = MM_TILE_M
    k, n = w.shape
    in_specs = [pl.BlockSpec((tm, a.shape[1]), lambda j, i: (i, 0)) for a in groups]
    in_specs += [pl.BlockSpec((1, k), lambda j, i: (0, 0)), pl.BlockSpec((k, tn), lambda j, i: (0, j))]
    return pl.pallas_call(
        functools.partial(_mm_groups_body, n_groups=len(groups)),
        grid=(n // tn, m_rows // tm), in_specs=in_specs,
        out_specs=pl.BlockSpec((tm, tn), lambda j, i: (i, j)),
        out_shape=jax.ShapeDtypeStruct((m_rows, n), BF16),
        compiler_params=_cparams(2), name=name,
    )(*groups, g.reshape(1, k), w)


def _mmk_body(a_ref, w_ref, o_ref, acc_ref, *, nk):
    kk = pl.program_id(2)

    @pl.when(kk == 0)
    def _():
        acc_ref[...] = jnp.zeros_like(acc_ref)

    acc_ref[...] += jnp.dot(a_ref[...], w_ref[...], preferred_element_type=F32)

    @pl.when(kk == nk - 1)
    def _():
        o_ref[...] = acc_ref[...].astype(o_ref.dtype)


def _matmul_ktiled(a, w, *, m_rows, tn, tk, out_dtype=BF16, name="mmk"):
    tm = MM_TILE_M
    k, n = w.shape
    nk = k // tk
    return pl.pallas_call(
        functools.partial(_mmk_body, nk=nk),
        grid=(n // tn, m_rows // tm, nk),
        in_specs=[pl.BlockSpec((tm, tk), lambda j, i, kk: (i, kk)),
                  pl.BlockSpec((tk, tn), lambda j, i, kk: (kk, j))],
        out_specs=pl.BlockSpec((tm, tn), lambda j, i, kk: (i, j)),
        out_shape=jax.ShapeDtypeStruct((m_rows, n), out_dtype),
        scratch_shapes=[pltpu.VMEM((tm, tn), F32)],
        compiler_params=_cparams(3), name=name,
    )(a, w)


def _glu_body(a_ref, w1_ref, w3_ref, o_ref):
    a = a_ref[...]
    h1 = jnp.dot(a, w1_ref[...], preferred_element_type=F32)
    h3 = jnp.dot(a, w3_ref[...], preferred_element_type=F32)
    o_ref[...] = (_silu(h1) * h3).astype(o_ref.dtype)


def _glu(a, w1, w3, *, m_rows, tf, name="glu"):
    tm = MM_TILE_M
    k, f = w1.shape
    return pl.pallas_call(
        _glu_body, grid=(f // tf, m_rows // tm),
        in_specs=[pl.BlockSpec((tm, k), lambda j, i: (i, 0)),
                  pl.BlockSpec((k, tf), lambda j, i: (0, j)),
                  pl.BlockSpec((k, tf), lambda j, i: (0, j))],
        out_specs=pl.BlockSpec((tm, tf), lambda j, i: (i, j)),
        out_shape=jax.ShapeDtypeStruct((m_rows, f), BF16),
        compiler_params=_cparams(2), name=name,
    )(a, w1, w3)


def _rope(x, c, s, half):
    lane = lax.broadcasted_iota(jnp.int32, x.shape, 1)
    first = (lane % (2 * half)) < half
    swapped = jnp.where(first, pltpu.roll(x, LANES - half, 1), pltpu.roll(x, half, 1))
    return x * c + swapped * s


def _prep_body(u_ref, qraw_ref, kvraw_ref, kpe_ref, cg_ref, sg_ref, cm_ref, sm_ref, gq_ref, gk_ref,
               qm_o, km_o, vm_o, qg_o, kg_o, vg_o, qn_o, kn_o, vn_o,
               *, hm, hg, hkv, hn, off_qg, off_kvg, off_n):
    cg, sg, cm, sm = cg_ref[...], sg_ref[...], cm_ref[...], sm_ref[...]
    scale_m = (MLA_NOPE + MLA_ROPE) ** -0.5 * LOG2E
    scale_g = HEAD_DIM ** -0.5 * LOG2E
    scale_n = HEAD_DIM ** -0.5
    d = HEAD_DIM
    tm = u_ref.shape[0]
    ones_rows = jnp.where(lax.broadcasted_iota(jnp.int32, (V_PAD_ROWS, tm), 0) == 0, 1.0, 0.0).astype(BF16)

    def t_bf16(v):
        return v.astype(F32).T.astype(BF16)

    kpe = _rope(kpe_ref[...].astype(F32), cm, sm, MLA_ROPE // 4)[:, :MLA_ROPE].astype(BF16)
    pe0 = hm * MLA_NOPE
    for h in range(hm):
        qm_o[h, 0:MLA_NOPE, :] = t_bf16(qraw_ref[:, h * MLA_NOPE:(h + 1) * MLA_NOPE].astype(F32) * scale_m)
        km_o[h, :, 0:MLA_NOPE] = kvraw_ref[:, h * 2 * d:h * 2 * d + MLA_NOPE]
        km_o[h, :, MLA_NOPE:MLA_NOPE + MLA_ROPE] = kpe
        vm_o[h, 0:MLA_V, :] = t_bf16(kvraw_ref[:, h * 2 * d + MLA_NOPE:(h + 1) * 2 * d])
        vm_o[h, MLA_V:MLA_V + V_PAD_ROWS, :] = ones_rows
    for j in range(hm // 2):
        pe = _rope(qraw_ref[:, pe0 + j * LANES:pe0 + (j + 1) * LANES].astype(F32), cm, sm, MLA_ROPE // 4)
        pe_t = t_bf16(pe * scale_m)
        qm_o[2 * j, MLA_NOPE:MLA_NOPE + MLA_ROPE, :] = pe_t[:MLA_ROPE]
        qm_o[2 * j + 1, MLA_NOPE:MLA_NOPE + MLA_ROPE, :] = pe_t[MLA_ROPE:]
    for h in range(hg):
        q = _rms(u_ref[:, off_qg + h * d:off_qg + (h + 1) * d].astype(F32), gq_ref[...])
        qg_o[h] = t_bf16(_rope(q, cg, sg, d // 4) * scale_g)
    for h in range(hkv):
        k = _rms(u_ref[:, off_kvg + h * d:off_kvg + (h + 1) * d].astype(F32), gk_ref[...])
        kg_o[h] = _rope(k, cg, sg, d // 4).astype(BF16)
        vg_o[h, 0:d, :] = t_bf16(u_ref[:, off_kvg + (hkv + h) * d:off_kvg + (hkv + h + 1) * d])
        vg_o[h, d:d + V_PAD_ROWS, :] = ones_rows
    for h in range(hn):
        qn_o[h] = (u_ref[:, off_n + h * d:off_n + (h + 1) * d].astype(F32) * scale_n).astype(BF16)
        kn_o[h] = u_ref[:, off_n + (hn + h) * d:off_n + (hn + h + 1) * d]
        vn_o[h] = u_ref[:, off_n + (2 * hn + h) * d:off_n + (2 * hn + h + 1) * d]


def _prep(u, qraw, kvraw, kpe, tables, g_q, g_k, *, dims, tk_m, tk_g):
    tm = ROW_TILE
    t_tot = u.shape[0]
    b, s, l = dims["B"], dims["S"], dims["L"]
    hm, hg, hkv, hn = dims["HM"], dims["HG"], dims["HKV"], dims["HN"]
    tk_len = s + l
    n_lat = b * s // tm
    spb, lpb = s // tm, l // tm

    def kmap(i):
        lat = i < n_lat
        j = i - n_lat
        bb = jnp.where(lat, i // spb, j // lpb)
        pos = jnp.where(lat, i % spb, spb + j % lpb)
        return bb, pos

    def kspec(h, dk):
        def im(i):
            bb, pos = kmap(i)
            return (bb, 0, pos, 0)
        return pl.BlockSpec((None, h, tm, dk), im)

    def vtspec(h, dv, tk):
        per = tk // tm

        def im(i):
            bb, pos = kmap(i)
            return (bb, 0, pos // per, 0, pos % per)
        return pl.BlockSpec((None, h, None, dv + V_PAD_ROWS, tm), im)

    def qtspec(h, dk):
        return pl.BlockSpec((h, dk, tm), lambda i: (0, 0, i))

    def qspec(h, dk):
        return pl.BlockSpec((h, tm, dk), lambda i: (0, i, 0))

    def full(arr):
        return pl.BlockSpec((tm, arr.shape[1]), lambda i: (i, 0))

    dk_m = MLA_NOPE + MLA_ROPE
    d = HEAD_DIM
    out_shape = [
        jax.ShapeDtypeStruct((hm, dk_m, t_tot), BF16), jax.ShapeDtypeStruct((b, hm, tk_len, dk_m), BF16),
        jax.ShapeDtypeStruct((b, hm, tk_len // tk_m, MLA_V + V_PAD_ROWS, tk_m), BF16),
        jax.ShapeDtypeStruct((hg, d, t_tot), BF16), jax.ShapeDtypeStruct((b, hkv, tk_len, d), BF16),
        jax.ShapeDtypeStruct((b, hkv, tk_len // tk_g, d + V_PAD_ROWS, tk_g), BF16),
        jax.ShapeDtypeStruct((hn, t_tot, d), BF16), jax.ShapeDtypeStruct((b, hn, tk_len, d), BF16),
        jax.ShapeDtypeStruct((b, hn, tk_len, d), BF16),
    ]
    out_specs = [qtspec(hm, dk_m), kspec(hm, dk_m), vtspec(hm, MLA_V, tk_m),
                 qtspec(hg, d), kspec(hkv, d), vtspec(hkv, d, tk_g),
                 qspec(hn, d), kspec(hn, d), kspec(hn, d)]
    vec = pl.BlockSpec((1, d), lambda i: (0, 0))
    return pl.pallas_call(
        functools.partial(_prep_body, hm=hm, hg=hg, hkv=hkv, hn=hn,
                          off_qg=dims["OFF_QG"], off_kvg=dims["OFF_KVG"], off_n=dims["OFF_N"]),
        grid=(t_tot // tm,),
        in_specs=[full(u), full(qraw), full(kvraw), full(kpe)] + [full(t) for t in tables] + [vec, vec],
        out_specs=out_specs, out_shape=out_shape,
        compiler_params=_cparams(1), name="prep",
    )(u, qraw, kvraw, kpe, *tables, g_q.reshape(1, d), g_k.reshape(1, d))


M_INIT = -1e30


def _attn_body(qt_ref, k_ref, vt_ref, o_ref, s0, s1, p0, p1, a0, a1, m_sc, acc_sc,
               *, g, tq, tk, nk, dv, n_lat_tiles, ctx_rows):
    s_buf, p_buf, a_buf = (s0, s1), (p0, p1), (a0, a1)
    if g == 1:
        qt = qt_ref[0]
    else:
        qt = jnp.concatenate([qt_ref[gi] for gi in range(g)], axis=1)

    def finish():
        acc = acc_sc[...]
        o_t = acc[0:dv] / acc[dv:dv + 1]
        for gi in range(g):
            o_ref[:, gi * dv:(gi + 1) * dv] = o_t[:, gi * tq:(gi + 1) * tq].T.astype(o_ref.dtype)

    def scores(c):
        off = pl.multiple_of(c * tk, tk)
        return jnp.dot(k_ref[pl.ds(off, tk), :], qt, preferred_element_type=F32)

    def softmax(slot):
        s = s_buf[slot][...]
        m_prev = m_sc[...]
        m_new = jnp.maximum(m_prev, jnp.max(s, axis=0, keepdims=True))
        p_buf[slot][...] = jnp.exp2(s - m_new).astype(BF16)
        a_buf[slot][...] = jnp.exp2(m_prev - m_new)
        m_sc[...] = m_new

    def values(c, slot):
        acc_sc[...] = (a_buf[slot][...] * acc_sc[...]
                       + jnp.dot(vt_ref[c], p_buf[slot][...], preferred_element_type=F32))

    def stage(c, slot):
        s_buf[1 - slot][...] = scores(c + 1)
        softmax(slot)
        values(c - 1, 1 - slot)

    def latent_tile():
        m_sc[...] = jnp.full_like(m_sc, M_INIT)
        acc_sc[...] = jnp.zeros_like(acc_sc)
        s_buf[0][...] = scores(0)
        softmax(0)
        if nk > 1:
            s_buf[1][...] = scores(1)
            n_steady = nk - 2

            def pair(t, carry):
                c = 2 * t + 1
                stage(c, 1)
                stage(c + 1, 0)
                return carry

            lax.fori_loop(0, n_steady // 2, pair, 0)
            if n_steady % 2:
                stage(nk - 2, 1)
            softmax((nk - 1) % 2)
            values(nk - 2, (nk - 2) % 2)
        values(nk - 1, (nk - 1) % 2)
        finish()

    def ctx_tile():
        t_k = k_ref.shape[0]
        s = jnp.dot(k_ref[t_k - ctx_rows:t_k, :], qt, preferred_element_type=F32)
        p = jnp.exp2(s - jnp.max(s, axis=0, keepdims=True)).astype(BF16)
        acc_sc[...] = jnp.dot(vt_ref[nk - 1][:, tk - ctx_rows:tk], p, preferred_element_type=F32)
        finish()

    if ctx_rows:
        i = pl.program_id(2)
        pl.when(i < n_lat_tiles)(latent_tile)
        pl.when(i >= n_lat_tiles)(ctx_tile)
    else:
        latent_tile()


def _attention(qt, k, vt, *, dims, g, tq, with_ctx, name):
    b, s_len, l_len = dims["B"], dims["S"], dims["L"]
    hk = k.shape[1]
    dk = qt.shape[1]
    nk, dvp, tk = vt.shape[2:]
    dv = dvp - V_PAD_ROWS
    spb = s_len // tq
    cpb = l_len // tq if with_ctx else 0
    lat_tiles = b * spb
    rows = b * (s_len + (l_len if with_ctx else 0))
    n = g * tq
    if with_ctx:
        assert l_len % tq == 0 and l_len <= tk

    def qtile(bb, i):
        return jnp.where(i < spb, bb * spb + i, lat_tiles + bb * cpb + (i - spb))

    return pl.pallas_call(
        functools.partial(_attn_body, g=g, tq=tq, tk=tk, nk=nk, dv=dv, n_lat_tiles=spb,
                          ctx_rows=l_len if with_ctx else 0),
        grid=(b, hk, spb + cpb),
        in_specs=[pl.BlockSpec((g, dk, tq), lambda bb, h, i: (h, 0, qtile(bb, i))),
                  pl.BlockSpec((None, None, s_len + l_len, dk), lambda bb, h, i: (bb, h, 0, 0)),
                  pl.BlockSpec((None, None, nk, dvp, tk), lambda bb, h, i: (bb, h, 0, 0, 0))],
        out_specs=pl.BlockSpec((tq, g * dv), lambda bb, h, i: (qtile(bb, i), h)),
        out_shape=jax.ShapeDtypeStruct((rows, hk * g * dv), BF16),
        scratch_shapes=[pltpu.VMEM((tk, n), F32), pltpu.VMEM((tk, n), F32),
                        pltpu.VMEM((tk, n), BF16), pltpu.VMEM((tk, n), BF16),
                        pltpu.VMEM((1, n), F32), pltpu.VMEM((1, n), F32),
                        pltpu.VMEM((1, n), F32), pltpu.VMEM((dvp, n), F32)],
        compiler_params=_cparams(3), name=name,
    )(qt, k, vt)


def _na_body(q_ref, k_ref, v_ref, bias_ref, o_ref, *, rows_per_step, n_grid_rows, kh, s_len, l_len, lat_steps,
             with_ctx):
    w = GRID_W
    j = pl.program_id(2)
    kc = k_ref[s_len:s_len + l_len, :]
    vc = v_ref[s_len:s_len + l_len, :]
    nt = (((1,), (1,)), ((), ()))

    def latent_step():
        for rr in range(rows_per_step):
            r = j * rows_per_step + rr
            start = jnp.clip(r - kh // 2, 0, n_grid_rows - kh)
            koff = pl.multiple_of(start * w, w)
            kw = k_ref[pl.ds(koff, kh * w), :]
            vw = v_ref[pl.ds(koff, kh * w), :]
            q = q_ref[rr * w:(rr + 1) * w, :]
            s_loc = lax.dot_general(q, kw, nt, preferred_element_type=F32) + bias_ref[r - start]
            s_ctx = lax.dot_general(q, kc, nt, preferred_element_type=F32)
            m = jnp.maximum(jnp.max(s_loc, axis=-1, keepdims=True), jnp.max(s_ctx, axis=-1, keepdims=True))
            p_loc = jnp.exp(s_loc - m)
            p_ctx = jnp.exp(s_ctx - m)
            denom = jnp.sum(p_loc, axis=-1, keepdims=True) + jnp.sum(p_ctx, axis=-1, keepdims=True)
            o = (jnp.dot(p_loc.astype(BF16), vw, preferred_element_type=F32)
                 + jnp.dot(p_ctx.astype(BF16), vc, preferred_element_type=F32))
            o_ref[rr * w:(rr + 1) * w, :] = (o / denom).astype(o_ref.dtype)

    def ctx_step():
        s = lax.dot_general(q_ref[...], kc, nt, preferred_element_type=F32)
        p = jnp.exp(s - jnp.max(s, axis=-1, keepdims=True))
        o = jnp.dot(p.astype(BF16), vc, preferred_element_type=F32)
        o_ref[...] = (o / jnp.sum(p, axis=-1, keepdims=True)).astype(o_ref.dtype)

    if with_ctx:
        pl.when(j < lat_steps)(latent_step)
        pl.when(j >= lat_steps)(ctx_step)
    else:
        latent_step()


def _na_bias(rpb, kh):
    col = np.arange(GRID_W)
    start_c = np.clip(col - NA_KW // 2, 0, GRID_W - NA_KW)
    col_mask = (col[None, :] >= start_c[:, None]) & (col[None, :] < start_c[:, None] + NA_KW)
    dc_idx = np.clip(col[None, :] - col[:, None] + NA_KW - 1, 0, 2 * NA_KW - 2)
    dr_idx = np.arange(kh)[None, :] - np.arange(kh)[:, None] + NA_KH - 1
    oh_r = (dr_idx[:, :, None] == np.arange(2 * NA_KH - 1)[None, None, :]).astype(np.float32)
    oh_c = (dc_idx[:, :, None] == np.arange(2 * NA_KW - 1)[None, None, :]).astype(np.float32)
    bias = jnp.einsum("dja,hab,qkb->hdqjk", oh_r, rpb.astype(F32), oh_c, precision=lax.Precision.HIGHEST)
    bias = jnp.where(col_mask[None, None, :, None, :], bias, NEG_INF)
    return bias.reshape(rpb.shape[0], kh, GRID_W, kh * GRID_W)


def _na(q, k, v, bias, *, dims, with_ctx):
    b, s, l, hn = dims["B"], dims["S"], dims["L"], dims["HN"]
    r_tot = s // GRID_W
    kh = min(NA_KH, r_tot)
    tq = ROW_TILE
    rps = tq // GRID_W
    d = HEAD_DIM
    spb = r_tot // rps
    cpb = l // tq if with_ctx else 0
    lat_tiles = b * spb
    rows = b * (s + (l if with_ctx else 0))

    def qtile(bb, j):
        return jnp.where(j < spb, bb * spb + j, lat_tiles + bb * cpb + (j - spb))

    return pl.pallas_call(
        functools.partial(_na_body, rows_per_step=rps, n_grid_rows=r_tot, kh=kh, s_len=s, l_len=l, lat_steps=spb,
                          with_ctx=with_ctx),
        grid=(b, hn, spb + cpb),
        in_specs=[pl.BlockSpec((None, tq, d), lambda bb, h, j: (h, qtile(bb, j), 0)),
                  pl.BlockSpec((None, None, s + l, d), lambda bb, h, j: (bb, h, 0, 0)),
                  pl.BlockSpec((None, None, s + l, d), lambda bb, h, j: (bb, h, 0, 0)),
                  pl.BlockSpec((None, kh, GRID_W, kh * GRID_W), lambda bb, h, j: (h, 0, 0, 0))],
        out_specs=pl.BlockSpec((tq, d), lambda bb, h, j: (qtile(bb, j), h)),
        out_shape=jax.ShapeDtypeStruct((rows, hn * d), BF16),
        compiler_params=_cparams(3), name="na",
    )(q, k, v, bias)


GATHER_ROWS = 256


def _gather_body(idx_ref, src_ref, o_ref, buf, sem):
    base = pl.program_id(0) * GATHER_ROWS

    def row_copy(r, row):
        return pltpu.make_async_copy(src_ref.at[pl.ds(row, 1)], buf.at[pl.ds(r, 1)], sem)

    def issue(r, c):
        row_copy(r, idx_ref[base + r]).start()
        return c

    def wait(r, c):
        row_copy(r, 0).wait()
        return c

    lax.fori_loop(0, GATHER_ROWS, issue, 0)
    lax.fori_loop(0, GATHER_ROWS, wait, 0)
    o_ref[...] = buf[...].astype(o_ref.dtype)


def _gather_rows(src, idx, out_dtype, name):
    n = idx.shape[0]
    d = src.shape[1]
    return pl.pallas_call(
        _gather_body,
        grid_spec=pltpu.PrefetchScalarGridSpec(
            num_scalar_prefetch=1, grid=(n // GATHER_ROWS,),
            in_specs=[pl.BlockSpec(memory_space=pl.ANY)],
            out_specs=pl.BlockSpec((GATHER_ROWS, d), lambda i, idx_ref: (i, 0)),
            scratch_shapes=[pltpu.VMEM((GATHER_ROWS, d), src.dtype), pltpu.SemaphoreType.DMA(())]),
        out_shape=jax.ShapeDtypeStruct((n, d), out_dtype),
        compiler_params=_cparams(1), name=name,
    )(idx, src)


def _moe_body(te_ref, tv_ref, x_ref, w1_ref, w3_ref, w2_ref, gate_ref, o_ref, acc_ref, *, nf):
    i, f = pl.program_id(0), pl.program_id(1)
    valid = tv_ref[i] > 0

    @pl.when(jnp.logical_and(valid, f == 0))
    def _():
        acc_ref[...] = jnp.zeros_like(acc_ref)

    @pl.when(valid)
    def _():
        x = x_ref[...]
        h1 = jnp.dot(x, w1_ref[...], preferred_element_type=F32)
        h3 = jnp.dot(x, w3_ref[...], preferred_element_type=F32)
        hm = (_silu(h1) * h3).astype(BF16)
        acc_ref[...] += jnp.dot(hm, w2_ref[...], preferred_element_type=F32)

    @pl.when(jnp.logical_and(valid, f == nf - 1))
    def _():
        o_ref[...] = acc_ref[...] * gate_ref[...]

    @pl.when(jnp.logical_and(jnp.logical_not(valid), f == nf - 1))
    def _():
        o_ref[...] = jnp.zeros_like(o_ref)


def _moe_ffn(xs, w1, w3, w2, gate_sorted, tile_expert, tile_valid, *, tf):
    tm = MOE_TILE_M
    p_tot, d = xs.shape
    f_dim = w1.shape[-1]
    nf = f_dim // tf

    def fidx(i, f, tv):
        return jnp.where(tv[i] > 0, f, nf - 1)

    return pl.pallas_call(
        functools.partial(_moe_body, nf=nf),
        grid_spec=pltpu.PrefetchScalarGridSpec(
            num_scalar_prefetch=2, grid=(p_tot // tm, nf),
            in_specs=[pl.BlockSpec((tm, d), lambda i, f, te, tv: (i, 0)),
                      pl.BlockSpec((None, d, tf), lambda i, f, te, tv: (te[i], 0, fidx(i, f, tv))),
                      pl.BlockSpec((None, d, tf), lambda i, f, te, tv: (te[i], 0, fidx(i, f, tv))),
                      pl.BlockSpec((None, tf, d), lambda i, f, te, tv: (te[i], fidx(i, f, tv), 0)),
                      pl.BlockSpec((tm, 1), lambda i, f, te, tv: (i, 0))],
            out_specs=pl.BlockSpec((tm, d), lambda i, f, te, tv: (i, 0)),
            scratch_shapes=[pltpu.VMEM((tm, d), F32)]),
        out_shape=jax.ShapeDtypeStruct((p_tot, d), F32),
        compiler_params=_cparams(2), name="moe_ffn",
    )(tile_expert, tile_valid, xs, w1, w3, w2, gate_sorted)


def _moe_plan(e_idx, gates, n_experts, tm):
    flat_e = e_idx.reshape(-1)
    n_slots = flat_e.shape[0]
    onehot = (flat_e[:, None] == jnp.arange(n_experts)[None, :]).astype(jnp.int32)
    csum = jnp.cumsum(onehot, axis=0)
    rank = jnp.sum(csum * onehot, axis=1) - 1
    counts = csum[-1]
    tiles_per = (counts + tm - 1) // tm
    cum_tiles = jnp.cumsum(tiles_per)
    row_start = (cum_tiles - tiles_per) * tm
    dest = jnp.sum(row_start[None, :] * onehot, axis=1) + rank
    n_tiles = n_slots // tm + n_experts
    p_tot = n_tiles * tm
    te = jnp.sum((jnp.arange(n_tiles)[:, None] >= cum_tiles[None, :]).astype(jnp.int32), axis=1)
    tile_valid = (te < n_experts).astype(jnp.int32)
    tile_expert = jnp.minimum(te, n_experts - 1)
    src = jnp.zeros((p_tot,), jnp.int32).at[dest].set(jnp.arange(n_slots, dtype=jnp.int32) // TOP_K)
    gate_sorted = jnp.zeros((p_tot,), F32).at[dest].set(gates.reshape(-1)).reshape(p_tot, 1)
    return src, dest.astype(jnp.int32), gate_sorted, tile_expert, tile_valid


def _rope_tables(n_batch, s_len, l_len):
    t = np.arange(s_len)
    rows, cols = t // GRID_W, t % GRID_W

    def table(width):
        half = width // 2
        quarter = half // 2
        lane = np.arange(width)
        pos = np.where((lane // half)[None, :] == 0, rows[:, None], cols[:, None]).astype(np.float64)
        fi = (lane % half) % quarter
        inv = ROPE_THETA ** (-(2.0 * fi) / half)
        ang = (pos.astype(np.float32) * inv.astype(np.float32)[None, :]).astype(np.float32)
        sign = np.where((lane % half) < quarter, -1.0, 1.0)
        return np.cos(ang).astype(np.float32), (np.sin(ang) * sign[None, :]).astype(np.float32)

    cg, sg = table(HEAD_DIM)
    cm, sm = table(MLA_ROPE)
    cm, sm = np.tile(cm, (1, 2)), np.tile(sm, (1, 2))

    def flat(lat, fill):
        return np.concatenate([np.tile(lat, (n_batch, 1)),
                               np.full((n_batch * l_len, LANES), fill, np.float32)], axis=0)

    return tuple(jnp.asarray(a) for a in (flat(cg, 1.0), flat(sg, 0.0), flat(cm, 1.0), flat(sm, 0.0)))


def _round_up(a, m):
    return (a + m - 1) // m * m


def _pick_tile(n, pref):
    for t in pref:
        if n % t == 0:
            return t
    return n


def kernel(x, c, ctx, c_ctx, w_ada, b_ada, g_pre_mix, g_post_mix, g_pre_ffn, g_post_ffn, w_in, g_q_a, g_kv_a, w_q_up, w_kv_up, g_q_gqa, g_k_gqa, na_rpb, g_grp, w_out, ffn_w1, ffn_w3, ffn_w2, router, moe_w1, moe_w3, moe_w2):
    n_batch, s_len, d = x.shape
    l_len = ctx.shape[1]
    depth = w_ada.shape[0]
    q_lora, kv_lora = g_q_a.shape[-1], g_kv_a.shape[-1]
    dk_m = MLA_NOPE + MLA_ROPE
    hm = w_q_up.shape[-1] // dk_m
    hg = 3 * d // (8 * HEAD_DIM)
    hkv = hg // 3
    hn = d // (4 * HEAD_DIM)
    g_gqa = hg // hkv
    n_lat, n_ctx = n_batch * s_len, n_batch * l_len
    t_tot = n_lat + n_ctx
    assert s_len % MM_TILE_M == 0 and n_ctx % MM_TILE_M == 0 and l_len % ROW_TILE == 0
    assert hm % 2 == 0 and n_batch + 1 <= SUBLANES

    off_qg = q_lora + kv_lora
    off_kvg = off_qg + hg * HEAD_DIM
    off_n = off_kvg + 2 * hkv * HEAD_DIM
    n_main = off_n + 3 * hn * HEAD_DIM
    dims = dict(B=n_batch, S=s_len, L=l_len, HM=hm, HG=hg, HKV=hkv, HN=hn,
                OFF_QG=off_qg, OFF_KVG=off_kvg, OFF_N=off_n)

    tables = _rope_tables(n_batch, s_len, l_len)
    kh = min(NA_KH, s_len // GRID_W)
    tk_all = s_len + l_len
    tk_m = _pick_tile(tk_all, (768, 512, 256))
    tk_g = ROW_TILE

    mods = _ada(jnp.concatenate([c_ctx[None, :], c], axis=0), w_ada, b_ada)

    lat_tiles, all_tiles = n_lat // ROW_TILE, t_tot // ROW_TILE
    tiles_per_b = s_len // ROW_TILE

    def group_of(i):
        return jnp.where(i < lat_tiles, 1 + i // tiles_per_b, 0)

    def modtab_of(i):
        return mods[i, :n_batch + 1].reshape((n_batch + 1) * 6, 1, d)

    xs = (x.reshape(n_lat, d), ctx.reshape(n_ctx, d))
    modtab = modtab_of(0)
    _, h, _ = _row_call(xs, n_tiles=all_tiles, group_fn=group_of, modtab=modtab,
                        nxt=(g_pre_mix[0], 0, 1, modtab), name="row_in")

    for i in range(depth):
        last = i == depth - 1
        modtab = modtab_of(i)
        m_out = n_lat if last else t_tot
        out_tiles = m_out // ROW_TILE

        wi = w_in[i]
        c_kpe = q_lora + kv_lora
        w_main = jnp.concatenate([wi[:, :c_kpe], wi[:, c_kpe + MLA_ROPE:]], axis=1).astype(BF16)
        w_kpe = jnp.zeros((d, LANES), BF16).at[:, :MLA_ROPE].set(wi[:, c_kpe:c_kpe + MLA_ROPE].astype(BF16))
        u = _matmul(h, w_main, m_rows=t_tot, tn=_pick_tile(n_main, (1024, 512, 256, 128)), name="mm_in")
        kpe = _matmul(h, w_kpe, m_rows=t_tot, tn=LANES, name="mm_kpe")
        wq = w_q_up[i].reshape(q_lora, hm, dk_m)
        wq = jnp.concatenate([wq[:, :, :MLA_NOPE].reshape(q_lora, hm * MLA_NOPE),
                              wq[:, :, MLA_NOPE:].reshape(q_lora, hm * MLA_ROPE)], axis=1).astype(BF16)
        a_cols = _round_up(off_qg, LANES)
        qraw = _matmul(u, wq, m_rows=t_tot, tn=_pick_tile(hm * dk_m, (768, 384, 128)), a_block_cols=a_cols,
                       a_lo=0, a_hi=q_lora, rms_g=g_q_a[i], name="mm_qup")
        kvraw = _matmul(u, w_kv_up[i].astype(BF16), m_rows=t_tot,
                        tn=_pick_tile(hm * (MLA_NOPE + MLA_V), (1024, 768, 512, 256)), a_block_cols=a_cols,
                        a_lo=q_lora, a_hi=q_lora + kv_lora, rms_g=g_kv_a[i], name="mm_kvup")
        qm, km, vm, qg, kg, vg, qn, kn, vn = _prep(u, qraw, kvraw, kpe, tables, g_q_gqa[i], g_k_gqa[i],
                                                   dims=dims, tk_m=tk_m, tk_g=tk_g)

        o_m = _attention(qm, km, vm, dims=dims, g=1, tq=ROW_TILE, with_ctx=not last, name="attn_mla")
        o_g = _attention(qg, kg, vg, dims=dims, g=g_gqa, tq=ROW_TILE, with_ctx=not last, name="attn_gqa")
        o_n = _na(qn, kn, vn, _na_bias(na_rpb[i], kh), dims=dims, with_ctx=not last)

        y = _matmul_groups((o_m, o_g, o_n), g_grp[i], w_out[i].astype(BF16), m_rows=m_out,
                           tn=_pick_tile(d, (1024, 512, 256)), name="mm_out")

        j = i // 2
        moe = i % 2 == 1
        res = ([(y, 0)], 2, g_post_mix[i])
        nxt = (g_pre_ffn[i], 3, 4, modtab)
        if moe:
            x_all, hf, rt = _row_call(xs, n_tiles=out_tiles, group_fn=group_of, modtab=modtab, res=res, nxt=nxt,
                                      router=router[j], h_dtype=F32, name="row_mix")
            n_experts = router.shape[-1]
            gates = rt[:, 0:TOP_K]
            e_idx = rt[:, TOP_K:2 * TOP_K].astype(jnp.int32)
            src, dest, gate_sorted, tile_expert, tile_valid = _moe_plan(e_idx, gates, n_experts, MOE_TILE_M)
            x_sorted = _gather_rows(hf, src, BF16, "moe_dispatch")
            ys = _moe_ffn(x_sorted, moe_w1[j].astype(BF16), moe_w3[j].astype(BF16), moe_w2[j].astype(BF16),
                          gate_sorted, tile_expert, tile_valid, tf=_pick_tile(moe_w1.shape[-1], (256, 128)))
            comb = jnp.concatenate([dest[k::TOP_K] for k in range(TOP_K)])
            f_out = _gather_rows(ys, comb, BF16, "moe_combine")
            f_blocks = [(f_out, k * out_tiles) for k in range(TOP_K)]
        else:
            x_all, h, _ = _row_call(xs, n_tiles=out_tiles, group_fn=group_of, modtab=modtab, res=res, nxt=nxt,
                                    name="row_mix")
            f_dim = ffn_w1.shape[-1]
            f_pad = _round_up(f_dim, 1024)
            pad = ((0, 0), (0, f_pad - f_dim))
            w1 = jnp.pad(ffn_w1[j].astype(BF16), pad)
            w3 = jnp.pad(ffn_w3[j].astype(BF16), pad)
            w2 = jnp.pad(ffn_w2[j].astype(BF16), (pad[1], pad[0]))
            gact = _glu(h, w1, w3, m_rows=m_out, tf=1024, name="ffn_glu")
            tk2 = f_pad // 4 if (f_pad // 4) % LANES == 0 else f_pad
            f_out = _matmul_ktiled(gact, w2, m_rows=m_out, tn=_pick_tile(d, (1024, 512, 256)), tk=tk2, name="ffn_down")
            f_blocks = [(f_out, 0)]
        xs = (x_all,)

        res = (f_blocks, 5, g_post_ffn[i])
        if last:
            x_all, _, _ = _row_call(xs, n_tiles=out_tiles, group_fn=group_of, modtab=modtab, res=res, name="row_ffn")
        else:
            x_all, h, _ = _row_call(xs, n_tiles=out_tiles, group_fn=group_of, modtab=modtab, res=res,
                                    nxt=(g_pre_mix[i + 1], 0, 1, modtab_of(i + 1)), name="row_ffn")
        xs = (x_all,)
    return xs[0][:n_lat].reshape(n_batch, s_len, d)
```

```python
import functools
import math

import jax
import jax.numpy as jnp
import numpy as np
from jax import lax
from jax.experimental import pallas as pl
from jax.experimental.pallas import tpu as pltpu

F32 = jnp.float32
BF16 = jnp.bfloat16

GRID_W = 64
HEAD_DIM = 128
ROPE_THETA = 10000.0
EPS = 1e-6
NEG_INF = -1e30
MLA_NOPE = 128
MLA_ROPE = 64
MLA_V = 128
NA_KH = 8
NA_KW = 16
TOP_K = 2
LOG2E = math.log2(math.e)

LANES = 128
SUBLANES = 8
BF16_SUBLANES = 16
VMEM_LIMIT_BYTES = 56 * 1024 * 1024

ROW_TILE = 256
MM_TILE_M = 512
MOE_TILE_M = 512
V_PAD_ROWS = BF16_SUBLANES


def _cparams(n_axes):
    return pltpu.CompilerParams(dimension_semantics=("arbitrary",) * n_axes,
                                vmem_limit_bytes=VMEM_LIMIT_BYTES)


def _rms(v, g):
    ms = jnp.mean(v * v, axis=-1, keepdims=True)
    return v * lax.rsqrt(ms + EPS) * g


def _silu(v):
    return v * jax.nn.sigmoid(v)


ADA_K_CHUNK = 64


def _ada_body(ct_ref, w_ref, b_ref, o_ref, sb_ref, *, n_rows):
    d, tn = w_ref.shape

    @pl.when(jnp.logical_and(pl.program_id(0) == 0, pl.program_id(1) == 0))
    def _():
        s = _silu(ct_ref[...])
        for r in range(n_rows):
            sb_ref[r] = jnp.broadcast_to(s[:, r:r + 1], (d, LANES))

    def step(i, accs):
        off = pl.multiple_of(i * ADA_K_CHUNK, ADA_K_CHUNK)
        w = w_ref[pl.ds(off, ADA_K_CHUNK), :]
        out = []
        for r in range(n_rows):
            sb = sb_ref[r, pl.ds(off, ADA_K_CHUNK), :]
            prod = w * jnp.concatenate([sb] * (tn // LANES), axis=1)
            part = prod[0:SUBLANES]
            for j in range(1, ADA_K_CHUNK // SUBLANES):
                part = part + prod[j * SUBLANES:(j + 1) * SUBLANES]
            out.append(accs[r] + part)
        return tuple(out)

    accs = lax.fori_loop(0, d // ADA_K_CHUNK, step,
                         tuple(jnp.zeros((SUBLANES, tn), F32) for _ in range(n_rows)), unroll=2)
    rows = [jnp.sum(a, axis=0, keepdims=True) for a in accs]
    rows.append(jnp.zeros((SUBLANES - n_rows, tn), F32))
    o_ref[...] = jnp.concatenate(rows, axis=0) + b_ref[...]


def _ada(cond, w_ada, b_ada):
    n_rows, d = cond.shape
    depth, _, n = w_ada.shape
    tn = 512
    ct = jnp.zeros((d, SUBLANES), F32).at[:, :n_rows].set(cond.T)
    return pl.pallas_call(
        functools.partial(_ada_body, n_rows=n_rows),
        grid=(depth, n // tn),
        in_specs=[pl.BlockSpec((d, SUBLANES), lambda l, j: (0, 0)),
                  pl.BlockSpec((None, d, tn), lambda l, j: (l, 0, j)),
                  pl.BlockSpec((None, 1, tn), lambda l, j: (l, 0, j))],
        out_specs=pl.BlockSpec((None, SUBLANES, tn), lambda l, j: (l, 0, j)),
        out_shape=jax.ShapeDtypeStruct((depth, SUBLANES, n), F32),
        scratch_shapes=[pltpu.VMEM((n_rows, d, LANES), F32)],
        compiler_params=_cparams(2),
        name="ada",
    )(ct, w_ada, b_ada.reshape(depth, 1, n))


def _row_body(*refs, n_x, x_split, n_y, has_next, router, n_experts):
    it = iter(refs)
    x_refs = [next(it) for _ in range(n_x)]
    has_res = n_y > 0
    if has_res:
        y_refs = [next(it) for _ in range(n_y)]
        gate_ref, gpost_ref = next(it), next(it)
    if has_next:
        gpre_ref, shift_ref, scale_ref = next(it), next(it), next(it)
    if router:
        rhi_ref, rlo_ref = next(it), next(it)
    if has_res:
        xo_ref = next(it)
    if has_next:
        h_ref = next(it)
    if router:
        rt_ref = next(it)

    if n_x == 1:
        x = x_refs[0][...]
    else:
        x = jnp.where(pl.program_id(0) < x_split, x_refs[0][...], x_refs[1][...])
    if has_res:
        y = y_refs[0][...].astype(F32)
        for y_ref in y_refs[1:]:
            y = y + y_ref[...].astype(F32)
        x = x + gate_ref[...] * _rms(y, gpost_ref[...])
        xo_ref[...] = x
    if has_next:
        h = _rms(x, gpre_ref[...]) * (1.0 + scale_ref[...]) + shift_ref[...]
        h_ref[...] = h.astype(h_ref.dtype)
    if router:
        hi = h.astype(BF16)
        lo = (h - hi.astype(F32)).astype(BF16)
        logits = (jnp.dot(hi, rhi_ref[...], preferred_element_type=F32)
                  + jnp.dot(hi, rlo_ref[...], preferred_element_type=F32)
                  + jnp.dot(lo, rhi_ref[...], preferred_element_type=F32))
        lane = lax.broadcasted_iota(jnp.int32, logits.shape, 1)
        lg = jnp.where(lane < n_experts, logits, -jnp.inf)
        v1 = jnp.max(lg, axis=-1, keepdims=True)
        i1 = jnp.min(jnp.where(lg == v1, lane, LANES), axis=-1, keepdims=True)
        lg2 = jnp.where(lane == i1, -jnp.inf, lg)
        v2 = jnp.max(lg2, axis=-1, keepdims=True)
        i2 = jnp.min(jnp.where(lg2 == v2, lane, LANES), axis=-1, keepdims=True)
        e = jnp.exp(v2 - v1)
        g1 = 1.0 / (1.0 + e)
        g2 = e / (1.0 + e)
        rt_ref[...] = jnp.where(lane == 0, g1,
                                jnp.where(lane == 1, g2,
                                          jnp.where(lane == 2, i1.astype(F32),
                                                    jnp.where(lane == 3, i2.astype(F32), 0.0))))


def _row_call(xs, *, n_tiles, group_fn, modtab, res=None, nxt=None, router=None, h_dtype=BF16, name="row"):
    tm = ROW_TILE
    d = xs[0].shape[-1]
    has_res, has_next = res is not None, nxt is not None

    def modspec(k):
        return pl.BlockSpec((None, 1, d), lambda i: (group_fn(i) * 6 + k, 0, 0))

    vecspec = pl.BlockSpec((1, d), lambda i: (0, 0))
    x_split = 0
    if len(xs) == 1:
        ins, in_specs = [xs[0]], [pl.BlockSpec((tm, d), lambda i: (i, 0))]
    else:
        x_split = xs[0].shape[0] // tm
        ins = list(xs)
        in_specs = [pl.BlockSpec((tm, d), lambda i: (jnp.minimum(i, x_split - 1), 0)),
                    pl.BlockSpec((tm, d), lambda i: (jnp.maximum(i - x_split, 0), 0))]
    n_y = 0
    if has_res:
        y_blocks, gate_k, g_post = res
        n_y = len(y_blocks)
        for arr, tile0 in y_blocks:
            ins.append(arr)
            in_specs.append(pl.BlockSpec((tm, d), lambda i, tile0=tile0: (i + tile0, 0)))
        ins += [modtab, g_post.reshape(1, d)]
        in_specs += [modspec(gate_k), vecspec]
    if has_next:
        g_pre, shift_k, scale_k, modtab_n = nxt
        ins += [g_pre.reshape(1, d), modtab_n, modtab_n]
        in_specs += [vecspec, modspec(shift_k), modspec(scale_k)]
    n_experts = 0
    if router is not None:
        n_experts = router.shape[-1]
        rp = jnp.zeros((d, LANES), F32).at[:, :n_experts].set(router)
        rhi = rp.astype(BF16)
        rlo = (rp - rhi.astype(F32)).astype(BF16)
        ins += [rhi, rlo]
        in_specs += [pl.BlockSpec((d, LANES), lambda i: (0, 0))] * 2
    out_shape, out_specs = [], []
    rowspec = pl.BlockSpec((tm, d), lambda i: (i, 0))
    if has_res:
        out_shape.append(jax.ShapeDtypeStruct((n_tiles * tm, d), F32))
        out_specs.append(rowspec)
    if has_next:
        out_shape.append(jax.ShapeDtypeStruct((n_tiles * tm, d), h_dtype))
        out_specs.append(rowspec)
    if router is not None:
        out_shape.append(jax.ShapeDtypeStruct((n_tiles * tm, LANES), F32))
        out_specs.append(pl.BlockSpec((tm, LANES), lambda i: (i, 0)))
    outs = pl.pallas_call(
        functools.partial(_row_body, n_x=len(xs), x_split=x_split, n_y=n_y, has_next=has_next,
                          router=router is not None, n_experts=n_experts),
        grid=(n_tiles,), in_specs=in_specs, out_specs=out_specs, out_shape=out_shape,
        compiler_params=_cparams(1), name=name,
    )(*ins)
    outs = list(outs)
    x_new = outs.pop(0) if has_res else None
    h = outs.pop(0) if has_next else None
    rt = outs.pop(0) if router is not None else None
    return x_new, h, rt


def _mm_body(*refs, a_lo, a_hi, rms):
    if rms:
        a_ref, g_ref, w_ref, o_ref = refs
    else:
        a_ref, w_ref, o_ref = refs
    a = a_ref[:, a_lo:a_hi]
    if rms:
        a = _rms(a.astype(F32), g_ref[...]).astype(BF16)
    o_ref[...] = jnp.dot(a, w_ref[...], preferred_element_type=F32).astype(o_ref.dtype)


def _matmul(a, w, *, m_rows, tn, out_dtype=BF16, a_block_cols=None, a_lo=0, a_hi=None, rms_g=None, name="mm"):
    tm = MM_TILE_M
    k, n = w.shape
    if a_block_cols is None:
        a_block_cols = a.shape[1]
    if a_hi is None:
        a_hi = a_lo + k
    ins, in_specs = [a], [pl.BlockSpec((tm, a_block_cols), lambda j, i: (i, 0))]
    if rms_g is not None:
        ins.append(rms_g.reshape(1, k))
        in_specs.append(pl.BlockSpec((1, k), lambda j, i: (0, 0)))
    ins.append(w)
    in_specs.append(pl.BlockSpec((k, tn), lambda j, i: (0, j)))
    return pl.pallas_call(
        functools.partial(_mm_body, a_lo=a_lo, a_hi=a_hi, rms=rms_g is not None),
        grid=(n // tn, m_rows // tm), in_specs=in_specs,
        out_specs=pl.BlockSpec((tm, tn), lambda j, i: (i, j)),
        out_shape=jax.ShapeDtypeStruct((m_rows, n), out_dtype),
        compiler_params=_cparams(2), name=name,
    )(*ins)


def _mm_groups_body(*refs, parts_per_group, split_tile):
    n_a = sum(parts_per_group)
    a_refs = refs[:n_a]
    g_ref, w_ref, o_ref = refs[n_a:]
    parts, lo, pos = [], 0, 0
    for n_parts in parts_per_group:
        if n_parts == 1:
            a = a_refs[pos][...]
        else:
            a = jnp.where(pl.program_id(1) < split_tile, a_refs[pos][...], a_refs[pos + 1][...])
        pos += n_parts
        width = a.shape[1]
        parts.append(_rms(a.astype(F32), g_ref[:, lo:lo + width]).astype(BF16))
        lo += width
    a = jnp.concatenate(parts, axis=1)
    o_ref[...] = jnp.dot(a, w_ref[...], preferred_element_type=F32).astype(o_ref.dtype)


def _matmul_groups(groups, g, w, *, m_rows, tn, name):
    tm = MM_TILE_M
    k, n = w.shape
    split_tile = 0
    ins, in_specs = [], []
    for grp in groups:
        if len(grp) == 1:
            in_specs.append(pl.BlockSpec((tm, grp[0].shape[1]), lambda j, i: (i, 0)))
        else:
            split_tile = grp[0].shape[0] // tm
            st = split_tile
            in_specs.append(pl.BlockSpec((tm, grp[0].shape[1]), lambda j, i, st=st: (jnp.minimum(i, st - 1), 0)))
            in_specs.append(pl.BlockSpec((tm, grp[1].shape[1]), lambda j, i, st=st: (jnp.maximum(i - st, 0), 0)))
        ins += list(grp)
    in_specs += [pl.BlockSpec((1, k), lambda j, i: (0, 0)), pl.BlockSpec((k, tn), lambda j, i: (0, j))]
    return pl.pallas_call(
        functools.partial(_mm_groups_body, parts_per_group=tuple(len(grp) for grp in groups), split_tile=split_tile),
        grid=(n // tn, m_rows // tm), in_specs=in_specs,
        out_specs=pl.BlockSpec((tm, tn), lambda j, i: (i, j)),
        out_shape=jax.ShapeDtypeStruct((m_rows, n), BF16),
        compiler_params=_cparams(2), name=name,
    )(*ins, g.reshape(1, k), w)


def _mmk_body(a_ref, w_ref, o_ref, acc_ref, *, nk):
    kk = pl.program_id(2)

    @pl.when(kk == 0)
    def _():
        acc_ref[...] = jnp.zeros_like(acc_ref)

    acc_ref[...] += jnp.dot(a_ref[...], w_ref[...], preferred_element_type=F32)

    @pl.when(kk == nk - 1)
    def _():
        o_ref[...] = acc_ref[...].astype(o_ref.dtype)


def _matmul_ktiled(a, w, *, m_rows, tn, tk, out_dtype=BF16, name="mmk"):
    tm = MM_TILE_M
    k, n = w.shape
    nk = k // tk
    return pl.pallas_call(
        functools.partial(_mmk_body, nk=nk),
        grid=(n // tn, m_rows // tm, nk),
        in_specs=[pl.BlockSpec((tm, tk), lambda j, i, kk: (i, kk)),
                  pl.BlockSpec((tk, tn), lambda j, i, kk: (kk, j))],
        out_specs=pl.BlockSpec((tm, tn), lambda j, i, kk: (i, j)),
        out_shape=jax.ShapeDtypeStruct((m_rows, n), out_dtype),
        scratch_shapes=[pltpu.VMEM((tm, tn), F32)],
        compiler_params=_cparams(3), name=name,
    )(a, w)


def _glu_body(a_ref, w1_ref, w3_ref, o_ref):
    a = a_ref[...]
    h1 = jnp.dot(a, w1_ref[...], preferred_element_type=F32)
    h3 = jnp.dot(a, w3_ref[...], preferred_element_type=F32)
    o_ref[...] = (_silu(h1) * h3).astype(o_ref.dtype)


def _glu(a, w1, w3, *, m_rows, tf, name="glu"):
    tm = MM_TILE_M
    k, f = w1.shape
    return pl.pallas_call(
        _glu_body, grid=(f // tf, m_rows // tm),
        in_specs=[pl.BlockSpec((tm, k), lambda j, i: (i, 0)),
                  pl.BlockSpec((k, tf), lambda j, i: (0, j)),
                  pl.BlockSpec((k, tf), lambda j, i: (0, j))],
        out_specs=pl.BlockSpec((tm, tf), lambda j, i: (i, j)),
        out_shape=jax.ShapeDtypeStruct((m_rows, f), BF16),
        compiler_params=_cparams(2), name=name,
    )(a, w1, w3)


def _rope(x, c, s, half):
    lane = lax.broadcasted_iota(jnp.int32, x.shape, 1)
    first = (lane % (2 * half)) < half
    swapped = jnp.where(first, pltpu.roll(x, LANES - half, 1), pltpu.roll(x, half, 1))
    return x * c + swapped * s


def _prep_body(u_ref, qraw_ref, kvraw_ref, kpe_ref, cg_ref, sg_ref, cm_ref, sm_ref, gq_ref, gk_ref,
               qm_o, km_o, vm_o, qg_o, kg_o, vg_o, qn_o, kn_o, vn_o,
               *, hm, hg, hkv, hn, off_qg, off_kvg, off_n):
    cg, sg, cm, sm = cg_ref[...], sg_ref[...], cm_ref[...], sm_ref[...]
    scale_m = (MLA_NOPE + MLA_ROPE) ** -0.5 * LOG2E
    scale_g = HEAD_DIM ** -0.5 * LOG2E
    scale_n = HEAD_DIM ** -0.5
    d = HEAD_DIM
    tm = u_ref.shape[0]
    ones_rows = jnp.where(lax.broadcasted_iota(jnp.int32, (V_PAD_ROWS, tm), 0) == 0, 1.0, 0.0).astype(BF16)

    def t_bf16(v):
        return v.astype(F32).T.astype(BF16)

    kpe = _rope(kpe_ref[...].astype(F32), cm, sm, MLA_ROPE // 4)[:, :MLA_ROPE].astype(BF16)
    pe0 = hm * MLA_NOPE
    for h in range(hm):
        qm_o[h, 0:MLA_NOPE, :] = t_bf16(qraw_ref[:, h * MLA_NOPE:(h + 1) * MLA_NOPE].astype(F32) * scale_m)
        km_o[h, :, 0:MLA_NOPE] = kvraw_ref[:, h * 2 * d:h * 2 * d + MLA_NOPE]
        km_o[h, :, MLA_NOPE:MLA_NOPE + MLA_ROPE] = kpe
        vm_o[h, 0:MLA_V, :] = t_bf16(kvraw_ref[:, h * 2 * d + MLA_NOPE:(h + 1) * 2 * d])
        vm_o[h, MLA_V:MLA_V + V_PAD_ROWS, :] = ones_rows
    for j in range(hm // 2):
        pe = _rope(qraw_ref[:, pe0 + j * LANES:pe0 + (j + 1) * LANES].astype(F32), cm, sm, MLA_ROPE // 4)
        pe_t = t_bf16(pe * scale_m)
        qm_o[2 * j, MLA_NOPE:MLA_NOPE + MLA_ROPE, :] = pe_t[:MLA_ROPE]
        qm_o[2 * j + 1, MLA_NOPE:MLA_NOPE + MLA_ROPE, :] = pe_t[MLA_ROPE:]
    for h in range(hg):
        q = _rms(u_ref[:, off_qg + h * d:off_qg + (h + 1) * d].astype(F32), gq_ref[...])
        qg_o[h] = t_bf16(_rope(q, cg, sg, d // 4) * scale_g)
    for h in range(hkv):
        k = _rms(u_ref[:, off_kvg + h * d:off_kvg + (h + 1) * d].astype(F32), gk_ref[...])
        kg_o[h] = _rope(k, cg, sg, d // 4).astype(BF16)
        vg_o[h, 0:d, :] = t_bf16(u_ref[:, off_kvg + (hkv + h) * d:off_kvg + (hkv + h + 1) * d])
        vg_o[h, d:d + V_PAD_ROWS, :] = ones_rows
    for h in range(hn):
        qn_o[h] = (u_ref[:, off_n + h * d:off_n + (h + 1) * d].astype(F32) * scale_n).astype(BF16)
        kn_o[h] = u_ref[:, off_n + (hn + h) * d:off_n + (hn + h + 1) * d]
        vn_o[h] = u_ref[:, off_n + (2 * hn + h) * d:off_n + (2 * hn + h + 1) * d]


def _prep(u, qraw, kvraw, kpe, tables, g_q, g_k, *, dims, tk_m, tk_g):
    tm = ROW_TILE
    t_tot = u.shape[0]
    b, s, l = dims["B"], dims["S"], dims["L"]
    hm, hg, hkv, hn = dims["HM"], dims["HG"], dims["HKV"], dims["HN"]
    tk_len = s + l
    n_lat = b * s // tm
    spb, lpb = s // tm, l // tm

    def kmap(i):
        lat = i < n_lat
        j = i - n_lat
        bb = jnp.where(lat, i // spb, j // lpb)
        pos = jnp.where(lat, i % spb, spb + j % lpb)
        return bb, pos

    def kspec(h, dk):
        def im(i):
            bb, pos = kmap(i)
            return (bb, 0, pos, 0)
        return pl.BlockSpec((None, h, tm, dk), im)

    def vtspec(h, dv, tk):
        per = tk // tm

        def im(i):
            bb, pos = kmap(i)
            return (bb, 0, pos // per, 0, pos % per)
        return pl.BlockSpec((None, h, None, dv + V_PAD_ROWS, tm), im)

    def qtspec(h, dk):
        return pl.BlockSpec((h, None, dk, tm), lambda i: (0, i, 0, 0))

    def qspec(h, dk):
        return pl.BlockSpec((h, tm, dk), lambda i: (0, i, 0))

    def full(arr):
        return pl.BlockSpec((tm, arr.shape[1]), lambda i: (i, 0))

    dk_m = MLA_NOPE + MLA_ROPE
    d = HEAD_DIM
    out_shape = [
        jax.ShapeDtypeStruct((hm, t_tot // tm, dk_m, tm), BF16), jax.ShapeDtypeStruct((b, hm, tk_len, dk_m), BF16),
        jax.ShapeDtypeStruct((b, hm, tk_len // tk_m, MLA_V + V_PAD_ROWS, tk_m), BF16),
        jax.ShapeDtypeStruct((hg, t_tot // tm, d, tm), BF16), jax.ShapeDtypeStruct((b, hkv, tk_len, d), BF16),
        jax.ShapeDtypeStruct((b, hkv, tk_len // tk_g, d + V_PAD_ROWS, tk_g), BF16),
        jax.ShapeDtypeStruct((hn, t_tot, d), BF16), jax.ShapeDtypeStruct((b, hn, tk_len, d), BF16),
        jax.ShapeDtypeStruct((b, hn, tk_len, d), BF16),
    ]
    out_specs = [qtspec(hm, dk_m), kspec(hm, dk_m), vtspec(hm, MLA_V, tk_m),
                 qtspec(hg, d), kspec(hkv, d), vtspec(hkv, d, tk_g),
                 qspec(hn, d), kspec(hn, d), kspec(hn, d)]
    vec = pl.BlockSpec((1, d), lambda i: (0, 0))
    return pl.pallas_call(
        functools.partial(_prep_body, hm=hm, hg=hg, hkv=hkv, hn=hn,
                          off_qg=dims["OFF_QG"], off_kvg=dims["OFF_KVG"], off_n=dims["OFF_N"]),
        grid=(t_tot // tm,),
        in_specs=[full(u), full(qraw), full(kvraw), full(kpe)] + [full(t) for t in tables] + [vec, vec],
        out_specs=out_specs, out_shape=out_shape,
        compiler_params=_cparams(1), name="prep",
    )(u, qraw, kvraw, kpe, *tables, g_q.reshape(1, d), g_k.reshape(1, d))


M_INIT = -1e30


def _q_tile(qt_ref, g, t):
    if g == 1:
        return qt_ref[0, t]
    return jnp.concatenate([qt_ref[gi, t] for gi in range(g)], axis=1)


def _attn_body(qt_ref, k_ref, vt_ref, o_ref, s0, s1, p0, p1, a0, a1, m_sc, acc_sc, *, g, tq, tk, nk, dv, nt):
    s_buf, p_buf, a_buf = (s0, s1), (p0, p1), (a0, a1)
    n_items = nt * nk

    def split(it):
        if isinstance(it, int):
            return it // nk, it % nk
        return lax.div(it, jnp.int32(nk)), lax.rem(it, jnp.int32(nk))

    def scores(it):
        t, c = split(it)
        off = c * tk if isinstance(c, int) else pl.multiple_of(c * tk, tk)
        return jnp.dot(k_ref[pl.ds(off, tk), :], _q_tile(qt_ref, g, t), preferred_element_type=F32)

    def softmax(it, slot):
        t, _ = split(it)
        s = s_buf[slot][...]
        m_prev = m_sc[t]
        m_new = jnp.maximum(m_prev, jnp.max(s, axis=0, keepdims=True))
        p_buf[slot][...] = jnp.exp2(s - m_new).astype(BF16)
        a_buf[slot][...] = jnp.exp2(m_prev - m_new)
        m_sc[t] = m_new

    def values(it, slot):
        t, c = split(it)
        acc_sc[t] = a_buf[slot][...] * acc_sc[t] + jnp.dot(vt_ref[c], p_buf[slot][...], preferred_element_type=F32)

    def stage(it, slot):
        s_buf[1 - slot][...] = scores(it + 1)
        softmax(it, slot)
        values(it - 1, 1 - slot)

    m_sc[...] = jnp.full_like(m_sc, M_INIT)
    acc_sc[...] = jnp.zeros_like(acc_sc)
    s_buf[0][...] = scores(0)
    softmax(0, 0)
    if n_items > 1:
        s_buf[1][...] = scores(1)
        n_steady = n_items - 2

        def pair(u, carry):
            it = 2 * u + 1
            stage(it, 1)
            stage(it + 1, 0)
            return carry

        lax.fori_loop(0, n_steady // 2, pair, 0)
        if n_steady % 2:
            stage(n_items - 2, 1)
        softmax(n_items - 1, (n_items - 1) % 2)
        values(n_items - 2, (n_items - 2) % 2)
    values(n_items - 1, (n_items - 1) % 2)

    def finish(t, carry):
        acc = acc_sc[t]
        o_t = acc[0:dv] / acc[dv:dv + 1]
        row = pl.multiple_of(t * tq, tq)
        for gi in range(g):
            o_ref[pl.ds(row, tq), gi * dv:(gi + 1) * dv] = o_t[:, gi * tq:(gi + 1) * tq].T.astype(o_ref.dtype)
        return carry

    lax.fori_loop(0, nt, finish, 0)


def _attention(qt, k, vt, *, dims, g, nt, name):
    b, s_len, l_len = dims["B"], dims["S"], dims["L"]
    hk = k.shape[1]
    dk, tq = qt.shape[2:]
    nk, dvp, tk = vt.shape[2:]
    dv = dvp - V_PAD_ROWS
    steps_per_b = s_len // (nt * tq)
    n = g * tq
    return pl.pallas_call(
        functools.partial(_attn_body, g=g, tq=tq, tk=tk, nk=nk, dv=dv, nt=nt),
        grid=(b, hk, steps_per_b),
        in_specs=[pl.BlockSpec((g, nt, dk, tq), lambda bb, h, i: (h, bb * steps_per_b + i, 0, 0)),
                  pl.BlockSpec((None, None, s_len + l_len, dk), lambda bb, h, i: (bb, h, 0, 0)),
                  pl.BlockSpec((None, None, nk, dvp, tk), lambda bb, h, i: (bb, h, 0, 0, 0))],
        out_specs=pl.BlockSpec((nt * tq, g * dv), lambda bb, h, i: (bb * steps_per_b + i, h)),
        out_shape=jax.ShapeDtypeStruct((b * s_len, hk * g * dv), BF16),
        scratch_shapes=[pltpu.VMEM((tk, n), F32), pltpu.VMEM((tk, n), F32),
                        pltpu.VMEM((tk, n), BF16), pltpu.VMEM((tk, n), BF16),
                        pltpu.VMEM((1, n), F32), pltpu.VMEM((1, n), F32),
                        pltpu.VMEM((nt, 1, n), F32), pltpu.VMEM((nt, dvp, n), F32)],
        compiler_params=_cparams(3), name=name,
    )(qt, k, vt)


def _attn_ctx_body(qt_ref, k_ref, vt_ref, o_ref, *, g, tq, dv, ctx_rows):
    qt = _q_tile(qt_ref, g, 0)
    tk = vt_ref.shape[-1]
    s = jnp.dot(k_ref[...], qt, preferred_element_type=F32)
    p = jnp.exp2(s - jnp.max(s, axis=0, keepdims=True)).astype(BF16)
    acc = jnp.dot(vt_ref[:, tk - ctx_rows:tk], p, preferred_element_type=F32)
    o_t = acc[0:dv] / acc[dv:dv + 1]
    for gi in range(g):
        o_ref[:, gi * dv:(gi + 1) * dv] = o_t[:, gi * tq:(gi + 1) * tq].T.astype(o_ref.dtype)


def _attention_ctx(qt, k, vt, *, dims, g, name):
    b, s_len, l_len = dims["B"], dims["S"], dims["L"]
    hk = k.shape[1]
    dk, tq = qt.shape[2:]
    nk, dvp, tk = vt.shape[2:]
    dv = dvp - V_PAD_ROWS
    cpb = l_len // tq
    lat_tiles = b * s_len // tq
    assert l_len % tq == 0 and l_len <= tk and s_len % l_len == 0
    return pl.pallas_call(
        functools.partial(_attn_ctx_body, g=g, tq=tq, dv=dv, ctx_rows=l_len),
        grid=(b, hk, cpb),
        in_specs=[pl.BlockSpec((g, 1, dk, tq), lambda bb, h, i: (h, lat_tiles + bb * cpb + i, 0, 0)),
                  pl.BlockSpec((None, None, l_len, dk), lambda bb, h, i: (bb, h, s_len // l_len, 0)),
                  pl.BlockSpec((None, None, None, dvp, tk), lambda bb, h, i: (bb, h, nk - 1, 0, 0))],
        out_specs=pl.BlockSpec((tq, g * dv), lambda bb, h, i: (bb * cpb + i, h)),
        out_shape=jax.ShapeDtypeStruct((b * l_len, hk * g * dv), BF16),
        compiler_params=_cparams(3), name=name,
    )(qt, k, vt)


def _na_body(q_ref, k_ref, v_ref, bias_ref, o_ref, *, rows_per_step, n_grid_rows, kh, s_len, l_len, lat_steps,
             with_ctx):
    w = GRID_W
    j = pl.program_id(2)
    kc = k_ref[s_len:s_len + l_len, :]
    vc = v_ref[s_len:s_len + l_len, :]
    nt = (((1,), (1,)), ((), ()))

    def latent_step():
        span = kh + rows_per_step
        r0 = j * rows_per_step
        u0 = jnp.clip(r0 - kh // 2, 0, n_grid_rows - span)
        koff = pl.multiple_of(u0 * w, w)
        kw = k_ref[pl.ds(koff, span * w), :]
        vw = v_ref[pl.ds(koff, span * w), :]
        q = q_ref[...]
        s_loc = lax.dot_general(q, kw, nt, preferred_element_type=F32) + bias_ref[(r0 - u0) // rows_per_step]
        s_ctx = lax.dot_general(q, kc, nt, preferred_element_type=F32)
        m = jnp.maximum(jnp.max(s_loc, axis=-1, keepdims=True), jnp.max(s_ctx, axis=-1, keepdims=True))
        p_loc = jnp.exp(s_loc - m)
        p_ctx = jnp.exp(s_ctx - m)
        denom = jnp.sum(p_loc, axis=-1, keepdims=True) + jnp.sum(p_ctx, axis=-1, keepdims=True)
        o = (jnp.dot(p_loc.astype(BF16), vw, preferred_element_type=F32)
             + jnp.dot(p_ctx.astype(BF16), vc, preferred_element_type=F32))
        o_ref[...] = (o / denom).astype(o_ref.dtype)

    def ctx_step():
        s = lax.dot_general(q_ref[...], kc, nt, preferred_element_type=F32)
        p = jnp.exp(s - jnp.max(s, axis=-1, keepdims=True))
        o = jnp.dot(p.astype(BF16), vc, preferred_element_type=F32)
        o_ref[...] = (o / jnp.sum(p, axis=-1, keepdims=True)).astype(o_ref.dtype)

    if with_ctx:
        pl.when(j < lat_steps)(latent_step)
        pl.when(j >= lat_steps)(ctx_step)
    else:
        latent_step()


def _na_bias(rpb, kh, n_grid_rows, rps):
    span = kh + rps
    col = np.arange(GRID_W)
    start_c = np.clip(col - NA_KW // 2, 0, GRID_W - NA_KW)
    col_mask = (col[None, :] >= start_c[:, None]) & (col[None, :] < start_c[:, None] + NA_KW)
    dc_idx = np.clip(col[None, :] - col[:, None] + NA_KW - 1, 0, 2 * NA_KW - 2)
    n_var = kh // rps + 1
    dr_idx = np.full((n_var, rps, span), -1)
    seen = set()
    for r0 in range(0, n_grid_rows, rps):
        u0 = int(np.clip(r0 - kh // 2, 0, n_grid_rows - span))
        var = (r0 - u0) // rps
        assert (r0 - u0) % rps == 0 and 0 <= var < n_var
        table = np.full((rps, span), -1)
        for qr in range(rps):
            r = r0 + qr
            start = int(np.clip(r - kh // 2, 0, n_grid_rows - kh))
            for kr in range(span):
                if start <= u0 + kr < start + kh:
                    table[qr, kr] = u0 + kr - r + NA_KH - 1
        assert var not in seen or np.array_equal(dr_idx[var], table)
        seen.add(var)
        dr_idx[var] = table
    row_mask = dr_idx >= 0
    oh_r = (dr_idx[..., None] == np.arange(2 * NA_KH - 1)).astype(np.float32)
    oh_c = (dc_idx[:, :, None] == np.arange(2 * NA_KW - 1)[None, None, :]).astype(np.float32)
    bias = jnp.einsum("vrja,hab,qkb->hvrqjk", oh_r, rpb.astype(F32), oh_c, precision=lax.Precision.HIGHEST)
    mask = row_mask[None, :, :, None, :, None] & col_mask[None, None, None, :, None, :]
    bias = jnp.where(mask, bias, NEG_INF)
    return bias.reshape(rpb.shape[0], n_var, rps * GRID_W, span * GRID_W)


def _na(q, k, v, rpb, *, dims, with_ctx):
    b, s, l, hn = dims["B"], dims["S"], dims["L"], dims["HN"]
    r_tot = s // GRID_W
    kh = min(NA_KH, r_tot)
    tq = ROW_TILE
    rps = tq // GRID_W
    assert r_tot >= kh + rps and kh % rps == 0 and (kh // 2) % rps == 0
    bias = _na_bias(rpb, kh, r_tot, rps)
    d = HEAD_DIM
    spb = r_tot // rps
    cpb = l // tq if with_ctx else 0
    lat_tiles = b * spb
    rows = b * (s + (l if with_ctx else 0))

    def qtile(bb, j):
        return jnp.where(j < spb, bb * spb + j, lat_tiles + bb * cpb + (j - spb))

    return pl.pallas_call(
        functools.partial(_na_body, rows_per_step=rps, n_grid_rows=r_tot, kh=kh, s_len=s, l_len=l, lat_steps=spb,
                          with_ctx=with_ctx),
        grid=(b, hn, spb + cpb),
        in_specs=[pl.BlockSpec((None, tq, d), lambda bb, h, j: (h, qtile(bb, j), 0)),
                  pl.BlockSpec((None, None, s + l, d), lambda bb, h, j: (bb, h, 0, 0)),
                  pl.BlockSpec((None, None, s + l, d), lambda bb, h, j: (bb, h, 0, 0)),
                  pl.BlockSpec((None,) + bias.shape[1:], lambda bb, h, j: (h, 0, 0, 0))],
        out_specs=pl.BlockSpec((tq, d), lambda bb, h, j: (qtile(bb, j), h)),
        out_shape=jax.ShapeDtypeStruct((rows, hn * d), BF16),
        compiler_params=_cparams(3), name="na",
    )(q, k, v, bias)


GATHER_ROWS = 256


def _gather_body(idx_ref, src_ref, o_ref, buf, sem):
    i, n_tiles = pl.program_id(0), pl.num_programs(0)
    slot = i % 2

    def row_copy(r, row, s):
        return pltpu.make_async_copy(src_ref.at[pl.ds(row, 1)], buf.at[s, pl.ds(r, 1)], sem.at[s])

    def issue_tile(tile, s):
        base = tile * GATHER_ROWS

        def issue(r, c):
            row_copy(r, idx_ref[base + r], s).start()
            return c

        lax.fori_loop(0, GATHER_ROWS, issue, 0)

    @pl.when(i == 0)
    def _():
        issue_tile(0, 0)

    @pl.when(i + 1 < n_tiles)
    def _():
        issue_tile(i + 1, 1 - slot)

    def wait(r, c):
        row_copy(r, 0, slot).wait()
        return c

    lax.fori_loop(0, GATHER_ROWS, wait, 0)
    o_ref[...] = buf[slot].astype(o_ref.dtype)


def _gather_rows(src, idx, out_dtype, name):
    n = idx.shape[0]
    d = src.shape[1]
    return pl.pallas_call(
        _gather_body,
        grid_spec=pltpu.PrefetchScalarGridSpec(
            num_scalar_prefetch=1, grid=(n // GATHER_ROWS,),
            in_specs=[pl.BlockSpec(memory_space=pl.ANY)],
            out_specs=pl.BlockSpec((GATHER_ROWS, d), lambda i, idx_ref: (i, 0)),
            scratch_shapes=[pltpu.VMEM((2, GATHER_ROWS, d), src.dtype), pltpu.SemaphoreType.DMA((2,))]),
        out_shape=jax.ShapeDtypeStruct((n, d), out_dtype),
        compiler_params=_cparams(1), name=name,
    )(idx, src)


def _moe_body(te_ref, tv_ref, x_ref, w1_ref, w3_ref, w2_ref, gate_ref, o_ref, acc_ref, *, nf):
    i, f = pl.program_id(0), pl.program_id(1)
    valid = tv_ref[i] > 0

    @pl.when(jnp.logical_and(valid, f == 0))
    def _():
        acc_ref[...] = jnp.zeros_like(acc_ref)

    @pl.when(valid)
    def _():
        x = x_ref[...]
        h1 = jnp.dot(x, w1_ref[...], preferred_element_type=F32)
        h3 = jnp.dot(x, w3_ref[...], preferred_element_type=F32)
        hm = (_silu(h1) * h3).astype(BF16)
        acc_ref[...] += jnp.dot(hm, w2_ref[...], preferred_element_type=F32)

    @pl.when(jnp.logical_and(valid, f == nf - 1))
    def _():
        o_ref[...] = acc_ref[...] * gate_ref[...]

    @pl.when(jnp.logical_and(jnp.logical_not(valid), f == nf - 1))
    def _():
        o_ref[...] = jnp.zeros_like(o_ref)


def _moe_ffn(xs, w1, w3, w2, gate_sorted, tile_expert, tile_valid, *, tf):
    tm = MOE_TILE_M
    p_tot, d = xs.shape
    f_dim = w1.shape[-1]
    nf = f_dim // tf

    def fidx(i, f, tv):
        return jnp.where(tv[i] > 0, f, nf - 1)

    return pl.pallas_call(
        functools.partial(_moe_body, nf=nf),
        grid_spec=pltpu.PrefetchScalarGridSpec(
            num_scalar_prefetch=2, grid=(p_tot // tm, nf),
            in_specs=[pl.BlockSpec((tm, d), lambda i, f, te, tv: (i, 0)),
                      pl.BlockSpec((None, d, tf), lambda i, f, te, tv: (te[i], 0, fidx(i, f, tv))),
                      pl.BlockSpec((None, d, tf), lambda i, f, te, tv: (te[i], 0, fidx(i, f, tv))),
                      pl.BlockSpec((None, tf, d), lambda i, f, te, tv: (te[i], fidx(i, f, tv), 0)),
                      pl.BlockSpec((tm, 1), lambda i, f, te, tv: (i, 0))],
            out_specs=pl.BlockSpec((tm, d), lambda i, f, te, tv: (i, 0)),
            scratch_shapes=[pltpu.VMEM((tm, d), F32)]),
        out_shape=jax.ShapeDtypeStruct((p_tot, d), F32),
        compiler_params=_cparams(2), name="moe_ffn",
    )(tile_expert, tile_valid, xs, w1, w3, w2, gate_sorted)


def _moe_plan(e_idx, gates, n_experts, tm):
    flat_e = e_idx.reshape(-1)
    n_slots = flat_e.shape[0]
    onehot = (flat_e[:, None] == jnp.arange(n_experts)[None, :]).astype(jnp.int32)
    csum = jnp.cumsum(onehot, axis=0)
    rank = jnp.sum(csum * onehot, axis=1) - 1
    counts = csum[-1]
    tiles_per = (counts + tm - 1) // tm
    cum_tiles = jnp.cumsum(tiles_per)
    row_start = (cum_tiles - tiles_per) * tm
    dest = jnp.sum(row_start[None, :] * onehot, axis=1) + rank
    n_tiles = n_slots // tm + n_experts
    p_tot = n_tiles * tm
    te = jnp.sum((jnp.arange(n_tiles)[:, None] >= cum_tiles[None, :]).astype(jnp.int32), axis=1)
    tile_valid = (te < n_experts).astype(jnp.int32)
    tile_expert = jnp.minimum(te, n_experts - 1)
    src = jnp.zeros((p_tot,), jnp.int32).at[dest].set(jnp.arange(n_slots, dtype=jnp.int32) // TOP_K)
    gate_sorted = jnp.zeros((p_tot,), F32).at[dest].set(gates.reshape(-1)).reshape(p_tot, 1)
    return src, dest.astype(jnp.int32), gate_sorted, tile_expert, tile_valid


def _rope_tables(n_batch, s_len, l_len):
    t = np.arange(s_len)
    rows, cols = t // GRID_W, t % GRID_W

    def table(width):
        half = width // 2
        quarter = half // 2
        lane = np.arange(width)
        pos = np.where((lane // half)[None, :] == 0, rows[:, None], cols[:, None]).astype(np.float64)
        fi = (lane % half) % quarter
        inv = ROPE_THETA ** (-(2.0 * fi) / half)
        ang = (pos.astype(np.float32) * inv.astype(np.float32)[None, :]).astype(np.float32)
        sign = np.where((lane % half) < quarter, -1.0, 1.0)
        return np.cos(ang).astype(np.float32), (np.sin(ang) * sign[None, :]).astype(np.float32)

    cg, sg = table(HEAD_DIM)
    cm, sm = table(MLA_ROPE)
    cm, sm = np.tile(cm, (1, 2)), np.tile(sm, (1, 2))

    def flat(lat, fill):
        return np.concatenate([np.tile(lat, (n_batch, 1)),
                               np.full((n_batch * l_len, LANES), fill, np.float32)], axis=0)

    return tuple(jnp.asarray(a) for a in (flat(cg, 1.0), flat(sg, 0.0), flat(cm, 1.0), flat(sm, 0.0)))


def _round_up(a, m):
    return (a + m - 1) // m * m


def _pick_tile(n, pref):
    for t in pref:
        if n % t == 0:
            return t
    return n


def kernel(x, c, ctx, c_ctx, w_ada, b_ada, g_pre_mix, g_post_mix, g_pre_ffn, g_post_ffn, w_in, g_q_a, g_kv_a, w_q_up, w_kv_up, g_q_gqa, g_k_gqa, na_rpb, g_grp, w_out, ffn_w1, ffn_w3, ffn_w2, router, moe_w1, moe_w3, moe_w2):
    n_batch, s_len, d = x.shape
    l_len = ctx.shape[1]
    depth = w_ada.shape[0]
    q_lora, kv_lora = g_q_a.shape[-1], g_kv_a.shape[-1]
    dk_m = MLA_NOPE + MLA_ROPE
    hm = w_q_up.shape[-1] // dk_m
    hg = 3 * d // (8 * HEAD_DIM)
    hkv = hg // 3
    hn = d // (4 * HEAD_DIM)
    g_gqa = hg // hkv
    n_lat, n_ctx = n_batch * s_len, n_batch * l_len
    t_tot = n_lat + n_ctx
    assert s_len % MM_TILE_M == 0 and n_ctx % MM_TILE_M == 0 and l_len % ROW_TILE == 0
    assert hm % 2 == 0 and n_batch + 1 <= SUBLANES

    off_qg = q_lora + kv_lora
    off_kvg = off_qg + hg * HEAD_DIM
    off_n = off_kvg + 2 * hkv * HEAD_DIM
    n_main = off_n + 3 * hn * HEAD_DIM
    dims = dict(B=n_batch, S=s_len, L=l_len, HM=hm, HG=hg, HKV=hkv, HN=hn,
                OFF_QG=off_qg, OFF_KVG=off_kvg, OFF_N=off_n)

    tables = _rope_tables(n_batch, s_len, l_len)
    kh = min(NA_KH, s_len // GRID_W)
    tk_all = s_len + l_len
    tk_m = _pick_tile(tk_all, (768, 512, 256))
    tk_g = ROW_TILE

    mods = _ada(jnp.concatenate([c_ctx[None, :], c], axis=0), w_ada, b_ada)

    lat_tiles, all_tiles = n_lat // ROW_TILE, t_tot // ROW_TILE
    tiles_per_b = s_len // ROW_TILE

    def group_of(i):
        return jnp.where(i < lat_tiles, 1 + i // tiles_per_b, 0)

    def modtab_of(i):
        return mods[i, :n_batch + 1].reshape((n_batch + 1) * 6, 1, d)

    xs = (x.reshape(n_lat, d), ctx.reshape(n_ctx, d))
    modtab = modtab_of(0)
    _, h, _ = _row_call(xs, n_tiles=all_tiles, group_fn=group_of, modtab=modtab,
                        nxt=(g_pre_mix[0], 0, 1, modtab), name="row_in")

    for i in range(depth):
        last = i == depth - 1
        modtab = modtab_of(i)
        m_out = n_lat if last else t_tot
        out_tiles = m_out // ROW_TILE

        wi = w_in[i]
        c_kpe = q_lora + kv_lora
        w_main = jnp.concatenate([wi[:, :c_kpe], wi[:, c_kpe + MLA_ROPE:]], axis=1).astype(BF16)
        w_kpe = jnp.zeros((d, LANES), BF16).at[:, :MLA_ROPE].set(wi[:, c_kpe:c_kpe + MLA_ROPE].astype(BF16))
        u = _matmul(h, w_main, m_rows=t_tot, tn=_pick_tile(n_main, (1024, 512, 256, 128)), name="mm_in")
        kpe = _matmul(h, w_kpe, m_rows=t_tot, tn=LANES, name="mm_kpe")
        wq = w_q_up[i].reshape(q_lora, hm, dk_m)
        wq = jnp.concatenate([wq[:, :, :MLA_NOPE].reshape(q_lora, hm * MLA_NOPE),
                              wq[:, :, MLA_NOPE:].reshape(q_lora, hm * MLA_ROPE)], axis=1).astype(BF16)
        a_cols = _round_up(off_qg, LANES)
        qraw = _matmul(u, wq, m_rows=t_tot, tn=_pick_tile(hm * dk_m, (768, 384, 128)), a_block_cols=a_cols,
                       a_lo=0, a_hi=q_lora, rms_g=g_q_a[i], name="mm_qup")
        kvraw = _matmul(u, w_kv_up[i].astype(BF16), m_rows=t_tot,
                        tn=_pick_tile(hm * (MLA_NOPE + MLA_V), (1024, 768, 512, 256)), a_block_cols=a_cols,
                        a_lo=q_lora, a_hi=q_lora + kv_lora, rms_g=g_kv_a[i], name="mm_kvup")
        qm, km, vm, qg, kg, vg, qn, kn, vn = _prep(u, qraw, kvraw, kpe, tables, g_q_gqa[i], g_k_gqa[i],
                                                   dims=dims, tk_m=tk_m, tk_g=tk_g)

        q_tiles_per_b = s_len // ROW_TILE
        o_m = (_attention(qm, km, vm, dims=dims, g=1, nt=q_tiles_per_b, name="attn_mla"),)
        o_g = (_attention(qg, kg, vg, dims=dims, g=g_gqa, nt=_pick_tile(q_tiles_per_b, (16, 8, 4, 2)),
                          name="attn_gqa"),)
        if not last:
            o_m += (_attention_ctx(qm, km, vm, dims=dims, g=1, name="attn_mla_ctx"),)
            o_g += (_attention_ctx(qg, kg, vg, dims=dims, g=g_gqa, name="attn_gqa_ctx"),)
        o_n = (_na(qn, kn, vn, na_rpb[i], dims=dims, with_ctx=not last),)

        y = _matmul_groups((o_m, o_g, o_n), g_grp[i], w_out[i].astype(BF16), m_rows=m_out,
                           tn=_pick_tile(d, (1024, 512, 256)), name="mm_out")

        j = i // 2
        moe = i % 2 == 1
        res = ([(y, 0)], 2, g_post_mix[i])
        nxt = (g_pre_ffn[i], 3, 4, modtab)
        if moe:
            x_all, hf, rt = _row_call(xs, n_tiles=out_tiles, group_fn=group_of, modtab=modtab, res=res, nxt=nxt,
                                      router=router[j], h_dtype=F32, name="row_mix")
            n_experts = router.shape[-1]
            gates = rt[:, 0:TOP_K]
            e_idx = rt[:, TOP_K:2 * TOP_K].astype(jnp.int32)
            src, dest, gate_sorted, tile_expert, tile_valid = _moe_plan(e_idx, gates, n_experts, MOE_TILE_M)
            x_sorted = _gather_rows(hf, src, BF16, "moe_dispatch")
            ys = _moe_ffn(x_sorted, moe_w1[j].astype(BF16), moe_w3[j].astype(BF16), moe_w2[j].astype(BF16),
                          gate_sorted, tile_expert, tile_valid, tf=_pick_tile(moe_w1.shape[-1], (256, 128)))
            comb = jnp.concatenate([dest[k::TOP_K] for k in range(TOP_K)])
            f_out = _gather_rows(ys, comb, BF16, "moe_combine")
            f_blocks = [(f_out, k * out_tiles) for k in range(TOP_K)]
        else:
            x_all, h, _ = _row_call(xs, n_tiles=out_tiles, group_fn=group_of, modtab=modtab, res=res, nxt=nxt,
                                    name="row_mix")
            f_dim = ffn_w1.shape[-1]
            f_pad = _round_up(f_dim, 1024)
            pad = ((0, 0), (0, f_pad - f_dim))
            w1 = jnp.pad(ffn_w1[j].astype(BF16), pad)
            w3 = jnp.pad(ffn_w3[j].astype(BF16), pad)
            w2 = jnp.pad(ffn_w2[j].astype(BF16), (pad[1], pad[0]))
            gact = _glu(h, w1, w3, m_rows=m_out, tf=1024, name="ffn_glu")
            tk2 = f_pad // 4 if (f_pad // 4) % LANES == 0 else f_pad
            f_out = _matmul_ktiled(gact, w2, m_rows=m_out, tn=_pick_tile(d, (1024, 512, 256)), tk=tk2, name="ffn_down")
            f_blocks = [(f_out, 0)]
        xs = (x_all,)

        res = (f_blocks, 5, g_post_ffn[i])
        if last:
            x_all, _, _ = _row_call(xs, n_tiles=out_tiles, group_fn=group_of, modtab=modtab, res=res, name="row_ffn")
        else:
            x_all, h, _ = _row_call(xs, n_tiles=out_tiles, group_fn=group_of, modtab=modtab, res=res,
                                    nxt=(g_pre_mix[i + 1], 0, 1, modtab_of(i + 1)), name="row_ffn")
        xs = (x_all,)
    return xs[0][:n_lat].reshape(n_batch, s_len, d)
```

```python
import functools
import math

import jax
import jax.numpy as jnp
import numpy as np
from jax import lax
from jax.experimental import pallas as pl
from jax.experimental.pallas import tpu as pltpu

F32 = jnp.float32
BF16 = jnp.bfloat16

GRID_W = 64
HEAD_DIM = 128
ROPE_THETA = 10000.0
EPS = 1e-6
NEG_INF = -1e30
MLA_NOPE = 128
MLA_ROPE = 64
MLA_V = 128
NA_KH = 8
NA_KW = 16
TOP_K = 2
LOG2E = math.log2(math.e)

LANES = 128
SUBLANES = 8
BF16_SUBLANES = 16
VMEM_LIMIT_BYTES = 56 * 1024 * 1024

ROW_TILE = 256
MM_TILE_M = 512
MOE_TILE_M = 512
V_PAD_ROWS = BF16_SUBLANES
MLA_TILES_PER_ITEM = 1


def _cparams(n_axes):
    return pltpu.CompilerParams(dimension_semantics=("arbitrary",) * n_axes,
                                vmem_limit_bytes=VMEM_LIMIT_BYTES)


def _rms(v, g):
    ms = jnp.mean(v * v, axis=-1, keepdims=True)
    return v * lax.rsqrt(ms + EPS) * g


def _silu(v):
    return v * jax.nn.sigmoid(v)


ADA_K_CHUNK = 64


def _ada_body(ct_ref, w_ref, b_ref, o_ref, sb_ref, *, n_rows):
    d, tn = w_ref.shape

    @pl.when(jnp.logical_and(pl.program_id(0) == 0, pl.program_id(1) == 0))
    def _():
        s = _silu(ct_ref[...])
        for r in range(n_rows):
            sb_ref[r] = jnp.broadcast_to(s[:, r:r + 1], (d, LANES))

    def step(i, accs):
        off = pl.multiple_of(i * ADA_K_CHUNK, ADA_K_CHUNK)
        w = w_ref[pl.ds(off, ADA_K_CHUNK), :]
        out = []
        for r in range(n_rows):
            sb = sb_ref[r, pl.ds(off, ADA_K_CHUNK), :]
            prod = w * jnp.concatenate([sb] * (tn // LANES), axis=1)
            part = prod[0:SUBLANES]
            for j in range(1, ADA_K_CHUNK // SUBLANES):
                part = part + prod[j * SUBLANES:(j + 1) * SUBLANES]
            out.append(accs[r] + part)
        return tuple(out)

    accs = lax.fori_loop(0, d // ADA_K_CHUNK, step,
                         tuple(jnp.zeros((SUBLANES, tn), F32) for _ in range(n_rows)), unroll=2)
    rows = [jnp.sum(a, axis=0, keepdims=True) for a in accs]
    rows.append(jnp.zeros((SUBLANES - n_rows, tn), F32))
    o_ref[...] = jnp.concatenate(rows, axis=0) + b_ref[...]


def _ada(cond, w_ada, b_ada):
    n_rows, d = cond.shape
    depth, _, n = w_ada.shape
    tn = 512
    ct = jnp.zeros((d, SUBLANES), F32).at[:, :n_rows].set(cond.T)
    return pl.pallas_call(
        functools.partial(_ada_body, n_rows=n_rows),
        grid=(depth, n // tn),
        in_specs=[pl.BlockSpec((d, SUBLANES), lambda l, j: (0, 0)),
                  pl.BlockSpec((None, d, tn), lambda l, j: (l, 0, j)),
                  pl.BlockSpec((None, 1, tn), lambda l, j: (l, 0, j))],
        out_specs=pl.BlockSpec((None, SUBLANES, tn), lambda l, j: (l, 0, j)),
        out_shape=jax.ShapeDtypeStruct((depth, SUBLANES, n), F32),
        scratch_shapes=[pltpu.VMEM((n_rows, d, LANES), F32)],
        compiler_params=_cparams(2),
        name="ada",
    )(ct, w_ada, b_ada.reshape(depth, 1, n))


def _row_body(*refs, n_x, x_split, n_y, has_next, router, n_experts):
    it = iter(refs)
    x_refs = [next(it) for _ in range(n_x)]
    has_res = n_y > 0
    if has_res:
        y_refs = [next(it) for _ in range(n_y)]
        gate_ref, gpost_ref = next(it), next(it)
    if has_next:
        gpre_ref, shift_ref, scale_ref = next(it), next(it), next(it)
    if router:
        rhi_ref, rlo_ref = next(it), next(it)
    if has_res:
        xo_ref = next(it)
    if has_next:
        h_ref = next(it)
    if router:
        rt_ref = next(it)

    if n_x == 1:
        x = x_refs[0][...]
    else:
        x = jnp.where(pl.program_id(0) < x_split, x_refs[0][...], x_refs[1][...])
    if has_res:
        y = y_refs[0][...].astype(F32)
        for y_ref in y_refs[1:]:
            y = y + y_ref[...].astype(F32)
        x = x + gate_ref[...] * _rms(y, gpost_ref[...])
        xo_ref[...] = x
    if has_next:
        h = _rms(x, gpre_ref[...]) * (1.0 + scale_ref[...]) + shift_ref[...]
        h_ref[...] = h.astype(h_ref.dtype)
    if router:
        hi = h.astype(BF16)
        lo = (h - hi.astype(F32)).astype(BF16)
        logits = (jnp.dot(hi, rhi_ref[...], preferred_element_type=F32)
                  + jnp.dot(hi, rlo_ref[...], preferred_element_type=F32)
                  + jnp.dot(lo, rhi_ref[...], preferred_element_type=F32))
        lane = lax.broadcasted_iota(jnp.int32, logits.shape, 1)
        lg = jnp.where(lane < n_experts, logits, -jnp.inf)
        v1 = jnp.max(lg, axis=-1, keepdims=True)
        i1 = jnp.min(jnp.where(lg == v1, lane, LANES), axis=-1, keepdims=True)
        lg2 = jnp.where(lane == i1, -jnp.inf, lg)
        v2 = jnp.max(lg2, axis=-1, keepdims=True)
        i2 = jnp.min(jnp.where(lg2 == v2, lane, LANES), axis=-1, keepdims=True)
        e = jnp.exp(v2 - v1)
        g1 = 1.0 / (1.0 + e)
        g2 = e / (1.0 + e)
        rt_ref[...] = jnp.where(lane == 0, g1,
                                jnp.where(lane == 1, g2,
                                          jnp.where(lane == 2, i1.astype(F32),
                                                    jnp.where(lane == 3, i2.astype(F32), 0.0))))


def _row_call(xs, *, n_tiles, group_fn, modtab, res=None, nxt=None, router=None, h_dtype=BF16, name="row"):
    tm = ROW_TILE
    d = xs[0].shape[-1]
    has_res, has_next = res is not None, nxt is not None

    def modspec(k):
        return pl.BlockSpec((None, 1, d), lambda i: (group_fn(i) * 6 + k, 0, 0))

    vecspec = pl.BlockSpec((1, d), lambda i: (0, 0))
    x_split = 0
    if len(xs) == 1:
        ins, in_specs = [xs[0]], [pl.BlockSpec((tm, d), lambda i: (i, 0))]
    else:
        x_split = xs[0].shape[0] // tm
        ins = list(xs)
        in_specs = [pl.BlockSpec((tm, d), lambda i: (jnp.minimum(i, x_split - 1), 0)),
                    pl.BlockSpec((tm, d), lambda i: (jnp.maximum(i - x_split, 0), 0))]
    n_y = 0
    if has_res:
        y_blocks, gate_k, g_post = res
        n_y = len(y_blocks)
        for arr, tile0 in y_blocks:
            ins.append(arr)
            in_specs.append(pl.BlockSpec((tm, d), lambda i, tile0=tile0: (i + tile0, 0)))
        ins += [modtab, g_post.reshape(1, d)]
        in_specs += [modspec(gate_k), vecspec]
    if has_next:
        g_pre, shift_k, scale_k, modtab_n = nxt
        ins += [g_pre.reshape(1, d), modtab_n, modtab_n]
        in_specs += [vecspec, modspec(shift_k), modspec(scale_k)]
    n_experts = 0
    if router is not None:
        n_experts = router.shape[-1]
        rp = jnp.zeros((d, LANES), F32).at[:, :n_experts].set(router)
        rhi = rp.astype(BF16)
        rlo = (rp - rhi.astype(F32)).astype(BF16)
        ins += [rhi, rlo]
        in_specs += [pl.BlockSpec((d, LANES), lambda i: (0, 0))] * 2
    out_shape, out_specs = [], []
    rowspec = pl.BlockSpec((tm, d), lambda i: (i, 0))
    if has_res:
        out_shape.append(jax.ShapeDtypeStruct((n_tiles * tm, d), F32))
        out_specs.append(rowspec)
    if has_next:
        out_shape.append(jax.ShapeDtypeStruct((n_tiles * tm, d), h_dtype))
        out_specs.append(rowspec)
    if router is not None:
        out_shape.append(jax.ShapeDtypeStruct((n_tiles * tm, LANES), F32))
        out_specs.append(pl.BlockSpec((tm, LANES), lambda i: (i, 0)))
    outs = pl.pallas_call(
        functools.partial(_row_body, n_x=len(xs), x_split=x_split, n_y=n_y, has_next=has_next,
                          router=router is not None, n_experts=n_experts),
        grid=(n_tiles,), in_specs=in_specs, out_specs=out_specs, out_shape=out_shape,
        compiler_params=_cparams(1), name=name,
    )(*ins)
    outs = list(outs)
    x_new = outs.pop(0) if has_res else None
    h = outs.pop(0) if has_next else None
    rt = outs.pop(0) if router is not None else None
    return x_new, h, rt


def _mm_body(*refs, a_lo, a_hi, rms):
    if rms:
        a_ref, g_ref, w_ref, o_ref = refs
    else:
        a_ref, w_ref, o_ref = refs
    a = a_ref[:, a_lo:a_hi]
    if rms:
        a = _rms(a.astype(F32), g_ref[...]).astype(BF16)
    o_ref[...] = jnp.dot(a, w_ref[...], preferred_element_type=F32).astype(o_ref.dtype)


def _matmul(a, w, *, m_rows, tn, out_dtype=BF16, a_block_cols=None, a_lo=0, a_hi=None, rms_g=None, name="mm"):
    tm = MM_TILE_M
    k, n = w.shape
    if a_block_cols is None:
        a_block_cols = a.shape[1]
    if a_hi is None:
        a_hi = a_lo + k
    ins, in_specs = [a], [pl.BlockSpec((tm, a_block_cols), lambda j, i: (i, 0))]
    if rms_g is not None:
        ins.append(rms_g.reshape(1, k))
        in_specs.append(pl.BlockSpec((1, k), lambda j, i: (0, 0)))
    ins.append(w)
    in_specs.append(pl.BlockSpec((k, tn), lambda j, i: (0, j)))
    return pl.pallas_call(
        functools.partial(_mm_body, a_lo=a_lo, a_hi=a_hi, rms=rms_g is not None),
        grid=(n // tn, m_rows // tm), in_specs=in_specs,
        out_specs=pl.BlockSpec((tm, tn), lambda j, i: (i, j)),
        out_shape=jax.ShapeDtypeStruct((m_rows, n), out_dtype),
        compiler_params=_cparams(2), name=name,
    )(*ins)


def _mm_groups_body(*refs, parts_per_group, split_tile):
    n_a = sum(parts_per_group)
    a_refs = refs[:n_a]
    g_ref, w_ref, o_ref = refs[n_a:]
    parts, lo, pos = [], 0, 0
    for n_parts in parts_per_group:
        if n_parts == 1:
            a = a_refs[pos][...]
        else:
            a = jnp.where(pl.program_id(1) < split_tile, a_refs[pos][...], a_refs[pos + 1][...])
        pos += n_parts
        width = a.shape[1]
        parts.append(_rms(a.astype(F32), g_ref[:, lo:lo + width]).astype(BF16))
        lo += width
    a = jnp.concatenate(parts, axis=1)
    o_ref[...] = jnp.dot(a, w_ref[...], preferred_element_type=F32).astype(o_ref.dtype)


def _matmul_groups(groups, g, w, *, m_rows, tn, name):
    tm = MM_TILE_M
    k, n = w.shape
    split_tile = 0
    ins, in_specs = [], []
    for grp in groups:
        if len(grp) == 1:
            in_specs.append(pl.BlockSpec((tm, grp[0].shape[1]), lambda j, i: (i, 0)))
        else:
            split_tile = grp[0].shape[0] // tm
            st = split_tile
            in_specs.append(pl.BlockSpec((tm, grp[0].shape[1]), lambda j, i, st=st: (jnp.minimum(i, st - 1), 0)))
            in_specs.append(pl.BlockSpec((tm, grp[1].shape[1]), lambda j, i, st=st: (jnp.maximum(i - st, 0), 0)))
        ins += list(grp)
    in_specs += [pl.BlockSpec((1, k), lambda j, i: (0, 0)), pl.BlockSpec((k, tn), lambda j, i: (0, j))]
    return pl.pallas_call(
        functools.partial(_mm_groups_body, parts_per_group=tuple(len(grp) for grp in groups), split_tile=split_tile),
        grid=(n // tn, m_rows // tm), in_specs=in_specs,
        out_specs=pl.BlockSpec((tm, tn), lambda j, i: (i, j)),
        out_shape=jax.ShapeDtypeStruct((m_rows, n), BF16),
        compiler_params=_cparams(2), name=name,
    )(*ins, g.reshape(1, k), w)


def _mmk_body(a_ref, w_ref, o_ref, acc_ref, *, nk):
    kk = pl.program_id(2)

    @pl.when(kk == 0)
    def _():
        acc_ref[...] = jnp.zeros_like(acc_ref)

    acc_ref[...] += jnp.dot(a_ref[...], w_ref[...], preferred_element_type=F32)

    @pl.when(kk == nk - 1)
    def _():
        o_ref[...] = acc_ref[...].astype(o_ref.dtype)


def _matmul_ktiled(a, w, *, m_rows, tn, tk, out_dtype=BF16, name="mmk"):
    tm = MM_TILE_M
    k, n = w.shape
    nk = k // tk
    return pl.pallas_call(
        functools.partial(_mmk_body, nk=nk),
        grid=(n // tn, m_rows // tm, nk),
        in_specs=[pl.BlockSpec((tm, tk), lambda j, i, kk: (i, kk)),
                  pl.BlockSpec((tk, tn), lambda j, i, kk: (kk, j))],
        out_specs=pl.BlockSpec((tm, tn), lambda j, i, kk: (i, j)),
        out_shape=jax.ShapeDtypeStruct((m_rows, n), out_dtype),
        scratch_shapes=[pltpu.VMEM((tm, tn), F32)],
        compiler_params=_cparams(3), name=name,
    )(a, w)


def _glu_body(a_ref, w1_ref, w3_ref, o_ref):
    a = a_ref[...]
    h1 = jnp.dot(a, w1_ref[...], preferred_element_type=F32)
    h3 = jnp.dot(a, w3_ref[...], preferred_element_type=F32)
    o_ref[...] = (_silu(h1) * h3).astype(o_ref.dtype)


def _glu(a, w1, w3, *, m_rows, tf, name="glu"):
    tm = MM_TILE_M
    k, f = w1.shape
    return pl.pallas_call(
        _glu_body, grid=(f // tf, m_rows // tm),
        in_specs=[pl.BlockSpec((tm, k), lambda j, i: (i, 0)),
                  pl.BlockSpec((k, tf), lambda j, i: (0, j)),
                  pl.BlockSpec((k, tf), lambda j, i: (0, j))],
        out_specs=pl.BlockSpec((tm, tf), lambda j, i: (i, j)),
        out_shape=jax.ShapeDtypeStruct((m_rows, f), BF16),
        compiler_params=_cparams(2), name=name,
    )(a, w1, w3)


def _rope(x, c, s, half):
    lane = lax.broadcasted_iota(jnp.int32, x.shape, 1)
    first = (lane % (2 * half)) < half
    swapped = jnp.where(first, pltpu.roll(x, LANES - half, 1), pltpu.roll(x, half, 1))
    return x * c + swapped * s


def _prep_body(u_ref, qraw_ref, kvraw_ref, kpe_ref, cg_ref, sg_ref, cm_ref, sm_ref, gq_ref, gk_ref,
               qm_o, km_o, vm_o, qg_o, kg_o, vg_o, qn_o, kn_o, vn_o,
               *, hm, hg, hkv, hn, off_qg, off_kvg, off_n):
    cg, sg, cm, sm = cg_ref[...], sg_ref[...], cm_ref[...], sm_ref[...]
    scale_m = (MLA_NOPE + MLA_ROPE) ** -0.5 * LOG2E
    scale_g = HEAD_DIM ** -0.5 * LOG2E
    scale_n = HEAD_DIM ** -0.5
    d = HEAD_DIM
    tm = u_ref.shape[0]
    ones_rows = jnp.where(lax.broadcasted_iota(jnp.int32, (V_PAD_ROWS, tm), 0) == 0, 1.0, 0.0).astype(BF16)

    def t_bf16(v):
        return v.astype(F32).T.astype(BF16)

    kpe = _rope(kpe_ref[...].astype(F32), cm, sm, MLA_ROPE // 4)[:, :MLA_ROPE].astype(BF16)
    pe0 = hm * MLA_NOPE
    for h in range(hm):
        qm_o[h, 0:MLA_NOPE, :] = t_bf16(qraw_ref[:, h * MLA_NOPE:(h + 1) * MLA_NOPE].astype(F32) * scale_m)
        km_o[h, :, 0:MLA_NOPE] = kvraw_ref[:, h * 2 * d:h * 2 * d + MLA_NOPE]
        km_o[h, :, MLA_NOPE:MLA_NOPE + MLA_ROPE] = kpe
        vm_o[h, 0:MLA_V, :] = t_bf16(kvraw_ref[:, h * 2 * d + MLA_NOPE:(h + 1) * 2 * d])
        vm_o[h, MLA_V:MLA_V + V_PAD_ROWS, :] = ones_rows
    for j in range(hm // 2):
        pe = _rope(qraw_ref[:, pe0 + j * LANES:pe0 + (j + 1) * LANES].astype(F32), cm, sm, MLA_ROPE // 4)
        pe_t = t_bf16(pe * scale_m)
        qm_o[2 * j, MLA_NOPE:MLA_NOPE + MLA_ROPE, :] = pe_t[:MLA_ROPE]
        qm_o[2 * j + 1, MLA_NOPE:MLA_NOPE + MLA_ROPE, :] = pe_t[MLA_ROPE:]
    for h in range(hg):
        q = _rms(u_ref[:, off_qg + h * d:off_qg + (h + 1) * d].astype(F32), gq_ref[...])
        qg_o[h] = t_bf16(_rope(q, cg, sg, d // 4) * scale_g)
    for h in range(hkv):
        k = _rms(u_ref[:, off_kvg + h * d:off_kvg + (h + 1) * d].astype(F32), gk_ref[...])
        kg_o[h] = _rope(k, cg, sg, d // 4).astype(BF16)
        vg_o[h, 0:d, :] = t_bf16(u_ref[:, off_kvg + (hkv + h) * d:off_kvg + (hkv + h + 1) * d])
        vg_o[h, d:d + V_PAD_ROWS, :] = ones_rows
    for h in range(hn):
        qn_o[h] = (u_ref[:, off_n + h * d:off_n + (h + 1) * d].astype(F32) * scale_n).astype(BF16)
        kn_o[h] = u_ref[:, off_n + (hn + h) * d:off_n + (hn + h + 1) * d]
        vn_o[h] = u_ref[:, off_n + (2 * hn + h) * d:off_n + (2 * hn + h + 1) * d]


def _prep(u, qraw, kvraw, kpe, tables, g_q, g_k, *, dims, tk_m, tk_g):
    tm = ROW_TILE
    t_tot = u.shape[0]
    b, s, l = dims["B"], dims["S"], dims["L"]
    hm, hg, hkv, hn = dims["HM"], dims["HG"], dims["HKV"], dims["HN"]
    tk_len = s + l
    n_lat = b * s // tm
    spb, lpb = s // tm, l // tm

    def kmap(i):
        lat = i < n_lat
        j = i - n_lat
        bb = jnp.where(lat, i // spb, j // lpb)
        pos = jnp.where(lat, i % spb, spb + j % lpb)
        return bb, pos

    def kspec(h, dk):
        def im(i):
            bb, pos = kmap(i)
            return (bb, 0, pos, 0)
        return pl.BlockSpec((None, h, tm, dk), im)

    def vtspec(h, dv, tk):
        per = tk // tm

        def im(i):
            bb, pos = kmap(i)
            return (bb, 0, pos // per, 0, pos % per)
        return pl.BlockSpec((None, h, None, dv + V_PAD_ROWS, tm), im)

    def qtspec(h, dk):
        return pl.BlockSpec((h, None, dk, tm), lambda i: (0, i, 0, 0))

    def qspec(h, dk):
        return pl.BlockSpec((h, tm, dk), lambda i: (0, i, 0))

    def full(arr):
        return pl.BlockSpec((tm, arr.shape[1]), lambda i: (i, 0))

    dk_m = MLA_NOPE + MLA_ROPE
    d = HEAD_DIM
    out_shape = [
        jax.ShapeDtypeStruct((hm, t_tot // tm, dk_m, tm), BF16), jax.ShapeDtypeStruct((b, hm, tk_len, dk_m), BF16),
        jax.ShapeDtypeStruct((b, hm, tk_len // tk_m, MLA_V + V_PAD_ROWS, tk_m), BF16),
        jax.ShapeDtypeStruct((hg, t_tot // tm, d, tm), BF16), jax.ShapeDtypeStruct((b, hkv, tk_len, d), BF16),
        jax.ShapeDtypeStruct((b, hkv, tk_len // tk_g, d + V_PAD_ROWS, tk_g), BF16),
        jax.ShapeDtypeStruct((hn, t_tot, d), BF16), jax.ShapeDtypeStruct((b, hn, tk_len, d), BF16),
        jax.ShapeDtypeStruct((b, hn, tk_len, d), BF16),
    ]
    out_specs = [qtspec(hm, dk_m), kspec(hm, dk_m), vtspec(hm, MLA_V, tk_m),
                 qtspec(hg, d), kspec(hkv, d), vtspec(hkv, d, tk_g),
                 qspec(hn, d), kspec(hn, d), kspec(hn, d)]
    vec = pl.BlockSpec((1, d), lambda i: (0, 0))
    return pl.pallas_call(
        functools.partial(_prep_body, hm=hm, hg=hg, hkv=hkv, hn=hn,
                          off_qg=dims["OFF_QG"], off_kvg=dims["OFF_KVG"], off_n=dims["OFF_N"]),
        grid=(t_tot // tm,),
        in_specs=[full(u), full(qraw), full(kvraw), full(kpe)] + [full(t) for t in tables] + [vec, vec],
        out_specs=out_specs, out_shape=out_shape,
        compiler_params=_cparams(1), name="prep",
    )(u, qraw, kvraw, kpe, *tables, g_q.reshape(1, d), g_k.reshape(1, d))


M_INIT = -1e30


def _q_tile(qt_ref, g, tpi, t):
    parts = [qt_ref[gi, t * tpi + j] for gi in range(g) for j in range(tpi)]
    return parts[0] if len(parts) == 1 else jnp.concatenate(parts, axis=1)


def _attn_body(qt_ref, k_ref, vt_ref, o_ref, s0, s1, p0, p1, a0, a1, m_sc, acc_sc,
               *, g, tpi, tq, tk, nk, dv, nt):
    s_buf, p_buf, a_buf = (s0, s1), (p0, p1), (a0, a1)
    n_items = nt * nk

    def split(it):
        if isinstance(it, int):
            return it // nk, it % nk
        return lax.div(it, jnp.int32(nk)), lax.rem(it, jnp.int32(nk))

    def scores(it):
        t, c = split(it)
        off = c * tk if isinstance(c, int) else pl.multiple_of(c * tk, tk)
        return jnp.dot(k_ref[pl.ds(off, tk), :], _q_tile(qt_ref, g, tpi, t), preferred_element_type=F32)

    def softmax(it, slot):
        t, _ = split(it)
        s = s_buf[slot][...]
        m_prev = m_sc[t]
        m_new = jnp.maximum(m_prev, jnp.max(s, axis=0, keepdims=True))
        p_buf[slot][...] = jnp.exp2(s - m_new).astype(BF16)
        a_buf[slot][...] = jnp.exp2(m_prev - m_new)
        m_sc[t] = m_new

    def values(it, slot):
        t, c = split(it)
        acc_sc[t] = a_buf[slot][...] * acc_sc[t] + jnp.dot(vt_ref[c], p_buf[slot][...], preferred_element_type=F32)

    def stage(it, slot):
        s_buf[1 - slot][...] = scores(it + 1)
        softmax(it, slot)
        values(it - 1, 1 - slot)

    m_sc[...] = jnp.full_like(m_sc, M_INIT)
    acc_sc[...] = jnp.zeros_like(acc_sc)
    s_buf[0][...] = scores(0)
    softmax(0, 0)
    if n_items > 1:
        s_buf[1][...] = scores(1)
        n_steady = n_items - 2

        def pair(u, carry):
            it = 2 * u + 1
            stage(it, 1)
            stage(it + 1, 0)
            return carry

        lax.fori_loop(0, n_steady // 2, pair, 0)
        if n_steady % 2:
            stage(n_items - 2, 1)
        softmax(n_items - 1, (n_items - 1) % 2)
        values(n_items - 2, (n_items - 2) % 2)
    values(n_items - 1, (n_items - 1) % 2)

    def finish(t, carry):
        acc = acc_sc[t]
        o_t = acc[0:dv] / acc[dv:dv + 1]
        for gi in range(g):
            for j in range(tpi):
                row = pl.multiple_of((t * tpi + j) * tq, tq)
                col = (gi * tpi + j) * tq
                o_ref[pl.ds(row, tq), gi * dv:(gi + 1) * dv] = o_t[:, col:col + tq].T.astype(o_ref.dtype)
        return carry

    lax.fori_loop(0, nt, finish, 0)


def _attention(qt, k, vt, *, dims, g, tpi, nt, name):
    b, s_len, l_len = dims["B"], dims["S"], dims["L"]
    hk = k.shape[1]
    dk, tq = qt.shape[2:]
    nk, dvp, tk = vt.shape[2:]
    dv = dvp - V_PAD_ROWS
    steps_per_b = s_len // (nt * tpi * tq)
    n = g * tpi * tq
    return pl.pallas_call(
        functools.partial(_attn_body, g=g, tpi=tpi, tq=tq, tk=tk, nk=nk, dv=dv, nt=nt),
        grid=(b, hk, steps_per_b),
        in_specs=[pl.BlockSpec((g, nt * tpi, dk, tq), lambda bb, h, i: (h, bb * steps_per_b + i, 0, 0)),
                  pl.BlockSpec((None, None, s_len + l_len, dk), lambda bb, h, i: (bb, h, 0, 0)),
                  pl.BlockSpec((None, None, nk, dvp, tk), lambda bb, h, i: (bb, h, 0, 0, 0))],
        out_specs=pl.BlockSpec((nt * tpi * tq, g * dv), lambda bb, h, i: (bb * steps_per_b + i, h)),
        out_shape=jax.ShapeDtypeStruct((b * s_len, hk * g * dv), BF16),
        scratch_shapes=[pltpu.VMEM((tk, n), F32), pltpu.VMEM((tk, n), F32),
                        pltpu.VMEM((tk, n), BF16), pltpu.VMEM((tk, n), BF16),
                        pltpu.VMEM((1, n), F32), pltpu.VMEM((1, n), F32),
                        pltpu.VMEM((nt, 1, n), F32), pltpu.VMEM((nt, dvp, n), F32)],
        compiler_params=_cparams(3), name=name,
    )(qt, k, vt)


def _attn_ctx_body(qt_ref, k_ref, vt_ref, o_ref, *, g, tq, dv, ctx_rows):
    qt = _q_tile(qt_ref, g, 1, 0)
    tk = vt_ref.shape[-1]
    s = jnp.dot(k_ref[...], qt, preferred_element_type=F32)
    p = jnp.exp2(s - jnp.max(s, axis=0, keepdims=True)).astype(BF16)
    acc = jnp.dot(vt_ref[:, tk - ctx_rows:tk], p, preferred_element_type=F32)
    o_t = acc[0:dv] / acc[dv:dv + 1]
    for gi in range(g):
        o_ref[:, gi * dv:(gi + 1) * dv] = o_t[:, gi * tq:(gi + 1) * tq].T.astype(o_ref.dtype)


def _attention_ctx(qt, k, vt, *, dims, g, name):
    b, s_len, l_len = dims["B"], dims["S"], dims["L"]
    hk = k.shape[1]
    dk, tq = qt.shape[2:]
    nk, dvp, tk = vt.shape[2:]
    dv = dvp - V_PAD_ROWS
    cpb = l_len // tq
    lat_tiles = b * s_len // tq
    assert l_len % tq == 0 and l_len <= tk and s_len % l_len == 0
    return pl.pallas_call(
        functools.partial(_attn_ctx_body, g=g, tq=tq, dv=dv, ctx_rows=l_len),
        grid=(b, hk, cpb),
        in_specs=[pl.BlockSpec((g, 1, dk, tq), lambda bb, h, i: (h, lat_tiles + bb * cpb + i, 0, 0)),
                  pl.BlockSpec((None, None, l_len, dk), lambda bb, h, i: (bb, h, s_len // l_len, 0)),
                  pl.BlockSpec((None, None, None, dvp, tk), lambda bb, h, i: (bb, h, nk - 1, 0, 0))],
        out_specs=pl.BlockSpec((tq, g * dv), lambda bb, h, i: (bb * cpb + i, h)),
        out_shape=jax.ShapeDtypeStruct((b * l_len, hk * g * dv), BF16),
        compiler_params=_cparams(3), name=name,
    )(qt, k, vt)


def _na_body(q_ref, k_ref, v_ref, bias_ref, o_ref, *, rows_per_step, n_grid_rows, kh, s_len, l_len, lat_steps,
             with_ctx):
    w = GRID_W
    j = pl.program_id(2)
    kc = k_ref[s_len:s_len + l_len, :]
    vc = v_ref[s_len:s_len + l_len, :]
    nt = (((1,), (1,)), ((), ()))

    def latent_step():
        span = kh + rows_per_step
        r0 = j * rows_per_step
        u0 = jnp.clip(r0 - kh // 2, 0, n_grid_rows - span)
        koff = pl.multiple_of(u0 * w, w)
        kw = k_ref[pl.ds(koff, span * w), :]
        vw = v_ref[pl.ds(koff, span * w), :]
        q = q_ref[...]
        s_loc = lax.dot_general(q, kw, nt, preferred_element_type=F32) + bias_ref[(r0 - u0) // rows_per_step]
        s_ctx = lax.dot_general(q, kc, nt, preferred_element_type=F32)
        m = jnp.maximum(jnp.max(s_loc, axis=-1, keepdims=True), jnp.max(s_ctx, axis=-1, keepdims=True))
        p_loc = jnp.exp(s_loc - m)
        p_ctx = jnp.exp(s_ctx - m)
        denom = jnp.sum(p_loc, axis=-1, keepdims=True) + jnp.sum(p_ctx, axis=-1, keepdims=True)
        o = (jnp.dot(p_loc.astype(BF16), vw, preferred_element_type=F32)
             + jnp.dot(p_ctx.astype(BF16), vc, preferred_element_type=F32))
        o_ref[...] = (o / denom).astype(o_ref.dtype)

    def ctx_step():
        s = lax.dot_general(q_ref[...], kc, nt, preferred_element_type=F32)
        p = jnp.exp(s - jnp.max(s, axis=-1, keepdims=True))
        o = jnp.dot(p.astype(BF16), vc, preferred_element_type=F32)
        o_ref[...] = (o / jnp.sum(p, axis=-1, keepdims=True)).astype(o_ref.dtype)

    if with_ctx:
        pl.when(j < lat_steps)(latent_step)
        pl.when(j >= lat_steps)(ctx_step)
    else:
        latent_step()


def _na_bias(rpb, kh, n_grid_rows, rps):
    span = kh + rps
    col = np.arange(GRID_W)
    start_c = np.clip(col - NA_KW // 2, 0, GRID_W - NA_KW)
    col_mask = (col[None, :] >= start_c[:, None]) & (col[None, :] < start_c[:, None] + NA_KW)
    dc_idx = np.clip(col[None, :] - col[:, None] + NA_KW - 1, 0, 2 * NA_KW - 2)
    n_var = kh // rps + 1
    dr_idx = np.full((n_var, rps, span), -1)
    seen = set()
    for r0 in range(0, n_grid_rows, rps):
        u0 = int(np.clip(r0 - kh // 2, 0, n_grid_rows - span))
        var = (r0 - u0) // rps
        assert (r0 - u0) % rps == 0 and 0 <= var < n_var
        table = np.full((rps, span), -1)
        for qr in range(rps):
            r = r0 + qr
            start = int(np.clip(r - kh // 2, 0, n_grid_rows - kh))
            for kr in range(span):
                if start <= u0 + kr < start + kh:
                    table[qr, kr] = u0 + kr - r + NA_KH - 1
        assert var not in seen or np.array_equal(dr_idx[var], table)
        seen.add(var)
        dr_idx[var] = table
    row_mask = dr_idx >= 0
    oh_r = (dr_idx[..., None] == np.arange(2 * NA_KH - 1)).astype(np.float32)
    oh_c = (dc_idx[:, :, None] == np.arange(2 * NA_KW - 1)[None, None, :]).astype(np.float32)
    bias = jnp.einsum("vrja,hab,qkb->hvrqjk", oh_r, rpb.astype(F32), oh_c, precision=lax.Precision.HIGHEST)
    mask = row_mask[None, :, :, None, :, None] & col_mask[None, None, None, :, None, :]
    bias = jnp.where(mask, bias, NEG_INF)
    return bias.reshape(rpb.shape[0], n_var, rps * GRID_W, span * GRID_W)


def _na(q, k, v, rpb, *, dims, with_ctx):
    b, s, l, hn = dims["B"], dims["S"], dims["L"], dims["HN"]
    r_tot = s // GRID_W
    kh = min(NA_KH, r_tot)
    tq = ROW_TILE
    rps = tq // GRID_W
    assert r_tot >= kh + rps and kh % rps == 0 and (kh // 2) % rps == 0
    bias = _na_bias(rpb, kh, r_tot, rps)
    d = HEAD_DIM
    spb = r_tot // rps
    cpb = l // tq if with_ctx else 0
    lat_tiles = b * spb
    rows = b * (s + (l if with_ctx else 0))

    def qtile(bb, j):
        return jnp.where(j < spb, bb * spb + j, lat_tiles + bb * cpb + (j - spb))

    return pl.pallas_call(
        functools.partial(_na_body, rows_per_step=rps, n_grid_rows=r_tot, kh=kh, s_len=s, l_len=l, lat_steps=spb,
                          with_ctx=with_ctx),
        grid=(b, hn, spb + cpb),
        in_specs=[pl.BlockSpec((None, tq, d), lambda bb, h, j: (h, qtile(bb, j), 0)),
                  pl.BlockSpec((None, None, s + l, d), lambda bb, h, j: (bb, h, 0, 0)),
                  pl.BlockSpec((None, None, s + l, d), lambda bb, h, j: (bb, h, 0, 0)),
                  pl.BlockSpec((None,) + bias.shape[1:], lambda bb, h, j: (h, 0, 0, 0))],
        out_specs=pl.BlockSpec((tq, d), lambda bb, h, j: (qtile(bb, j), h)),
        out_shape=jax.ShapeDtypeStruct((rows, hn * d), BF16),
        compiler_params=_cparams(3), name="na",
    )(q, k, v, bias)


def _moe_body(te_ref, tv_ref, src_ref, dst_ref, h_ref, w1_ref, w3_ref, w2_ref, gate_ref, out_ref,
              xg, xb, acc_ref, ys, gsem, ssem, *, nf, tm):
    i, f = pl.program_id(0), pl.program_id(1)
    n_tiles = pl.num_programs(0)
    valid = tv_ref[i] > 0
    chunk = tm // nf
    dump1 = out_ref.shape[0] - tm

    def gather_copy(r, row):
        return pltpu.make_async_copy(h_ref.at[pl.ds(row, 1)], xg.at[pl.ds(r, 1)], gsem)

    def scatter_copy(slot, r, row):
        return pltpu.make_async_copy(ys.at[slot, pl.ds(r, 1)], out_ref.at[pl.ds(row, 1)], ssem.at[slot])

    def wait_gather():
        pltpu.make_async_copy(h_ref.at[pl.ds(0, tm)], xg, gsem).wait()

    def wait_scatter(slot):
        pltpu.make_async_copy(ys.at[slot], out_ref.at[pl.ds(0, tm)], ssem.at[slot]).wait()

    def issue_chunks():
        g_tile = jnp.minimum(i + 1, n_tiles - 1)
        s_tile = jnp.maximum(i - 1, 0)
        s_slot = (i + 1) % 2
        for r in range(chunk):
            rr = f * chunk + r
            gather_copy(rr, src_ref[g_tile * tm + rr]).start()
            row = jnp.where(i > 0, dst_ref[s_tile * tm + rr], dump1 + rr)
            scatter_copy(s_slot, rr, row).start()

    @pl.when(jnp.logical_and(i == 0, f == 0))
    def _():
        ys[...] = jnp.zeros_like(ys)

        def go(r, c):
            gather_copy(r, src_ref[r]).start()
            return c
        lax.fori_loop(0, tm, go, 0)

    @pl.when(f == 0)
    def _():
        wait_gather()
        xb[...] = xg[...].astype(BF16)
        acc_ref[...] = jnp.zeros_like(acc_ref)

    @pl.when(valid)
    def _():
        issue_chunks()
        x = xb[...]
        h1 = jnp.dot(x, w1_ref[...], preferred_element_type=F32)
        h3 = jnp.dot(x, w3_ref[...], preferred_element_type=F32)
        hm = (_silu(h1) * h3).astype(BF16)
        acc_ref[...] += jnp.dot(hm, w2_ref[...], preferred_element_type=F32)

    @pl.when(jnp.logical_not(valid))
    def _():
        issue_chunks()

    @pl.when(f == nf - 1)
    def _():
        @pl.when(i > 0)
        def _():
            wait_scatter(i % 2)

        ys[i % 2] = acc_ref[...] * gate_ref[...]

        @pl.when(i == n_tiles - 1)
        def _():
            def go(r, c):
                scatter_copy(i % 2, r, dst_ref[i * tm + r]).start()
                return c
            lax.fori_loop(0, tm, go, 0)
            wait_scatter(i % 2)
            wait_scatter((i + 1) % 2)
            wait_gather()


def _moe_ffn(h, w1, w3, w2, plan, *, n_out_rows, tf):
    src, dst, gate_sorted, tile_expert, tile_valid = plan
    tm = MOE_TILE_M
    p_tot = src.shape[0]
    d = h.shape[1]
    f_dim = w1.shape[-1]
    nf = f_dim // tf
    assert tm % nf == 0

    def fidx(i, f, tv):
        return jnp.where(tv[i] > 0, f, nf - 1)

    return pl.pallas_call(
        functools.partial(_moe_body, nf=nf, tm=tm),
        grid_spec=pltpu.PrefetchScalarGridSpec(
            num_scalar_prefetch=4, grid=(p_tot // tm, nf),
            in_specs=[pl.BlockSpec(memory_space=pl.ANY),
                      pl.BlockSpec((None, d, tf), lambda i, f, te, tv, sr, ds: (te[i], 0, fidx(i, f, tv))),
                      pl.BlockSpec((None, d, tf), lambda i, f, te, tv, sr, ds: (te[i], 0, fidx(i, f, tv))),
                      pl.BlockSpec((None, tf, d), lambda i, f, te, tv, sr, ds: (te[i], fidx(i, f, tv), 0)),
                      pl.BlockSpec((tm, 1), lambda i, f, te, tv, sr, ds: (i, 0))],
            out_specs=pl.BlockSpec(memory_space=pl.ANY),
            scratch_shapes=[pltpu.VMEM((tm, d), F32), pltpu.VMEM((tm, d), BF16), pltpu.VMEM((tm, d), F32),
                            pltpu.VMEM((2, tm, d), F32), pltpu.SemaphoreType.DMA(()),
                            pltpu.SemaphoreType.DMA((2,))]),
        out_shape=jax.ShapeDtypeStruct((n_out_rows + 2 * tm, d), F32),
        compiler_params=_cparams(2), name="moe_ffn",
    )(tile_expert, tile_valid, src, dst, h, w1, w3, w2, gate_sorted)


def _moe_plan(e_idx, gates, n_experts, tm):
    flat_e = e_idx.reshape(-1)
    n_slots = flat_e.shape[0]
    onehot = (flat_e[:, None] == jnp.arange(n_experts)[None, :]).astype(jnp.int32)
    csum = jnp.cumsum(onehot, axis=0)
    rank = jnp.sum(csum * onehot, axis=1) - 1
    counts = csum[-1]
    tiles_per = (counts + tm - 1) // tm
    cum_tiles = jnp.cumsum(tiles_per)
    row_start = (cum_tiles - tiles_per) * tm
    dest = jnp.sum(row_start[None, :] * onehot, axis=1) + rank
    n_tiles = n_slots // tm + n_experts
    n_tiles += n_tiles % 2 == 0
    p_tot = n_tiles * tm
    te = jnp.sum((jnp.arange(n_tiles)[:, None] >= cum_tiles[None, :]).astype(jnp.int32), axis=1)
    tile_valid = (te < n_experts).astype(jnp.int32)
    tile_expert = jnp.minimum(te, n_experts - 1)
    slot = jnp.arange(n_slots, dtype=jnp.int32)
    n_tok = n_slots // TOP_K
    src = jnp.zeros((p_tot,), jnp.int32).at[dest].set(slot // TOP_K)
    pos = jnp.arange(p_tot, dtype=jnp.int32)
    dump = n_slots + ((pos // tm) % 2) * tm + pos % tm
    dst = dump.at[dest].set((slot % TOP_K) * n_tok + slot // TOP_K)
    gate_sorted = jnp.zeros((p_tot,), F32).at[dest].set(gates.reshape(-1)).reshape(p_tot, 1)
    return src, dst, gate_sorted, tile_expert, tile_valid


def _rope_tables(n_batch, s_len, l_len):
    t = np.arange(s_len)
    rows, cols = t // GRID_W, t % GRID_W

    def table(width):
        half = width // 2
        quarter = half // 2
        lane = np.arange(width)
        pos = np.where((lane // half)[None, :] == 0, rows[:, None], cols[:, None]).astype(np.float64)
        fi = (lane % half) % quarter
        inv = ROPE_THETA ** (-(2.0 * fi) / half)
        ang = (pos.astype(np.float32) * inv.astype(np.float32)[None, :]).astype(np.float32)
        sign = np.where((lane % half) < quarter, -1.0, 1.0)
        return np.cos(ang).astype(np.float32), (np.sin(ang) * sign[None, :]).astype(np.float32)

    cg, sg = table(HEAD_DIM)
    cm, sm = table(MLA_ROPE)
    cm, sm = np.tile(cm, (1, 2)), np.tile(sm, (1, 2))

    def flat(lat, fill):
        return np.concatenate([np.tile(lat, (n_batch, 1)),
                               np.full((n_batch * l_len, LANES), fill, np.float32)], axis=0)

    return tuple(jnp.asarray(a) for a in (flat(cg, 1.0), flat(sg, 0.0), flat(cm, 1.0), flat(sm, 0.0)))


def _round_up(a, m):
    return (a + m - 1) // m * m


def _pick_tile(n, pref):
    for t in pref:
        if n % t == 0:
            return t
    return n


def kernel(x, c, ctx, c_ctx, w_ada, b_ada, g_pre_mix, g_post_mix, g_pre_ffn, g_post_ffn, w_in, g_q_a, g_kv_a, w_q_up, w_kv_up, g_q_gqa, g_k_gqa, na_rpb, g_grp, w_out, ffn_w1, ffn_w3, ffn_w2, router, moe_w1, moe_w3, moe_w2):
    n_batch, s_len, d = x.shape
    l_len = ctx.shape[1]
    depth = w_ada.shape[0]
    q_lora, kv_lora = g_q_a.shape[-1], g_kv_a.shape[-1]
    dk_m = MLA_NOPE + MLA_ROPE
    hm = w_q_up.shape[-1] // dk_m
    hg = 3 * d // (8 * HEAD_DIM)
    hkv = hg // 3
    hn = d // (4 * HEAD_DIM)
    g_gqa = hg // hkv
    n_lat, n_ctx = n_batch * s_len, n_batch * l_len
    t_tot = n_lat + n_ctx
    assert s_len % MM_TILE_M == 0 and n_ctx % MM_TILE_M == 0 and l_len % ROW_TILE == 0
    assert hm % 2 == 0 and n_batch + 1 <= SUBLANES

    off_qg = q_lora + kv_lora
    off_kvg = off_qg + hg * HEAD_DIM
    off_n = off_kvg + 2 * hkv * HEAD_DIM
    n_main = off_n + 3 * hn * HEAD_DIM
    dims = dict(B=n_batch, S=s_len, L=l_len, HM=hm, HG=hg, HKV=hkv, HN=hn,
                OFF_QG=off_qg, OFF_KVG=off_kvg, OFF_N=off_n)

    tables = _rope_tables(n_batch, s_len, l_len)
    tk_all = s_len + l_len
    tk_m = _pick_tile(tk_all, (768, 512, 256))
    tk_g = tk_m

    mods = _ada(jnp.concatenate([c_ctx[None, :], c], axis=0), w_ada, b_ada)

    lat_tiles, all_tiles = n_lat // ROW_TILE, t_tot // ROW_TILE
    tiles_per_b = s_len // ROW_TILE

    def group_of(i):
        return jnp.where(i < lat_tiles, 1 + i // tiles_per_b, 0)

    def modtab_of(i):
        return mods[i, :n_batch + 1].reshape((n_batch + 1) * 6, 1, d)

    xs = (x.reshape(n_lat, d), ctx.reshape(n_ctx, d))
    modtab = modtab_of(0)
    _, h, _ = _row_call(xs, n_tiles=all_tiles, group_fn=group_of, modtab=modtab,
                        nxt=(g_pre_mix[0], 0, 1, modtab), name="row_in")

    for i in range(depth):
        last = i == depth - 1
        modtab = modtab_of(i)
        m_out = n_lat if last else t_tot
        out_tiles = m_out // ROW_TILE

        wi = w_in[i]
        c_kpe = q_lora + kv_lora
        w_main = jnp.concatenate([wi[:, :c_kpe], wi[:, c_kpe + MLA_ROPE:]], axis=1).astype(BF16)
        w_kpe = jnp.zeros((d, LANES), BF16).at[:, :MLA_ROPE].set(wi[:, c_kpe:c_kpe + MLA_ROPE].astype(BF16))
        u = _matmul(h, w_main, m_rows=t_tot, tn=_pick_tile(n_main, (1024, 512, 256, 128)), name="mm_in")
        kpe = _matmul(h, w_kpe, m_rows=t_tot, tn=LANES, name="mm_kpe")
        wq = w_q_up[i].reshape(q_lora, hm, dk_m)
        wq = jnp.concatenate([wq[:, :, :MLA_NOPE].reshape(q_lora, hm * MLA_NOPE),
                              wq[:, :, MLA_NOPE:].reshape(q_lora, hm * MLA_ROPE)], axis=1).astype(BF16)
        a_cols = _round_up(off_qg, LANES)
        qraw = _matmul(u, wq, m_rows=t_tot, tn=_pick_tile(hm * dk_m, (768, 384, 128)), a_block_cols=a_cols,
                       a_lo=0, a_hi=q_lora, rms_g=g_q_a[i], name="mm_qup")
        kvraw = _matmul(u, w_kv_up[i].astype(BF16), m_rows=t_tot,
                        tn=_pick_tile(hm * (MLA_NOPE + MLA_V), (1024, 768, 512, 256)), a_block_cols=a_cols,
                        a_lo=q_lora, a_hi=q_lora + kv_lora, rms_g=g_kv_a[i], name="mm_kvup")
        qm, km, vm, qg, kg, vg, qn, kn, vn = _prep(u, qraw, kvraw, kpe, tables, g_q_gqa[i], g_k_gqa[i],
                                                   dims=dims, tk_m=tk_m, tk_g=tk_g)

        q_tiles_per_b = s_len // ROW_TILE
        o_m = (_attention(qm, km, vm, dims=dims, g=1, tpi=MLA_TILES_PER_ITEM,
                          nt=q_tiles_per_b // MLA_TILES_PER_ITEM, name="attn_mla"),)
        o_g = (_attention(qg, kg, vg, dims=dims, g=g_gqa, tpi=1, nt=_pick_tile(q_tiles_per_b, (16, 8, 4, 2)),
                          name="attn_gqa"),)
        if not last:
            o_m += (_attention_ctx(qm, km, vm, dims=dims, g=1, name="attn_mla_ctx"),)
            o_g += (_attention_ctx(qg, kg, vg, dims=dims, g=g_gqa, name="attn_gqa_ctx"),)
        o_n = (_na(qn, kn, vn, na_rpb[i], dims=dims, with_ctx=not last),)

        y = _matmul_groups((o_m, o_g, o_n), g_grp[i], w_out[i].astype(BF16), m_rows=m_out,
                           tn=_pick_tile(d, (1024, 512, 256)), name="mm_out")

        j = i // 2
        moe = i % 2 == 1
        res = ([(y, 0)], 2, g_post_mix[i])
        nxt = (g_pre_ffn[i], 3, 4, modtab)
        if moe:
            x_all, hf, rt = _row_call(xs, n_tiles=out_tiles, group_fn=group_of, modtab=modtab, res=res, nxt=nxt,
                                      router=router[j], h_dtype=F32, name="row_mix")
            n_experts = router.shape[-1]
            gates = rt[:, 0:TOP_K]
            e_idx = rt[:, TOP_K:2 * TOP_K].astype(jnp.int32)
            plan = _moe_plan(e_idx, gates, n_experts, MOE_TILE_M)
            f_out = _moe_ffn(hf, moe_w1[j].astype(BF16), moe_w3[j].astype(BF16), moe_w2[j].astype(BF16), plan,
                             n_out_rows=TOP_K * m_out, tf=_pick_tile(moe_w1.shape[-1], (256, 128)))
            f_blocks = [(f_out, k * out_tiles) for k in range(TOP_K)]
        else:
            x_all, h, _ = _row_call(xs, n_tiles=out_tiles, group_fn=group_of, modtab=modtab, res=res, nxt=nxt,
                                    name="row_mix")
            f_dim = ffn_w1.shape[-1]
            f_pad = _round_up(f_dim, 1024)
            pad = ((0, 0), (0, f_pad - f_dim))
            w1 = jnp.pad(ffn_w1[j].astype(BF16), pad)
            w3 = jnp.pad(ffn_w3[j].astype(BF16), pad)
            w2 = jnp.pad(ffn_w2[j].astype(BF16), (pad[1], pad[0]))
            gact = _glu(h, w1, w3, m_rows=m_out, tf=1024, name="ffn_glu")
            tk2 = f_pad // 4 if (f_pad // 4) % LANES == 0 else f_pad
            f_out = _matmul_ktiled(gact, w2, m_rows=m_out, tn=_pick_tile(d, (1024, 512, 256)), tk=tk2, name="ffn_down")
            f_blocks = [(f_out, 0)]
        xs = (x_all,)

        res = (f_blocks, 5, g_post_ffn[i])
        if last:
            x_all, _, _ = _row_call(xs, n_tiles=out_tiles, group_fn=group_of, modtab=modtab, res=res, name="row_ffn")
        else:
            x_all, h, _ = _row_call(xs, n_tiles=out_tiles, group_fn=group_of, modtab=modtab, res=res,
                                    nxt=(g_pre_mix[i + 1], 0, 1, modtab_of(i + 1)), name="row_ffn")
        xs = (x_all,)
    return xs[0][:n_lat].reshape(n_batch, s_len, d)
```

```python
import functools
import math

import jax
import jax.numpy as jnp
import numpy as np
from jax import lax
from jax.experimental import pallas as pl
from jax.experimental.pallas import tpu as pltpu

F32 = jnp.float32
BF16 = jnp.bfloat16

GRID_W = 64
HEAD_DIM = 128
ROPE_THETA = 10000.0
EPS = 1e-6
NEG_INF = -1e30
MLA_NOPE = 128
MLA_ROPE = 64
MLA_V = 128
NA_KH = 8
NA_KW = 16
TOP_K = 2
LOG2E = math.log2(math.e)

LANES = 128
SUBLANES = 8
BF16_SUBLANES = 16
VMEM_LIMIT_BYTES = 56 * 1024 * 1024

ROW_TILE = 256
MM_TILE_M = 512
MOE_TILE_M = 512
V_PAD_ROWS = BF16_SUBLANES
MLA_TILES_PER_ITEM = 1


def _cparams(n_axes):
    return pltpu.CompilerParams(dimension_semantics=("arbitrary",) * n_axes,
                                vmem_limit_bytes=VMEM_LIMIT_BYTES)


def _rms(v, g):
    ms = jnp.mean(v * v, axis=-1, keepdims=True)
    return v * lax.rsqrt(ms + EPS) * g


def _silu(v):
    return v * jax.nn.sigmoid(v)


ADA_K_CHUNK = 64


def _ada_body(ct_ref, w_ref, b_ref, o_ref, sb_ref, *, n_rows):
    d, tn = w_ref.shape

    @pl.when(jnp.logical_and(pl.program_id(0) == 0, pl.program_id(1) == 0))
    def _():
        s = _silu(ct_ref[...])
        for r in range(n_rows):
            sb_ref[r] = jnp.broadcast_to(s[:, r:r + 1], (d, LANES))

    def step(i, accs):
        off = pl.multiple_of(i * ADA_K_CHUNK, ADA_K_CHUNK)
        w = w_ref[pl.ds(off, ADA_K_CHUNK), :]
        out = []
        for r in range(n_rows):
            sb = sb_ref[r, pl.ds(off, ADA_K_CHUNK), :]
            prod = w * jnp.concatenate([sb] * (tn // LANES), axis=1)
            part = prod[0:SUBLANES]
            for j in range(1, ADA_K_CHUNK // SUBLANES):
                part = part + prod[j * SUBLANES:(j + 1) * SUBLANES]
            out.append(accs[r] + part)
        return tuple(out)

    accs = lax.fori_loop(0, d // ADA_K_CHUNK, step,
                         tuple(jnp.zeros((SUBLANES, tn), F32) for _ in range(n_rows)), unroll=2)
    rows = [jnp.sum(a, axis=0, keepdims=True) for a in accs]
    rows.append(jnp.zeros((SUBLANES - n_rows, tn), F32))
    o_ref[...] = jnp.concatenate(rows, axis=0) + b_ref[...]


def _ada(cond, w_ada, b_ada):
    n_rows, d = cond.shape
    depth, _, n = w_ada.shape
    tn = 512
    ct = jnp.zeros((d, SUBLANES), F32).at[:, :n_rows].set(cond.T)
    return pl.pallas_call(
        functools.partial(_ada_body, n_rows=n_rows),
        grid=(depth, n // tn),
        in_specs=[pl.BlockSpec((d, SUBLANES), lambda l, j: (0, 0)),
                  pl.BlockSpec((None, d, tn), lambda l, j: (l, 0, j)),
                  pl.BlockSpec((None, 1, tn), lambda l, j: (l, 0, j))],
        out_specs=pl.BlockSpec((None, SUBLANES, tn), lambda l, j: (l, 0, j)),
        out_shape=jax.ShapeDtypeStruct((depth, SUBLANES, n), F32),
        scratch_shapes=[pltpu.VMEM((n_rows, d, LANES), F32)],
        compiler_params=_cparams(2),
        name="ada",
    )(ct, w_ada, b_ada.reshape(depth, 1, n))


def _row_body(*refs, n_x, x_split, n_y, y_weighted, has_next, router, n_experts):
    it = iter(refs)
    x_refs = [next(it) for _ in range(n_x)]
    has_res = n_y > 0
    if has_res:
        y_refs = [next(it) for _ in range(n_y)]
        if y_weighted:
            yw_ref = next(it)
        gate_ref, gpost_ref = next(it), next(it)
    if has_next:
        gpre_ref, shift_ref, scale_ref = next(it), next(it), next(it)
    if router:
        rhi_ref, rlo_ref = next(it), next(it)
    if has_res:
        xo_ref = next(it)
    if has_next:
        h_ref = next(it)
    if router:
        rt_ref = next(it)

    if n_x == 1:
        x = x_refs[0][...]
    else:
        x = jnp.where(pl.program_id(0) < x_split, x_refs[0][...], x_refs[1][...])
    if has_res:
        if y_weighted:
            y = yw_ref[:, 0:1] * y_refs[0][...].astype(F32)
            for k in range(1, n_y):
                y = y + yw_ref[:, k:k + 1] * y_refs[k][...].astype(F32)
        else:
            y = y_refs[0][...].astype(F32)
            for y_ref in y_refs[1:]:
                y = y + y_ref[...].astype(F32)
        x = x + gate_ref[...] * _rms(y, gpost_ref[...])
        xo_ref[...] = x
    if has_next:
        h = _rms(x, gpre_ref[...]) * (1.0 + scale_ref[...]) + shift_ref[...]
        h_ref[...] = h.astype(h_ref.dtype)
    if router:
        hi = h.astype(BF16)
        lo = (h - hi.astype(F32)).astype(BF16)
        logits = (jnp.dot(hi, rhi_ref[...], preferred_element_type=F32)
                  + jnp.dot(hi, rlo_ref[...], preferred_element_type=F32)
                  + jnp.dot(lo, rhi_ref[...], preferred_element_type=F32))
        lane = lax.broadcasted_iota(jnp.int32, logits.shape, 1)
        lg = jnp.where(lane < n_experts, logits, -jnp.inf)
        v1 = jnp.max(lg, axis=-1, keepdims=True)
        i1 = jnp.min(jnp.where(lg == v1, lane, LANES), axis=-1, keepdims=True)
        lg2 = jnp.where(lane == i1, -jnp.inf, lg)
        v2 = jnp.max(lg2, axis=-1, keepdims=True)
        i2 = jnp.min(jnp.where(lg2 == v2, lane, LANES), axis=-1, keepdims=True)
        e = jnp.exp(v2 - v1)
        g1 = 1.0 / (1.0 + e)
        g2 = e / (1.0 + e)
        rt_ref[...] = jnp.where(lane == 0, g1,
                                jnp.where(lane == 1, g2,
                                          jnp.where(lane == 2, i1.astype(F32),
                                                    jnp.where(lane == 3, i2.astype(F32), 0.0))))


def _row_call(xs, *, n_tiles, group_fn, modtab, res=None, nxt=None, router=None, h_dtype=BF16, name="row"):
    tm = ROW_TILE
    d = xs[0].shape[-1]
    has_res, has_next = res is not None, nxt is not None

    def modspec(k):
        return pl.BlockSpec((None, 1, d), lambda i: (group_fn(i) * 6 + k, 0, 0))

    vecspec = pl.BlockSpec((1, d), lambda i: (0, 0))
    x_split = 0
    if len(xs) == 1:
        ins, in_specs = [xs[0]], [pl.BlockSpec((tm, d), lambda i: (i, 0))]
    else:
        x_split = xs[0].shape[0] // tm
        ins = list(xs)
        in_specs = [pl.BlockSpec((tm, d), lambda i: (jnp.minimum(i, x_split - 1), 0)),
                    pl.BlockSpec((tm, d), lambda i: (jnp.maximum(i - x_split, 0), 0))]
    n_y = 0
    y_weights = None
    if has_res:
        y_blocks, gate_k, g_post, y_weights = res
        n_y = len(y_blocks)
        for arr, tile0 in y_blocks:
            ins.append(arr)
            in_specs.append(pl.BlockSpec((tm, d), lambda i, tile0=tile0: (i + tile0, 0)))
        if y_weights is not None:
            ins.append(y_weights)
            in_specs.append(pl.BlockSpec((tm, y_weights.shape[1]), lambda i: (i, 0)))
        ins += [modtab, g_post.reshape(1, d)]
        in_specs += [modspec(gate_k), vecspec]
    if has_next:
        g_pre, shift_k, scale_k, modtab_n = nxt
        ins += [g_pre.reshape(1, d), modtab_n, modtab_n]
        in_specs += [vecspec, modspec(shift_k), modspec(scale_k)]
    n_experts = 0
    if router is not None:
        n_experts = router.shape[-1]
        rp = jnp.zeros((d, LANES), F32).at[:, :n_experts].set(router)
        rhi = rp.astype(BF16)
        rlo = (rp - rhi.astype(F32)).astype(BF16)
        ins += [rhi, rlo]
        in_specs += [pl.BlockSpec((d, LANES), lambda i: (0, 0))] * 2
    out_shape, out_specs = [], []
    rowspec = pl.BlockSpec((tm, d), lambda i: (i, 0))
    if has_res:
        out_shape.append(jax.ShapeDtypeStruct((n_tiles * tm, d), F32))
        out_specs.append(rowspec)
    if has_next:
        out_shape.append(jax.ShapeDtypeStruct((n_tiles * tm, d), h_dtype))
        out_specs.append(rowspec)
    if router is not None:
        out_shape.append(jax.ShapeDtypeStruct((n_tiles * tm, LANES), F32))
        out_specs.append(pl.BlockSpec((tm, LANES), lambda i: (i, 0)))
    outs = pl.pallas_call(
        functools.partial(_row_body, n_x=len(xs), x_split=x_split, n_y=n_y, y_weighted=y_weights is not None,
                          has_next=has_next,
                          router=router is not None, n_experts=n_experts),
        grid=(n_tiles,), in_specs=in_specs, out_specs=out_specs, out_shape=out_shape,
        compiler_params=_cparams(1), name=name,
    )(*ins)
    outs = list(outs)
    x_new = outs.pop(0) if has_res else None
    h = outs.pop(0) if has_next else None
    rt = outs.pop(0) if router is not None else None
    return x_new, h, rt


def _mm_body(*refs, a_lo, a_hi, rms):
    if rms:
        a_ref, g_ref, w_ref, o_ref = refs
    else:
        a_ref, w_ref, o_ref = refs
    a = a_ref[:, a_lo:a_hi]
    if rms:
        a = _rms(a.astype(F32), g_ref[...]).astype(BF16)
    o_ref[...] = jnp.dot(a, w_ref[...], preferred_element_type=F32).astype(o_ref.dtype)


def _matmul(a, w, *, m_rows, tn, out_dtype=BF16, a_block_cols=None, a_lo=0, a_hi=None, rms_g=None, name="mm"):
    tm = MM_TILE_M
    k, n = w.shape
    if a_block_cols is None:
        a_block_cols = a.shape[1]
    if a_hi is None:
        a_hi = a_lo + k
    ins, in_specs = [a], [pl.BlockSpec((tm, a_block_cols), lambda j, i: (i, 0))]
    if rms_g is not None:
        ins.append(rms_g.reshape(1, k))
        in_specs.append(pl.BlockSpec((1, k), lambda j, i: (0, 0)))
    ins.append(w)
    in_specs.append(pl.BlockSpec((k, tn), lambda j, i: (0, j)))
    return pl.pallas_call(
        functools.partial(_mm_body, a_lo=a_lo, a_hi=a_hi, rms=rms_g is not None),
        grid=(n // tn, m_rows // tm), in_specs=in_specs,
        out_specs=pl.BlockSpec((tm, tn), lambda j, i: (i, j)),
        out_shape=jax.ShapeDtypeStruct((m_rows, n), out_dtype),
        compiler_params=_cparams(2), name=name,
    )(*ins)


def _mm_groups_body(*refs, parts_per_group, split_tile):
    n_a = sum(parts_per_group)
    a_refs = refs[:n_a]
    g_ref, w_ref, o_ref = refs[n_a:]
    parts, lo, pos = [], 0, 0
    for n_parts in parts_per_group:
        if n_parts == 1:
            a = a_refs[pos][...]
        else:
            a = jnp.where(pl.program_id(1) < split_tile, a_refs[pos][...], a_refs[pos + 1][...])
        pos += n_parts
        width = a.shape[1]
        parts.append(_rms(a.astype(F32), g_ref[:, lo:lo + width]).astype(BF16))
        lo += width
    a = jnp.concatenate(parts, axis=1)
    o_ref[...] = jnp.dot(a, w_ref[...], preferred_element_type=F32).astype(o_ref.dtype)


def _matmul_groups(groups, g, w, *, m_rows, tn, name):
    tm = MM_TILE_M
    k, n = w.shape
    split_tile = 0
    ins, in_specs = [], []
    for grp in groups:
        if len(grp) == 1:
            in_specs.append(pl.BlockSpec((tm, grp[0].shape[1]), lambda j, i: (i, 0)))
        else:
            split_tile = grp[0].shape[0] // tm
            st = split_tile
            in_specs.append(pl.BlockSpec((tm, grp[0].shape[1]), lambda j, i, st=st: (jnp.minimum(i, st - 1), 0)))
            in_specs.append(pl.BlockSpec((tm, grp[1].shape[1]), lambda j, i, st=st: (jnp.maximum(i - st, 0), 0)))
        ins += list(grp)
    in_specs += [pl.BlockSpec((1, k), lambda j, i: (0, 0)), pl.BlockSpec((k, tn), lambda j, i: (0, j))]
    return pl.pallas_call(
        functools.partial(_mm_groups_body, parts_per_group=tuple(len(grp) for grp in groups), split_tile=split_tile),
        grid=(n // tn, m_rows // tm), in_specs=in_specs,
        out_specs=pl.BlockSpec((tm, tn), lambda j, i: (i, j)),
        out_shape=jax.ShapeDtypeStruct((m_rows, n), BF16),
        compiler_params=_cparams(2), name=name,
    )(*ins, g.reshape(1, k), w)


def _mmk_body(a_ref, w_ref, o_ref, acc_ref, *, nk):
    kk = pl.program_id(2)

    @pl.when(kk == 0)
    def _():
        acc_ref[...] = jnp.zeros_like(acc_ref)

    acc_ref[...] += jnp.dot(a_ref[...], w_ref[...], preferred_element_type=F32)

    @pl.when(kk == nk - 1)
    def _():
        o_ref[...] = acc_ref[...].astype(o_ref.dtype)


def _matmul_ktiled(a, w, *, m_rows, tn, tk, out_dtype=BF16, name="mmk"):
    tm = MM_TILE_M
    k, n = w.shape
    nk = k // tk
    return pl.pallas_call(
        functools.partial(_mmk_body, nk=nk),
        grid=(n // tn, m_rows // tm, nk),
        in_specs=[pl.BlockSpec((tm, tk), lambda j, i, kk: (i, kk)),
                  pl.BlockSpec((tk, tn), lambda j, i, kk: (kk, j))],
        out_specs=pl.BlockSpec((tm, tn), lambda j, i, kk: (i, j)),
        out_shape=jax.ShapeDtypeStruct((m_rows, n), out_dtype),
        scratch_shapes=[pltpu.VMEM((tm, tn), F32)],
        compiler_params=_cparams(3), name=name,
    )(a, w)


def _glu_body(a_ref, w1_ref, w3_ref, o_ref):
    a = a_ref[...]
    h1 = jnp.dot(a, w1_ref[...], preferred_element_type=F32)
    h3 = jnp.dot(a, w3_ref[...], preferred_element_type=F32)
    o_ref[...] = (_silu(h1) * h3).astype(o_ref.dtype)


def _glu(a, w1, w3, *, m_rows, tf, name="glu"):
    tm = MM_TILE_M
    k, f = w1.shape
    return pl.pallas_call(
        _glu_body, grid=(f // tf, m_rows // tm),
        in_specs=[pl.BlockSpec((tm, k), lambda j, i: (i, 0)),
                  pl.BlockSpec((k, tf), lambda j, i: (0, j)),
                  pl.BlockSpec((k, tf), lambda j, i: (0, j))],
        out_specs=pl.BlockSpec((tm, tf), lambda j, i: (i, j)),
        out_shape=jax.ShapeDtypeStruct((m_rows, f), BF16),
        compiler_params=_cparams(2), name=name,
    )(a, w1, w3)


def _rope(x, c, s, half):
    lane = lax.broadcasted_iota(jnp.int32, x.shape, 1)
    first = (lane % (2 * half)) < half
    swapped = jnp.where(first, pltpu.roll(x, LANES - half, 1), pltpu.roll(x, half, 1))
    return x * c + swapped * s


def _prep_body(u_ref, qraw_ref, kvraw_ref, kpe_ref, cg_ref, sg_ref, cm_ref, sm_ref, gq_ref, gk_ref,
               qm_o, km_o, vm_o, qg_o, kg_o, vg_o, qn_o, kn_o, vn_o,
               *, hm, hg, hkv, hn, off_qg, off_kvg, off_n):
    cg, sg, cm, sm = cg_ref[...], sg_ref[...], cm_ref[...], sm_ref[...]
    scale_m = (MLA_NOPE + MLA_ROPE) ** -0.5 * LOG2E
    scale_g = HEAD_DIM ** -0.5 * LOG2E
    scale_n = HEAD_DIM ** -0.5
    d = HEAD_DIM
    tm = u_ref.shape[0]
    ones_rows = jnp.where(lax.broadcasted_iota(jnp.int32, (V_PAD_ROWS, tm), 0) == 0, 1.0, 0.0).astype(BF16)

    def t_bf16(v):
        return v.astype(F32).T.astype(BF16)

    kpe = _rope(kpe_ref[...].astype(F32), cm, sm, MLA_ROPE // 4)[:, :MLA_ROPE].astype(BF16)
    pe0 = hm * MLA_NOPE
    for h in range(hm):
        qm_o[h, 0:MLA_NOPE, :] = t_bf16(qraw_ref[:, h * MLA_NOPE:(h + 1) * MLA_NOPE].astype(F32) * scale_m)
        km_o[h, :, 0:MLA_NOPE] = kvraw_ref[:, h * 2 * d:h * 2 * d + MLA_NOPE]
        km_o[h, :, MLA_NOPE:MLA_NOPE + MLA_ROPE] = kpe
        vm_o[h, 0:MLA_V, :] = t_bf16(kvraw_ref[:, h * 2 * d + MLA_NOPE:(h + 1) * 2 * d])
        vm_o[h, MLA_V:MLA_V + V_PAD_ROWS, :] = ones_rows
    for j in range(hm // 2):
        pe = _rope(qraw_ref[:, pe0 + j * LANES:pe0 + (j + 1) * LANES].astype(F32), cm, sm, MLA_ROPE // 4)
        pe_t = t_bf16(pe * scale_m)
        qm_o[2 * j, MLA_NOPE:MLA_NOPE + MLA_ROPE, :] = pe_t[:MLA_ROPE]
        qm_o[2 * j + 1, MLA_NOPE:MLA_NOPE + MLA_ROPE, :] = pe_t[MLA_ROPE:]
    for h in range(hg):
        q = _rms(u_ref[:, off_qg + h * d:off_qg + (h + 1) * d].astype(F32), gq_ref[...])
        qg_o[h] = t_bf16(_rope(q, cg, sg, d // 4) * scale_g)
    for h in range(hkv):
        k = _rms(u_ref[:, off_kvg + h * d:off_kvg + (h + 1) * d].astype(F32), gk_ref[...])
        kg_o[h] = _rope(k, cg, sg, d // 4).astype(BF16)
        vg_o[h, 0:d, :] = t_bf16(u_ref[:, off_kvg + (hkv + h) * d:off_kvg + (hkv + h + 1) * d])
        vg_o[h, d:d + V_PAD_ROWS, :] = ones_rows
    for h in range(hn):
        qn_o[h] = (u_ref[:, off_n + h * d:off_n + (h + 1) * d].astype(F32) * scale_n).astype(BF16)
        kn_o[h] = u_ref[:, off_n + (hn + h) * d:off_n + (hn + h + 1) * d]
        vn_o[h] = u_ref[:, off_n + (2 * hn + h) * d:off_n + (2 * hn + h + 1) * d]


def _prep(u, qraw, kvraw, kpe, tables, g_q, g_k, *, dims, tk_m, tk_g):
    tm = ROW_TILE
    t_tot = u.shape[0]
    b, s, l = dims["B"], dims["S"], dims["L"]
    hm, hg, hkv, hn = dims["HM"], dims["HG"], dims["HKV"], dims["HN"]
    tk_len = s + l
    n_lat = b * s // tm
    spb, lpb = s // tm, l // tm

    def kmap(i):
        lat = i < n_lat
        j = i - n_lat
        bb = jnp.where(lat, i // spb, j // lpb)
        pos = jnp.where(lat, i % spb, spb + j % lpb)
        return bb, pos

    def kspec(h, dk):
        def im(i):
            bb, pos = kmap(i)
            return (bb, 0, pos, 0)
        return pl.BlockSpec((None, h, tm, dk), im)

    def vtspec(h, dv, tk):
        per = tk // tm

        def im(i):
            bb, pos = kmap(i)
            return (bb, 0, pos // per, 0, pos % per)
        return pl.BlockSpec((None, h, None, dv + V_PAD_ROWS, tm), im)

    def qtspec(h, dk):
        return pl.BlockSpec((h, None, dk, tm), lambda i: (0, i, 0, 0))

    def qspec(h, dk):
        return pl.BlockSpec((h, tm, dk), lambda i: (0, i, 0))

    def full(arr):
        return pl.BlockSpec((tm, arr.shape[1]), lambda i: (i, 0))

    dk_m = MLA_NOPE + MLA_ROPE
    d = HEAD_DIM
    out_shape = [
        jax.ShapeDtypeStruct((hm, t_tot // tm, dk_m, tm), BF16), jax.ShapeDtypeStruct((b, hm, tk_len, dk_m), BF16),
        jax.ShapeDtypeStruct((b, hm, tk_len // tk_m, MLA_V + V_PAD_ROWS, tk_m), BF16),
        jax.ShapeDtypeStruct((hg, t_tot // tm, d, tm), BF16), jax.ShapeDtypeStruct((b, hkv, tk_len, d), BF16),
        jax.ShapeDtypeStruct((b, hkv, tk_len // tk_g, d + V_PAD_ROWS, tk_g), BF16),
        jax.ShapeDtypeStruct((hn, t_tot, d), BF16), jax.ShapeDtypeStruct((b, hn, tk_len, d), BF16),
        jax.ShapeDtypeStruct((b, hn, tk_len, d), BF16),
    ]
    out_specs = [qtspec(hm, dk_m), kspec(hm, dk_m), vtspec(hm, MLA_V, tk_m),
                 qtspec(hg, d), kspec(hkv, d), vtspec(hkv, d, tk_g),
                 qspec(hn, d), kspec(hn, d), kspec(hn, d)]
    vec = pl.BlockSpec((1, d), lambda i: (0, 0))
    return pl.pallas_call(
        functools.partial(_prep_body, hm=hm, hg=hg, hkv=hkv, hn=hn,
                          off_qg=dims["OFF_QG"], off_kvg=dims["OFF_KVG"], off_n=dims["OFF_N"]),
        grid=(t_tot // tm,),
        in_specs=[full(u), full(qraw), full(kvraw), full(kpe)] + [full(t) for t in tables] + [vec, vec],
        out_specs=out_specs, out_shape=out_shape,
        compiler_params=_cparams(1), name="prep",
    )(u, qraw, kvraw, kpe, *tables, g_q.reshape(1, d), g_k.reshape(1, d))


M_INIT = -1e30


def _q_tile(qt_ref, g, tpi, t):
    parts = [qt_ref[gi, t * tpi + j] for gi in range(g) for j in range(tpi)]
    return parts[0] if len(parts) == 1 else jnp.concatenate(parts, axis=1)


def _attn_body(qt_ref, k_ref, vt_ref, o_ref, s0, s1, p0, p1, a0, a1, m_sc, acc_sc,
               *, g, tpi, tq, tk, nk, dv, nt, unroll):
    s_buf, p_buf, a_buf = (s0, s1), (p0, p1), (a0, a1)
    n_items = nt * nk

    def split(it):
        if isinstance(it, int):
            return it // nk, it % nk
        return lax.div(it, jnp.int32(nk)), lax.rem(it, jnp.int32(nk))

    def scores(it):
        t, c = split(it)
        off = c * tk if isinstance(c, int) else pl.multiple_of(c * tk, tk)
        return jnp.dot(k_ref[pl.ds(off, tk), :], _q_tile(qt_ref, g, tpi, t), preferred_element_type=F32)

    def softmax(it, slot):
        t, _ = split(it)
        s = s_buf[slot][...]
        m_prev = m_sc[t]
        m_new = jnp.maximum(m_prev, jnp.max(s, axis=0, keepdims=True))
        p_buf[slot][...] = jnp.exp2(s - m_new).astype(BF16)
        a_buf[slot][...] = jnp.exp2(m_prev - m_new)
        m_sc[t] = m_new

    def values(it, slot):
        t, c = split(it)
        acc_sc[t] = a_buf[slot][...] * acc_sc[t] + jnp.dot(vt_ref[c], p_buf[slot][...], preferred_element_type=F32)

    def stage(it, slot):
        s_buf[1 - slot][...] = scores(it + 1)
        softmax(it, slot)
        values(it - 1, 1 - slot)

    m_sc[...] = jnp.full_like(m_sc, M_INIT)
    acc_sc[...] = jnp.zeros_like(acc_sc)
    s_buf[0][...] = scores(0)
    softmax(0, 0)
    if n_items > 1:
        s_buf[1][...] = scores(1)
        n_steady = n_items - 2

        def pair(u, carry):
            it = 2 * u + 1
            stage(it, 1)
            stage(it + 1, 0)
            return carry

        lax.fori_loop(0, n_steady // 2, pair, 0, unroll=unroll)
        if n_steady % 2:
            stage(n_items - 2, 1)
        softmax(n_items - 1, (n_items - 1) % 2)
        values(n_items - 2, (n_items - 2) % 2)
    values(n_items - 1, (n_items - 1) % 2)

    def finish(t, carry):
        acc = acc_sc[t]
        o_t = acc[0:dv] / acc[dv:dv + 1]
        for gi in range(g):
            for j in range(tpi):
                row = pl.multiple_of((t * tpi + j) * tq, tq)
                col = (gi * tpi + j) * tq
                o_ref[pl.ds(row, tq), gi * dv:(gi + 1) * dv] = o_t[:, col:col + tq].T.astype(o_ref.dtype)
        return carry

    lax.fori_loop(0, nt, finish, 0)


def _attention(qt, k, vt, *, dims, g, tpi, nt, unroll, name):
    b, s_len, l_len = dims["B"], dims["S"], dims["L"]
    hk = k.shape[1]
    dk, tq = qt.shape[2:]
    nk, dvp, tk = vt.shape[2:]
    dv = dvp - V_PAD_ROWS
    steps_per_b = s_len // (nt * tpi * tq)
    n = g * tpi * tq
    return pl.pallas_call(
        functools.partial(_attn_body, g=g, tpi=tpi, tq=tq, tk=tk, nk=nk, dv=dv, nt=nt, unroll=unroll),
        grid=(b, hk, steps_per_b),
        in_specs=[pl.BlockSpec((g, nt * tpi, dk, tq), lambda bb, h, i: (h, bb * steps_per_b + i, 0, 0)),
                  pl.BlockSpec((None, None, s_len + l_len, dk), lambda bb, h, i: (bb, h, 0, 0)),
                  pl.BlockSpec((None, None, nk, dvp, tk), lambda bb, h, i: (bb, h, 0, 0, 0))],
        out_specs=pl.BlockSpec((nt * tpi * tq, g * dv), lambda bb, h, i: (bb * steps_per_b + i, h)),
        out_shape=jax.ShapeDtypeStruct((b * s_len, hk * g * dv), BF16),
        scratch_shapes=[pltpu.VMEM((tk, n), F32), pltpu.VMEM((tk, n), F32),
                        pltpu.VMEM((tk, n), BF16), pltpu.VMEM((tk, n), BF16),
                        pltpu.VMEM((1, n), F32), pltpu.VMEM((1, n), F32),
                        pltpu.VMEM((nt, 1, n), F32), pltpu.VMEM((nt, dvp, n), F32)],
        compiler_params=_cparams(3), name=name,
    )(qt, k, vt)


def _attn_ctx_body(qt_ref, k_ref, vt_ref, o_ref, *, g, tq, dv, ctx_rows):
    qt = _q_tile(qt_ref, g, 1, 0)
    tk = vt_ref.shape[-1]
    s = jnp.dot(k_ref[...], qt, preferred_element_type=F32)
    p = jnp.exp2(s - jnp.max(s, axis=0, keepdims=True)).astype(BF16)
    acc = jnp.dot(vt_ref[:, tk - ctx_rows:tk], p, preferred_element_type=F32)
    o_t = acc[0:dv] / acc[dv:dv + 1]
    for gi in range(g):
        o_ref[:, gi * dv:(gi + 1) * dv] = o_t[:, gi * tq:(gi + 1) * tq].T.astype(o_ref.dtype)


def _attention_ctx(qt, k, vt, *, dims, g, name):
    b, s_len, l_len = dims["B"], dims["S"], dims["L"]
    hk = k.shape[1]
    dk, tq = qt.shape[2:]
    nk, dvp, tk = vt.shape[2:]
    dv = dvp - V_PAD_ROWS
    cpb = l_len // tq
    lat_tiles = b * s_len // tq
    assert l_len % tq == 0 and l_len <= tk and s_len % l_len == 0
    return pl.pallas_call(
        functools.partial(_attn_ctx_body, g=g, tq=tq, dv=dv, ctx_rows=l_len),
        grid=(b, hk, cpb),
        in_specs=[pl.BlockSpec((g, 1, dk, tq), lambda bb, h, i: (h, lat_tiles + bb * cpb + i, 0, 0)),
                  pl.BlockSpec((None, None, l_len, dk), lambda bb, h, i: (bb, h, s_len // l_len, 0)),
                  pl.BlockSpec((None, None, None, dvp, tk), lambda bb, h, i: (bb, h, nk - 1, 0, 0))],
        out_specs=pl.BlockSpec((tq, g * dv), lambda bb, h, i: (bb * cpb + i, h)),
        out_shape=jax.ShapeDtypeStruct((b * l_len, hk * g * dv), BF16),
        compiler_params=_cparams(3), name=name,
    )(qt, k, vt)


def _na_body(q_ref, k_ref, v_ref, bias_ref, o_ref, *, rows_per_step, n_grid_rows, kh, s_len, l_len, lat_steps,
             with_ctx):
    w = GRID_W
    j = pl.program_id(2)
    kc = k_ref[s_len:s_len + l_len, :]
    vc = v_ref[s_len:s_len + l_len, :]
    nt = (((1,), (1,)), ((), ()))

    def latent_step():
        span = kh + rows_per_step
        r0 = j * rows_per_step
        u0 = jnp.clip(r0 - kh // 2, 0, n_grid_rows - span)
        koff = pl.multiple_of(u0 * w, w)
        kw = k_ref[pl.ds(koff, span * w), :]
        vw = v_ref[pl.ds(koff, span * w), :]
        q = q_ref[...]
        s_loc = lax.dot_general(q, kw, nt, preferred_element_type=F32) + bias_ref[(r0 - u0) // rows_per_step]
        s_ctx = lax.dot_general(q, kc, nt, preferred_element_type=F32)
        m = jnp.maximum(jnp.max(s_loc, axis=-1, keepdims=True), jnp.max(s_ctx, axis=-1, keepdims=True))
        p_loc = jnp.exp(s_loc - m)
        p_ctx = jnp.exp(s_ctx - m)
        denom = jnp.sum(p_loc, axis=-1, keepdims=True) + jnp.sum(p_ctx, axis=-1, keepdims=True)
        o = (jnp.dot(p_loc.astype(BF16), vw, preferred_element_type=F32)
             + jnp.dot(p_ctx.astype(BF16), vc, preferred_element_type=F32))
        o_ref[...] = (o / denom).astype(o_ref.dtype)

    def ctx_step():
        s = lax.dot_general(q_ref[...], kc, nt, preferred_element_type=F32)
        p = jnp.exp(s - jnp.max(s, axis=-1, keepdims=True))
        o = jnp.dot(p.astype(BF16), vc, preferred_element_type=F32)
        o_ref[...] = (o / jnp.sum(p, axis=-1, keepdims=True)).astype(o_ref.dtype)

    if with_ctx:
        pl.when(j < lat_steps)(latent_step)
        pl.when(j >= lat_steps)(ctx_step)
    else:
        latent_step()


def _na_bias(rpb, kh, n_grid_rows, rps):
    span = kh + rps
    col = np.arange(GRID_W)
    start_c = np.clip(col - NA_KW // 2, 0, GRID_W - NA_KW)
    col_mask = (col[None, :] >= start_c[:, None]) & (col[None, :] < start_c[:, None] + NA_KW)
    dc_idx = np.clip(col[None, :] - col[:, None] + NA_KW - 1, 0, 2 * NA_KW - 2)
    n_var = kh // rps + 1
    dr_idx = np.full((n_var, rps, span), -1)
    seen = set()
    for r0 in range(0, n_grid_rows, rps):
        u0 = int(np.clip(r0 - kh // 2, 0, n_grid_rows - span))
        var = (r0 - u0) // rps
        assert (r0 - u0) % rps == 0 and 0 <= var < n_var
        table = np.full((rps, span), -1)
        for qr in range(rps):
            r = r0 + qr
            start = int(np.clip(r - kh // 2, 0, n_grid_rows - kh))
            for kr in range(span):
                if start <= u0 + kr < start + kh:
                    table[qr, kr] = u0 + kr - r + NA_KH - 1
        assert var not in seen or np.array_equal(dr_idx[var], table)
        seen.add(var)
        dr_idx[var] = table
    row_mask = dr_idx >= 0
    oh_r = (dr_idx[..., None] == np.arange(2 * NA_KH - 1)).astype(np.float32)
    oh_c = (dc_idx[:, :, None] == np.arange(2 * NA_KW - 1)[None, None, :]).astype(np.float32)
    bias = jnp.einsum("vrja,hab,qkb->hvrqjk", oh_r, rpb.astype(F32), oh_c, precision=lax.Precision.HIGHEST)
    mask = row_mask[None, :, :, None, :, None] & col_mask[None, None, None, :, None, :]
    bias = jnp.where(mask, bias, NEG_INF)
    return bias.reshape(rpb.shape[0], n_var, rps * GRID_W, span * GRID_W)


def _na(q, k, v, rpb, *, dims, with_ctx):
    b, s, l, hn = dims["B"], dims["S"], dims["L"], dims["HN"]
    r_tot = s // GRID_W
    kh = min(NA_KH, r_tot)
    tq = ROW_TILE
    rps = tq // GRID_W
    assert r_tot >= kh + rps and kh % rps == 0 and (kh // 2) % rps == 0
    bias = _na_bias(rpb, kh, r_tot, rps)
    d = HEAD_DIM
    spb = r_tot // rps
    cpb = l // tq if with_ctx else 0
    lat_tiles = b * spb
    rows = b * (s + (l if with_ctx else 0))

    def qtile(bb, j):
        return jnp.where(j < spb, bb * spb + j, lat_tiles + bb * cpb + (j - spb))

    return pl.pallas_call(
        functools.partial(_na_body, rows_per_step=rps, n_grid_rows=r_tot, kh=kh, s_len=s, l_len=l, lat_steps=spb,
                          with_ctx=with_ctx),
        grid=(b, hn, spb + cpb),
        in_specs=[pl.BlockSpec((None, tq, d), lambda bb, h, j: (h, qtile(bb, j), 0)),
                  pl.BlockSpec((None, None, s + l, d), lambda bb, h, j: (bb, h, 0, 0)),
                  pl.BlockSpec((None, None, s + l, d), lambda bb, h, j: (bb, h, 0, 0)),
                  pl.BlockSpec((None,) + bias.shape[1:], lambda bb, h, j: (h, 0, 0, 0))],
        out_specs=pl.BlockSpec((tq, d), lambda bb, h, j: (qtile(bb, j), h)),
        out_shape=jax.ShapeDtypeStruct((rows, hn * d), BF16),
        compiler_params=_cparams(3), name="na",
    )(q, k, v, bias)


ROUTE_DST_BITS = 16


def _moe_body(te_ref, tv_ref, route_ref, h_ref, w1_ref, w3_ref, w2_ref, out_ref,
              xg, xb, acc_ref, ys, gsem, ssem, *, nf, tm):
    i, f = pl.program_id(0), pl.program_id(1)
    n_tiles = pl.num_programs(0)
    valid = tv_ref[i] > 0
    chunk = tm // nf
    dump1 = out_ref.shape[0] - tm

    def src_row(p):
        return lax.shift_right_logical(route_ref[p], ROUTE_DST_BITS)

    def dst_row(p):
        return route_ref[p] & ((1 << ROUTE_DST_BITS) - 1)

    def gather_copy(r, row):
        return pltpu.make_async_copy(h_ref.at[pl.ds(row, 1)], xg.at[pl.ds(r, 1)], gsem)

    def scatter_copy(slot, r, row):
        return pltpu.make_async_copy(ys.at[slot, pl.ds(r, 1)], out_ref.at[pl.ds(row, 1)], ssem.at[slot])

    def wait_gather():
        pltpu.make_async_copy(h_ref.at[pl.ds(0, tm)], xg, gsem).wait()

    def wait_scatter(slot):
        pltpu.make_async_copy(ys.at[slot], out_ref.at[pl.ds(0, tm)], ssem.at[slot]).wait()

    def issue_chunks():
        g_tile = jnp.minimum(i + 1, n_tiles - 1)
        s_tile = jnp.maximum(i - 1, 0)
        s_slot = (i + 1) % 2
        for r in range(chunk):
            rr = f * chunk + r
            gather_copy(rr, src_row(g_tile * tm + rr)).start()
            row = jnp.where(i > 0, dst_row(s_tile * tm + rr), dump1 + rr)
            scatter_copy(s_slot, rr, row).start()

    @pl.when(jnp.logical_and(i == 0, f == 0))
    def _():
        ys[...] = jnp.zeros_like(ys)

        def go(r, c):
            gather_copy(r, src_row(r)).start()
            return c
        lax.fori_loop(0, tm, go, 0)

    @pl.when(f == 0)
    def _():
        wait_gather()
        xb[...] = xg[...].astype(BF16)
        acc_ref[...] = jnp.zeros_like(acc_ref)

    @pl.when(valid)
    def _():
        issue_chunks()
        x = xb[...]
        h1 = jnp.dot(x, w1_ref[...], preferred_element_type=F32)
        h3 = jnp.dot(x, w3_ref[...], preferred_element_type=F32)
        hm = (_silu(h1) * h3).astype(BF16)
        acc_ref[...] += jnp.dot(hm, w2_ref[...], preferred_element_type=F32)

    @pl.when(jnp.logical_not(valid))
    def _():
        issue_chunks()

    @pl.when(f == nf - 1)
    def _():
        @pl.when(i > 0)
        def _():
            wait_scatter(i % 2)

        ys[i % 2] = acc_ref[...]

        @pl.when(i == n_tiles - 1)
        def _():
            def go(r, c):
                scatter_copy(i % 2, r, dst_row(i * tm + r)).start()
                return c
            lax.fori_loop(0, tm, go, 0)
            wait_scatter(i % 2)
            wait_scatter((i + 1) % 2)
            wait_gather()


def _moe_ffn(h, w1, w3, w2, plan, *, n_out_rows, tf):
    route, tile_expert, tile_valid = plan
    tm = MOE_TILE_M
    p_tot = route.shape[0]
    d = h.shape[1]
    f_dim = w1.shape[-1]
    nf = f_dim // tf
    assert tm % nf == 0 and n_out_rows + 2 * tm <= 1 << ROUTE_DST_BITS and h.shape[0] < 1 << (31 - ROUTE_DST_BITS)

    def fidx(i, f, tv):
        return jnp.where(tv[i] > 0, f, nf - 1)

    return pl.pallas_call(
        functools.partial(_moe_body, nf=nf, tm=tm),
        grid_spec=pltpu.PrefetchScalarGridSpec(
            num_scalar_prefetch=3, grid=(p_tot // tm, nf),
            in_specs=[pl.BlockSpec(memory_space=pl.ANY),
                      pl.BlockSpec((None, d, tf), lambda i, f, te, tv, rt: (te[i], 0, fidx(i, f, tv))),
                      pl.BlockSpec((None, d, tf), lambda i, f, te, tv, rt: (te[i], 0, fidx(i, f, tv))),
                      pl.BlockSpec((None, tf, d), lambda i, f, te, tv, rt: (te[i], fidx(i, f, tv), 0))],
            out_specs=pl.BlockSpec(memory_space=pl.ANY),
            scratch_shapes=[pltpu.VMEM((tm, d), F32), pltpu.VMEM((tm, d), BF16), pltpu.VMEM((tm, d), F32),
                            pltpu.VMEM((2, tm, d), F32), pltpu.SemaphoreType.DMA(()),
                            pltpu.SemaphoreType.DMA((2,))]),
        out_shape=jax.ShapeDtypeStruct((n_out_rows + 2 * tm, d), F32),
        compiler_params=_cparams(2), name="moe_ffn",
    )(tile_expert, tile_valid, route, h, w1, w3, w2)


def _moe_plan(e_idx, n_experts, tm):
    flat_e = e_idx.reshape(-1)
    n_slots = flat_e.shape[0]
    onehot = (flat_e[:, None] == jnp.arange(n_experts)[None, :]).astype(jnp.int32)
    csum = jnp.cumsum(onehot, axis=0)
    rank = jnp.sum(csum * onehot, axis=1) - 1
    counts = csum[-1]
    tiles_per = (counts + tm - 1) // tm
    cum_tiles = jnp.cumsum(tiles_per)
    row_start = (cum_tiles - tiles_per) * tm
    dest = jnp.sum(row_start[None, :] * onehot, axis=1) + rank
    n_tiles = n_slots // tm + n_experts
    n_tiles += n_tiles % 2 == 0
    p_tot = n_tiles * tm
    te = jnp.sum((jnp.arange(n_tiles)[:, None] >= cum_tiles[None, :]).astype(jnp.int32), axis=1)
    tile_valid = (te < n_experts).astype(jnp.int32)
    tile_expert = jnp.minimum(te, n_experts - 1)
    slot = jnp.arange(n_slots, dtype=jnp.int32)
    n_tok = n_slots // TOP_K
    pos = jnp.arange(p_tot, dtype=jnp.int32)
    dump = n_slots + ((pos // tm) % 2) * tm + pos % tm
    word = (slot // TOP_K) * (1 << ROUTE_DST_BITS) + (slot % TOP_K) * n_tok + slot // TOP_K
    route = dump.at[dest].set(word)
    return route, tile_expert, tile_valid


def _rope_tables(n_batch, s_len, l_len):
    t = np.arange(s_len)
    rows, cols = t // GRID_W, t % GRID_W

    def table(width):
        half = width // 2
        quarter = half // 2
        lane = np.arange(width)
        pos = np.where((lane // half)[None, :] == 0, rows[:, None], cols[:, None]).astype(np.float64)
        fi = (lane % half) % quarter
        inv = ROPE_THETA ** (-(2.0 * fi) / half)
        ang = (pos.astype(np.float32) * inv.astype(np.float32)[None, :]).astype(np.float32)
        sign = np.where((lane % half) < quarter, -1.0, 1.0)
        return np.cos(ang).astype(np.float32), (np.sin(ang) * sign[None, :]).astype(np.float32)

    cg, sg = table(HEAD_DIM)
    cm, sm = table(MLA_ROPE)
    cm, sm = np.tile(cm, (1, 2)), np.tile(sm, (1, 2))

    def flat(lat, fill):
        return np.concatenate([np.tile(lat, (n_batch, 1)),
                               np.full((n_batch * l_len, LANES), fill, np.float32)], axis=0)

    return tuple(jnp.asarray(a) for a in (flat(cg, 1.0), flat(sg, 0.0), flat(cm, 1.0), flat(sm, 0.0)))


def _round_up(a, m):
    return (a + m - 1) // m * m


def _pick_tile(n, pref):
    for t in pref:
        if n % t == 0:
            return t
    return n


def kernel(x, c, ctx, c_ctx, w_ada, b_ada, g_pre_mix, g_post_mix, g_pre_ffn, g_post_ffn, w_in, g_q_a, g_kv_a, w_q_up, w_kv_up, g_q_gqa, g_k_gqa, na_rpb, g_grp, w_out, ffn_w1, ffn_w3, ffn_w2, router, moe_w1, moe_w3, moe_w2):
    n_batch, s_len, d = x.shape
    l_len = ctx.shape[1]
    depth = w_ada.shape[0]
    q_lora, kv_lora = g_q_a.shape[-1], g_kv_a.shape[-1]
    dk_m = MLA_NOPE + MLA_ROPE
    hm = w_q_up.shape[-1] // dk_m
    hg = 3 * d // (8 * HEAD_DIM)
    hkv = hg // 3
    hn = d // (4 * HEAD_DIM)
    g_gqa = hg // hkv
    n_lat, n_ctx = n_batch * s_len, n_batch * l_len
    t_tot = n_lat + n_ctx
    assert s_len % MM_TILE_M == 0 and n_ctx % MM_TILE_M == 0 and l_len % ROW_TILE == 0
    assert hm % 2 == 0 and n_batch + 1 <= SUBLANES

    off_qg = q_lora + kv_lora
    off_kvg = off_qg + hg * HEAD_DIM
    off_n = off_kvg + 2 * hkv * HEAD_DIM
    n_main = off_n + 3 * hn * HEAD_DIM
    dims = dict(B=n_batch, S=s_len, L=l_len, HM=hm, HG=hg, HKV=hkv, HN=hn,
                OFF_QG=off_qg, OFF_KVG=off_kvg, OFF_N=off_n)

    tables = _rope_tables(n_batch, s_len, l_len)
    tk_all = s_len + l_len
    tk_m = _pick_tile(tk_all, (768, 512, 256))
    tk_g = tk_m

    mods = _ada(jnp.concatenate([c_ctx[None, :], c], axis=0), w_ada, b_ada)

    lat_tiles, all_tiles = n_lat // ROW_TILE, t_tot // ROW_TILE
    tiles_per_b = s_len // ROW_TILE

    def group_of(i):
        return jnp.where(i < lat_tiles, 1 + i // tiles_per_b, 0)

    def modtab_of(i):
        return mods[i, :n_batch + 1].reshape((n_batch + 1) * 6, 1, d)

    xs = (x.reshape(n_lat, d), ctx.reshape(n_ctx, d))
    modtab = modtab_of(0)
    _, h, _ = _row_call(xs, n_tiles=all_tiles, group_fn=group_of, modtab=modtab,
                        nxt=(g_pre_mix[0], 0, 1, modtab), name="row_in")

    for i in range(depth):
        last = i == depth - 1
        modtab = modtab_of(i)
        m_out = n_lat if last else t_tot
        out_tiles = m_out // ROW_TILE

        wi = w_in[i]
        c_kpe = q_lora + kv_lora
        w_main = jnp.concatenate([wi[:, :c_kpe], wi[:, c_kpe + MLA_ROPE:]], axis=1).astype(BF16)
        w_kpe = jnp.zeros((d, LANES), BF16).at[:, :MLA_ROPE].set(wi[:, c_kpe:c_kpe + MLA_ROPE].astype(BF16))
        u = _matmul(h, w_main, m_rows=t_tot, tn=_pick_tile(n_main, (1024, 512, 256, 128)), name="mm_in")
        kpe = _matmul(h, w_kpe, m_rows=t_tot, tn=LANES, name="mm_kpe")
        wq = w_q_up[i].reshape(q_lora, hm, dk_m)
        wq = jnp.concatenate([wq[:, :, :MLA_NOPE].reshape(q_lora, hm * MLA_NOPE),
                              wq[:, :, MLA_NOPE:].reshape(q_lora, hm * MLA_ROPE)], axis=1).astype(BF16)
        a_cols = _round_up(off_qg, LANES)
        qraw = _matmul(u, wq, m_rows=t_tot, tn=_pick_tile(hm * dk_m, (768, 384, 128)), a_block_cols=a_cols,
                       a_lo=0, a_hi=q_lora, rms_g=g_q_a[i], name="mm_qup")
        kvraw = _matmul(u, w_kv_up[i].astype(BF16), m_rows=t_tot,
                        tn=_pick_tile(hm * (MLA_NOPE + MLA_V), (1024, 768, 512, 256)), a_block_cols=a_cols,
                        a_lo=q_lora, a_hi=q_lora + kv_lora, rms_g=g_kv_a[i], name="mm_kvup")
        qm, km, vm, qg, kg, vg, qn, kn, vn = _prep(u, qraw, kvraw, kpe, tables, g_q_gqa[i], g_k_gqa[i],
                                                   dims=dims, tk_m=tk_m, tk_g=tk_g)

        q_tiles_per_b = s_len // ROW_TILE
        o_m = (_attention(qm, km, vm, dims=dims, g=1, tpi=MLA_TILES_PER_ITEM,
                          nt=q_tiles_per_b // MLA_TILES_PER_ITEM, unroll=2, name="attn_mla"),)
        o_g = (_attention(qg, kg, vg, dims=dims, g=g_gqa, tpi=1, nt=_pick_tile(q_tiles_per_b, (16, 8, 4, 2)),
                          unroll=1, name="attn_gqa"),)
        if not last:
            o_m += (_attention_ctx(qm, km, vm, dims=dims, g=1, name="attn_mla_ctx"),)
            o_g += (_attention_ctx(qg, kg, vg, dims=dims, g=g_gqa, name="attn_gqa_ctx"),)
        o_n = (_na(qn, kn, vn, na_rpb[i], dims=dims, with_ctx=not last),)

        y = _matmul_groups((o_m, o_g, o_n), g_grp[i], w_out[i].astype(BF16), m_rows=m_out,
                           tn=_pick_tile(d, (1024, 512, 256)), name="mm_out")

        j = i // 2
        moe = i % 2 == 1
        res = ([(y, 0)], 2, g_post_mix[i], None)
        nxt = (g_pre_ffn[i], 3, 4, modtab)
        if moe:
            x_all, hf, rt = _row_call(xs, n_tiles=out_tiles, group_fn=group_of, modtab=modtab, res=res, nxt=nxt,
                                      router=router[j], h_dtype=F32, name="row_mix")
            n_experts = router.shape[-1]
            e_idx = rt[:, TOP_K:2 * TOP_K].astype(jnp.int32)
            plan = _moe_plan(e_idx, n_experts, MOE_TILE_M)
            f_weights = rt
            f_out = _moe_ffn(hf, moe_w1[j].astype(BF16), moe_w3[j].astype(BF16), moe_w2[j].astype(BF16), plan,
                             n_out_rows=TOP_K * m_out, tf=_pick_tile(moe_w1.shape[-1], (256, 128)))
            f_blocks = [(f_out, k * out_tiles) for k in range(TOP_K)]
        else:
            x_all, h, _ = _row_call(xs, n_tiles=out_tiles, group_fn=group_of, modtab=modtab, res=res, nxt=nxt,
                                    name="row_mix")
            f_dim = ffn_w1.shape[-1]
            f_pad = _round_up(f_dim, 1024)
            pad = ((0, 0), (0, f_pad - f_dim))
            w1 = jnp.pad(ffn_w1[j].astype(BF16), pad)
            w3 = jnp.pad(ffn_w3[j].astype(BF16), pad)
            w2 = jnp.pad(ffn_w2[j].astype(BF16), (pad[1], pad[0]))
            gact = _glu(h, w1, w3, m_rows=m_out, tf=1024, name="ffn_glu")
            f_out = _matmul(gact, w2, m_rows=m_out, tn=_pick_tile(d, (512, 256)), name="ffn_down")
            f_blocks = [(f_out, 0)]
            f_weights = None
        xs = (x_all,)

        res = (f_blocks, 5, g_post_ffn[i], f_weights)
        if last:
            x_all, _, _ = _row_call(xs, n_tiles=out_tiles, group_fn=group_of, modtab=modtab, res=res, name="row_ffn")
        else:
            x_all, h, _ = _row_call(xs, n_tiles=out_tiles, group_fn=group_of, modtab=modtab, res=res,
                                    nxt=(g_pre_mix[i + 1], 0, 1, modtab_of(i + 1)), name="row_ffn")
        xs = (x_all,)
    return xs[0][:n_lat].reshape(n_batch, s_len, d)
```

```python
import functools
import math

import jax
import jax.numpy as jnp
import numpy as np
from jax import lax
from jax.experimental import pallas as pl
from jax.experimental.pallas import tpu as pltpu

F32 = jnp.float32
BF16 = jnp.bfloat16

GRID_W = 64
HEAD_DIM = 128
ROPE_THETA = 10000.0
EPS = 1e-6
NEG_INF = -1e30
MLA_NOPE = 128
MLA_ROPE = 64
MLA_V = 128
NA_KH = 8
NA_KW = 16
TOP_K = 2
LOG2E = math.log2(math.e)

LANES = 128
SUBLANES = 8
BF16_SUBLANES = 16
VMEM_LIMIT_BYTES = 56 * 1024 * 1024

ROW_TILE = 256
MM_TILE_M = 512
MOE_TILE_M = 512
V_PAD_ROWS = BF16_SUBLANES
MLA_TILES_PER_ITEM = 1


def _cparams(n_axes):
    return pltpu.CompilerParams(dimension_semantics=("arbitrary",) * n_axes,
                                vmem_limit_bytes=VMEM_LIMIT_BYTES)


def _rms(v, g):
    ms = jnp.mean(v * v, axis=-1, keepdims=True)
    return v * lax.rsqrt(ms + EPS) * g


def _silu(v):
    return v * jax.nn.sigmoid(v)


ADA_K_CHUNK = 64


def _ada_body(ct_ref, w_ref, b_ref, o_ref, sb_ref, *, n_rows):
    d, tn = w_ref.shape

    @pl.when(jnp.logical_and(pl.program_id(0) == 0, pl.program_id(1) == 0))
    def _():
        s = _silu(ct_ref[...])
        for r in range(n_rows):
            sb_ref[r] = jnp.broadcast_to(s[:, r:r + 1], (d, LANES))

    def step(i, accs):
        off = pl.multiple_of(i * ADA_K_CHUNK, ADA_K_CHUNK)
        w = w_ref[pl.ds(off, ADA_K_CHUNK), :]
        out = []
        for r in range(n_rows):
            sb = sb_ref[r, pl.ds(off, ADA_K_CHUNK), :]
            prod = w * jnp.concatenate([sb] * (tn // LANES), axis=1)
            part = prod[0:SUBLANES]
            for j in range(1, ADA_K_CHUNK // SUBLANES):
                part = part + prod[j * SUBLANES:(j + 1) * SUBLANES]
            out.append(accs[r] + part)
        return tuple(out)

    accs = lax.fori_loop(0, d // ADA_K_CHUNK, step,
                         tuple(jnp.zeros((SUBLANES, tn), F32) for _ in range(n_rows)), unroll=2)
    rows = [jnp.sum(a, axis=0, keepdims=True) for a in accs]
    rows.append(jnp.zeros((SUBLANES - n_rows, tn), F32))
    o_ref[...] = jnp.concatenate(rows, axis=0) + b_ref[...]


def _ada(cond, w_ada, b_ada):
    n_rows, d = cond.shape
    depth, _, n = w_ada.shape
    tn = 512
    ct = jnp.zeros((d, SUBLANES), F32).at[:, :n_rows].set(cond.T)
    return pl.pallas_call(
        functools.partial(_ada_body, n_rows=n_rows),
        grid=(depth, n // tn),
        in_specs=[pl.BlockSpec((d, SUBLANES), lambda l, j: (0, 0)),
                  pl.BlockSpec((None, d, tn), lambda l, j: (l, 0, j)),
                  pl.BlockSpec((None, 1, tn), lambda l, j: (l, 0, j))],
        out_specs=pl.BlockSpec((None, SUBLANES, tn), lambda l, j: (l, 0, j)),
        out_shape=jax.ShapeDtypeStruct((depth, SUBLANES, n), F32),
        scratch_shapes=[pltpu.VMEM((n_rows, d, LANES), F32)],
        compiler_params=_cparams(2),
        name="ada",
    )(ct, w_ada, b_ada.reshape(depth, 1, n))


def _row_body(*refs, n_x, x_split, n_y, y_weighted, has_next, router, n_experts):
    it = iter(refs)
    x_refs = [next(it) for _ in range(n_x)]
    has_res = n_y > 0
    if has_res:
        y_refs = [next(it) for _ in range(n_y)]
        if y_weighted:
            yw_ref = next(it)
        gate_ref, gpost_ref = next(it), next(it)
    if has_next:
        gpre_ref, shift_ref, scale_ref = next(it), next(it), next(it)
    if router:
        rhi_ref, rlo_ref = next(it), next(it)
    if has_res:
        xo_ref = next(it)
    if has_next:
        h_ref = next(it)
    if router:
        rt_ref = next(it)

    if n_x == 1:
        x = x_refs[0][...]
    else:
        x = jnp.where(pl.program_id(0) < x_split, x_refs[0][...], x_refs[1][...])
    if has_res:
        if y_weighted:
            y = yw_ref[:, 0:1] * y_refs[0][...].astype(F32)
            for k in range(1, n_y):
                y = y + yw_ref[:, k:k + 1] * y_refs[k][...].astype(F32)
        else:
            y = y_refs[0][...].astype(F32)
            for y_ref in y_refs[1:]:
                y = y + y_ref[...].astype(F32)
        x = x + gate_ref[...] * _rms(y, gpost_ref[...])
        xo_ref[...] = x
    if has_next:
        h = _rms(x, gpre_ref[...]) * (1.0 + scale_ref[...]) + shift_ref[...]
        h_ref[...] = h.astype(h_ref.dtype)
    if router:
        hi = h.astype(BF16)
        lo = (h - hi.astype(F32)).astype(BF16)
        logits = (jnp.dot(hi, rhi_ref[...], preferred_element_type=F32)
                  + jnp.dot(hi, rlo_ref[...], preferred_element_type=F32)
                  + jnp.dot(lo, rhi_ref[...], preferred_element_type=F32))
        lane = lax.broadcasted_iota(jnp.int32, logits.shape, 1)
        lg = jnp.where(lane < n_experts, logits, -jnp.inf)
        v1 = jnp.max(lg, axis=-1, keepdims=True)
        i1 = jnp.min(jnp.where(lg == v1, lane, LANES), axis=-1, keepdims=True)
        lg2 = jnp.where(lane == i1, -jnp.inf, lg)
        v2 = jnp.max(lg2, axis=-1, keepdims=True)
        i2 = jnp.min(jnp.where(lg2 == v2, lane, LANES), axis=-1, keepdims=True)
        e = jnp.exp(v2 - v1)
        g1 = 1.0 / (1.0 + e)
        g2 = e / (1.0 + e)
        rt_ref[...] = jnp.where(lane == 0, g1,
                                jnp.where(lane == 1, g2,
                                          jnp.where(lane == 2, i1.astype(F32),
                                                    jnp.where(lane == 3, i2.astype(F32), 0.0))))


def _row_call(xs, *, n_tiles, group_fn, modtab, res=None, nxt=None, router=None, h_dtype=BF16, name="row"):
    tm = ROW_TILE
    d = xs[0].shape[-1]
    has_res, has_next = res is not None, nxt is not None

    def modspec(k):
        return pl.BlockSpec((None, 1, d), lambda i: (group_fn(i) * 6 + k, 0, 0))

    vecspec = pl.BlockSpec((1, d), lambda i: (0, 0))
    x_split = 0
    if len(xs) == 1:
        ins, in_specs = [xs[0]], [pl.BlockSpec((tm, d), lambda i: (i, 0))]
    else:
        x_split = xs[0].shape[0] // tm
        ins = list(xs)
        in_specs = [pl.BlockSpec((tm, d), lambda i: (jnp.minimum(i, x_split - 1), 0)),
                    pl.BlockSpec((tm, d), lambda i: (jnp.maximum(i - x_split, 0), 0))]
    n_y = 0
    y_weights = None
    if has_res:
        y_blocks, gate_k, g_post, y_weights = res
        n_y = len(y_blocks)
        for arr, tile0 in y_blocks:
            ins.append(arr)
            in_specs.append(pl.BlockSpec((tm, d), lambda i, tile0=tile0: (i + tile0, 0)))
        if y_weights is not None:
            ins.append(y_weights)
            in_specs.append(pl.BlockSpec((tm, y_weights.shape[1]), lambda i: (i, 0)))
        ins += [modtab, g_post.reshape(1, d)]
        in_specs += [modspec(gate_k), vecspec]
    if has_next:
        g_pre, shift_k, scale_k, modtab_n = nxt
        ins += [g_pre.reshape(1, d), modtab_n, modtab_n]
        in_specs += [vecspec, modspec(shift_k), modspec(scale_k)]
    n_experts = 0
    if router is not None:
        n_experts = router.shape[-1]
        rp = jnp.zeros((d, LANES), F32).at[:, :n_experts].set(router)
        rhi = rp.astype(BF16)
        rlo = (rp - rhi.astype(F32)).astype(BF16)
        ins += [rhi, rlo]
        in_specs += [pl.BlockSpec((d, LANES), lambda i: (0, 0))] * 2
    out_shape, out_specs = [], []
    rowspec = pl.BlockSpec((tm, d), lambda i: (i, 0))
    if has_res:
        out_shape.append(jax.ShapeDtypeStruct((n_tiles * tm, d), F32))
        out_specs.append(rowspec)
    if has_next:
        out_shape.append(jax.ShapeDtypeStruct((n_tiles * tm, d), h_dtype))
        out_specs.append(rowspec)
    if router is not None:
        out_shape.append(jax.ShapeDtypeStruct((n_tiles * tm, LANES), F32))
        out_specs.append(pl.BlockSpec((tm, LANES), lambda i: (i, 0)))
    outs = pl.pallas_call(
        functools.partial(_row_body, n_x=len(xs), x_split=x_split, n_y=n_y, y_weighted=y_weights is not None,
                          has_next=has_next,
                          router=router is not None, n_experts=n_experts),
        grid=(n_tiles,), in_specs=in_specs, out_specs=out_specs, out_shape=out_shape,
        compiler_params=_cparams(1), name=name,
    )(*ins)
    outs = list(outs)
    x_new = outs.pop(0) if has_res else None
    h = outs.pop(0) if has_next else None
    rt = outs.pop(0) if router is not None else None
    return x_new, h, rt


def _mm_body(*refs, a_lo, a_hi, rms):
    if rms:
        a_ref, g_ref, w_ref, o_ref = refs
    else:
        a_ref, w_ref, o_ref = refs
    a = a_ref[:, a_lo:a_hi]
    if rms:
        a = _rms(a.astype(F32), g_ref[...]).astype(BF16)
    o_ref[...] = jnp.dot(a, w_ref[...], preferred_element_type=F32).astype(o_ref.dtype)


def _matmul(a, w, *, m_rows, tn, out_dtype=BF16, a_block_cols=None, a_lo=0, a_hi=None, rms_g=None, name="mm"):
    tm = MM_TILE_M
    k, n = w.shape
    if a_block_cols is None:
        a_block_cols = a.shape[1]
    if a_hi is None:
        a_hi = a_lo + k
    ins, in_specs = [a], [pl.BlockSpec((tm, a_block_cols), lambda j, i: (i, 0))]
    if rms_g is not None:
        ins.append(rms_g.reshape(1, k))
        in_specs.append(pl.BlockSpec((1, k), lambda j, i: (0, 0)))
    ins.append(w)
    in_specs.append(pl.BlockSpec((k, tn), lambda j, i: (0, j)))
    return pl.pallas_call(
        functools.partial(_mm_body, a_lo=a_lo, a_hi=a_hi, rms=rms_g is not None),
        grid=(n // tn, m_rows // tm), in_specs=in_specs,
        out_specs=pl.BlockSpec((tm, tn), lambda j, i: (i, j)),
        out_shape=jax.ShapeDtypeStruct((m_rows, n), out_dtype),
        compiler_params=_cparams(2), name=name,
    )(*ins)


def _mm_groups_body(*refs, parts_per_group, split_tile):
    n_a = sum(parts_per_group)
    a_refs = refs[:n_a]
    g_ref, w_ref, o_ref = refs[n_a:]
    parts, lo, pos = [], 0, 0
    for n_parts in parts_per_group:
        if n_parts == 1:
            a = a_refs[pos][...]
        else:
            a = jnp.where(pl.program_id(1) < split_tile, a_refs[pos][...], a_refs[pos + 1][...])
        pos += n_parts
        width = a.shape[1]
        parts.append(_rms(a.astype(F32), g_ref[:, lo:lo + width]).astype(BF16))
        lo += width
    a = jnp.concatenate(parts, axis=1)
    o_ref[...] = jnp.dot(a, w_ref[...], preferred_element_type=F32).astype(o_ref.dtype)


def _matmul_groups(groups, g, w, *, m_rows, tn, name):
    tm = MM_TILE_M
    k, n = w.shape
    split_tile = 0
    ins, in_specs = [], []
    for grp in groups:
        if len(grp) == 1:
            in_specs.append(pl.BlockSpec((tm, grp[0].shape[1]), lambda j, i: (i, 0)))
        else:
            split_tile = grp[0].shape[0] // tm
            st = split_tile
            in_specs.append(pl.BlockSpec((tm, grp[0].shape[1]), lambda j, i, st=st: (jnp.minimum(i, st - 1), 0)))
            in_specs.append(pl.BlockSpec((tm, grp[1].shape[1]), lambda j, i, st=st: (jnp.maximum(i - st, 0), 0)))
        ins += list(grp)
    in_specs += [pl.BlockSpec((1, k), lambda j, i: (0, 0)), pl.BlockSpec((k, tn), lambda j, i: (0, j))]
    return pl.pallas_call(
        functools.partial(_mm_groups_body, parts_per_group=tuple(len(grp) for grp in groups), split_tile=split_tile),
        grid=(n // tn, m_rows // tm), in_specs=in_specs,
        out_specs=pl.BlockSpec((tm, tn), lambda j, i: (i, j)),
        out_shape=jax.ShapeDtypeStruct((m_rows, n), BF16),
        compiler_params=_cparams(2), name=name,
    )(*ins, g.reshape(1, k), w)


def _glu_body(a_ref, w1_ref, w3_ref, o_ref):
    a = a_ref[...]
    h1 = jnp.dot(a, w1_ref[...], preferred_element_type=F32)
    h3 = jnp.dot(a, w3_ref[...], preferred_element_type=F32)
    o_ref[...] = (_silu(h1) * h3).astype(o_ref.dtype)


def _glu(a, w1, w3, *, m_rows, tf, name="glu"):
    tm = MM_TILE_M
    k, f = w1.shape
    return pl.pallas_call(
        _glu_body, grid=(f // tf, m_rows // tm),
        in_specs=[pl.BlockSpec((tm, k), lambda j, i: (i, 0)),
                  pl.BlockSpec((k, tf), lambda j, i: (0, j)),
                  pl.BlockSpec((k, tf), lambda j, i: (0, j))],
        out_specs=pl.BlockSpec((tm, tf), lambda j, i: (i, j)),
        out_shape=jax.ShapeDtypeStruct((m_rows, f), BF16),
        compiler_params=_cparams(2), name=name,
    )(a, w1, w3)


def _rope(x, c, s, half):
    lane = lax.broadcasted_iota(jnp.int32, x.shape, 1)
    first = (lane % (2 * half)) < half
    swapped = jnp.where(first, pltpu.roll(x, LANES - half, 1), pltpu.roll(x, half, 1))
    return x * c + swapped * s


def _prep_body(u_ref, qraw_ref, kvraw_ref, kpe_ref, cg_ref, sg_ref, cm_ref, sm_ref, gq_ref, gk_ref,
               qm_o, km_o, vm_o, qg_o, kg_o, vg_o, qn_o, kn_o, vn_o,
               *, hm, hg, hkv, hn, off_qg, off_kvg, off_n):
    cg, sg, cm, sm = cg_ref[...], sg_ref[...], cm_ref[...], sm_ref[...]
    scale_m = (MLA_NOPE + MLA_ROPE) ** -0.5 * LOG2E
    scale_g = HEAD_DIM ** -0.5 * LOG2E
    scale_n = HEAD_DIM ** -0.5
    d = HEAD_DIM
    tm = u_ref.shape[0]
    ones_rows = jnp.where(lax.broadcasted_iota(jnp.int32, (V_PAD_ROWS, tm), 0) == 0, 1.0, 0.0).astype(BF16)

    def t_bf16(v):
        return v.astype(F32).T.astype(BF16)

    kpe = _rope(kpe_ref[...].astype(F32), cm, sm, MLA_ROPE // 4)[:, :MLA_ROPE].astype(BF16)
    pe0 = hm * MLA_NOPE
    for h in range(hm):
        qm_o[h, 0:MLA_NOPE, :] = t_bf16(qraw_ref[:, h * MLA_NOPE:(h + 1) * MLA_NOPE].astype(F32) * scale_m)
        km_o[h, :, 0:MLA_NOPE] = kvraw_ref[:, h * 2 * d:h * 2 * d + MLA_NOPE]
        km_o[h, :, MLA_NOPE:MLA_NOPE + MLA_ROPE] = kpe
        vm_o[h, 0:MLA_V, :] = t_bf16(kvraw_ref[:, h * 2 * d + MLA_NOPE:(h + 1) * 2 * d])
        vm_o[h, MLA_V:MLA_V + V_PAD_ROWS, :] = ones_rows
    for j in range(hm // 2):
        pe = _rope(qraw_ref[:, pe0 + j * LANES:pe0 + (j + 1) * LANES].astype(F32), cm, sm, MLA_ROPE // 4)
        pe_t = t_bf16(pe * scale_m)
        qm_o[2 * j, MLA_NOPE:MLA_NOPE + MLA_ROPE, :] = pe_t[:MLA_ROPE]
        qm_o[2 * j + 1, MLA_NOPE:MLA_NOPE + MLA_ROPE, :] = pe_t[MLA_ROPE:]
    for h in range(hg):
        q = _rms(u_ref[:, off_qg + h * d:off_qg + (h + 1) * d].astype(F32), gq_ref[...])
        qg_o[h] = t_bf16(_rope(q, cg, sg, d // 4) * scale_g)
    for h in range(hkv):
        k = _rms(u_ref[:, off_kvg + h * d:off_kvg + (h + 1) * d].astype(F32), gk_ref[...])
        kg_o[h] = _rope(k, cg, sg, d // 4).astype(BF16)
        vg_o[h, 0:d, :] = t_bf16(u_ref[:, off_kvg + (hkv + h) * d:off_kvg + (hkv + h + 1) * d])
        vg_o[h, d:d + V_PAD_ROWS, :] = ones_rows
    for h in range(hn):
        qn_o[h] = (u_ref[:, off_n + h * d:off_n + (h + 1) * d].astype(F32) * scale_n).astype(BF16)
        kn_o[h] = u_ref[:, off_n + (hn + h) * d:off_n + (hn + h + 1) * d]
        vn_o[h] = u_ref[:, off_n + (2 * hn + h) * d:off_n + (2 * hn + h + 1) * d]


def _prep(u, qraw, kvraw, kpe, tables, g_q, g_k, *, dims, tk_m, tk_g):
    tm = ROW_TILE
    t_tot = u.shape[0]
    b, s, l = dims["B"], dims["S"], dims["L"]
    hm, hg, hkv, hn = dims["HM"], dims["HG"], dims["HKV"], dims["HN"]
    tk_len = s + l
    n_lat = b * s // tm
    spb, lpb = s // tm, l // tm

    def kmap(i):
        lat = i < n_lat
        j = i - n_lat
        bb = jnp.where(lat, i // spb, j // lpb)
        pos = jnp.where(lat, i % spb, spb + j % lpb)
        return bb, pos

    def kspec(h, dk):
        def im(i):
            bb, pos = kmap(i)
            return (bb, 0, pos, 0)
        return pl.BlockSpec((None, h, tm, dk), im)

    def vtspec(h, dv, tk):
        per = tk // tm

        def im(i):
            bb, pos = kmap(i)
            return (bb, 0, pos // per, 0, pos % per)
        return pl.BlockSpec((None, h, None, dv + V_PAD_ROWS, tm), im)

    def qtspec(h, dk):
        return pl.BlockSpec((h, None, dk, tm), lambda i: (0, i, 0, 0))

    def qspec(h, dk):
        return pl.BlockSpec((h, tm, dk), lambda i: (0, i, 0))

    def full(arr):
        return pl.BlockSpec((tm, arr.shape[1]), lambda i: (i, 0))

    dk_m = MLA_NOPE + MLA_ROPE
    d = HEAD_DIM
    out_shape = [
        jax.ShapeDtypeStruct((hm, t_tot // tm, dk_m, tm), BF16), jax.ShapeDtypeStruct((b, hm, tk_len, dk_m), BF16),
        jax.ShapeDtypeStruct((b, hm, tk_len // tk_m, MLA_V + V_PAD_ROWS, tk_m), BF16),
        jax.ShapeDtypeStruct((hg, t_tot // tm, d, tm), BF16), jax.ShapeDtypeStruct((b, hkv, tk_len, d), BF16),
        jax.ShapeDtypeStruct((b, hkv, tk_len // tk_g, d + V_PAD_ROWS, tk_g), BF16),
        jax.ShapeDtypeStruct((hn, t_tot, d), BF16), jax.ShapeDtypeStruct((b, hn, tk_len, d), BF16),
        jax.ShapeDtypeStruct((b, hn, tk_len, d), BF16),
    ]
    out_specs = [qtspec(hm, dk_m), kspec(hm, dk_m), vtspec(hm, MLA_V, tk_m),
                 qtspec(hg, d), kspec(hkv, d), vtspec(hkv, d, tk_g),
                 qspec(hn, d), kspec(hn, d), kspec(hn, d)]
    vec = pl.BlockSpec((1, d), lambda i: (0, 0))
    return pl.pallas_call(
        functools.partial(_prep_body, hm=hm, hg=hg, hkv=hkv, hn=hn,
                          off_qg=dims["OFF_QG"], off_kvg=dims["OFF_KVG"], off_n=dims["OFF_N"]),
        grid=(t_tot // tm,),
        in_specs=[full(u), full(qraw), full(kvraw), full(kpe)] + [full(t) for t in tables] + [vec, vec],
        out_specs=out_specs, out_shape=out_shape,
        compiler_params=_cparams(1), name="prep",
    )(u, qraw, kvraw, kpe, *tables, g_q.reshape(1, d), g_k.reshape(1, d))


M_INIT = -1e30


def _q_tile(qt_ref, g, tpi, t):
    parts = [qt_ref[gi, t * tpi + j] for gi in range(g) for j in range(tpi)]
    return parts[0] if len(parts) == 1 else jnp.concatenate(parts, axis=1)


def _attn_body(qt_ref, k_ref, vt_ref, o_ref, s0, s1, p0, p1, a0, a1, m_sc, acc_sc,
               *, g, tpi, tq, tk, nk, dv, nt, unroll):
    s_buf, p_buf, a_buf = (s0, s1), (p0, p1), (a0, a1)
    n_items = nt * nk

    def split(it):
        if isinstance(it, int):
            return it // nk, it % nk
        return lax.div(it, jnp.int32(nk)), lax.rem(it, jnp.int32(nk))

    def scores(it):
        t, c = split(it)
        off = c * tk if isinstance(c, int) else pl.multiple_of(c * tk, tk)
        return jnp.dot(k_ref[pl.ds(off, tk), :], _q_tile(qt_ref, g, tpi, t), preferred_element_type=F32)

    def softmax(it, slot):
        t, _ = split(it)
        s = s_buf[slot][...]
        m_prev = m_sc[t]
        m_new = jnp.maximum(m_prev, jnp.max(s, axis=0, keepdims=True))
        p_buf[slot][...] = jnp.exp2(s - m_new).astype(BF16)
        a_buf[slot][...] = jnp.exp2(m_prev - m_new)
        m_sc[t] = m_new

    def values(it, slot):
        t, c = split(it)
        acc_sc[t] = a_buf[slot][...] * acc_sc[t] + jnp.dot(vt_ref[c], p_buf[slot][...], preferred_element_type=F32)

    def stage(it, slot):
        s_buf[1 - slot][...] = scores(it + 1)
        softmax(it, slot)
        values(it - 1, 1 - slot)

    m_sc[...] = jnp.full_like(m_sc, M_INIT)
    acc_sc[...] = jnp.zeros_like(acc_sc)
    s_buf[0][...] = scores(0)
    softmax(0, 0)
    if n_items > 1:
        s_buf[1][...] = scores(1)
        n_steady = n_items - 2

        def pair(u, carry):
            it = 2 * u + 1
            stage(it, 1)
            stage(it + 1, 0)
            return carry

        lax.fori_loop(0, n_steady // 2, pair, 0, unroll=unroll)
        if n_steady % 2:
            stage(n_items - 2, 1)
        softmax(n_items - 1, (n_items - 1) % 2)
        values(n_items - 2, (n_items - 2) % 2)
    values(n_items - 1, (n_items - 1) % 2)

    def finish(t, carry):
        acc = acc_sc[t]
        o_t = acc[0:dv] / acc[dv:dv + 1]
        for gi in range(g):
            for j in range(tpi):
                row = pl.multiple_of((t * tpi + j) * tq, tq)
                col = (gi * tpi + j) * tq
                o_ref[pl.ds(row, tq), gi * dv:(gi + 1) * dv] = o_t[:, col:col + tq].T.astype(o_ref.dtype)
        return carry

    lax.fori_loop(0, nt, finish, 0)


def _attention(qt, k, vt, *, dims, g, tpi, nt, unroll, name):
    b, s_len, l_len = dims["B"], dims["S"], dims["L"]
    hk = k.shape[1]
    dk, tq = qt.shape[2:]
    nk, dvp, tk = vt.shape[2:]
    dv = dvp - V_PAD_ROWS
    steps_per_b = s_len // (nt * tpi * tq)
    n = g * tpi * tq
    return pl.pallas_call(
        functools.partial(_attn_body, g=g, tpi=tpi, tq=tq, tk=tk, nk=nk, dv=dv, nt=nt, unroll=unroll),
        grid=(b, hk, steps_per_b),
        in_specs=[pl.BlockSpec((g, nt * tpi, dk, tq), lambda bb, h, i: (h, bb * steps_per_b + i, 0, 0)),
                  pl.BlockSpec((None, None, s_len + l_len, dk), lambda bb, h, i: (bb, h, 0, 0)),
                  pl.BlockSpec((None, None, nk, dvp, tk), lambda bb, h, i: (bb, h, 0, 0, 0))],
        out_specs=pl.BlockSpec((nt * tpi * tq, g * dv), lambda bb, h, i: (bb * steps_per_b + i, h)),
        out_shape=jax.ShapeDtypeStruct((b * s_len, hk * g * dv), BF16),
        scratch_shapes=[pltpu.VMEM((tk, n), F32), pltpu.VMEM((tk, n), F32),
                        pltpu.VMEM((tk, n), BF16), pltpu.VMEM((tk, n), BF16),
                        pltpu.VMEM((1, n), F32), pltpu.VMEM((1, n), F32),
                        pltpu.VMEM((nt, 1, n), F32), pltpu.VMEM((nt, dvp, n), F32)],
        compiler_params=_cparams(3), name=name,
    )(qt, k, vt)


def _attn_ctx_body(qt_ref, k_ref, vt_ref, o_ref, *, g, tq, dv, ctx_rows):
    qt = _q_tile(qt_ref, g, 1, 0)
    tk = vt_ref.shape[-1]
    s = jnp.dot(k_ref[...], qt, preferred_element_type=F32)
    p = jnp.exp2(s - jnp.max(s, axis=0, keepdims=True)).astype(BF16)
    acc = jnp.dot(vt_ref[:, tk - ctx_rows:tk], p, preferred_element_type=F32)
    o_t = acc[0:dv] / acc[dv:dv + 1]
    for gi in range(g):
        o_ref[:, gi * dv:(gi + 1) * dv] = o_t[:, gi * tq:(gi + 1) * tq].T.astype(o_ref.dtype)


def _attention_ctx(qt, k, vt, *, dims, g, name):
    b, s_len, l_len = dims["B"], dims["S"], dims["L"]
    hk = k.shape[1]
    dk, tq = qt.shape[2:]
    nk, dvp, tk = vt.shape[2:]
    dv = dvp - V_PAD_ROWS
    cpb = l_len // tq
    lat_tiles = b * s_len // tq
    assert l_len % tq == 0 and l_len <= tk and s_len % l_len == 0
    return pl.pallas_call(
        functools.partial(_attn_ctx_body, g=g, tq=tq, dv=dv, ctx_rows=l_len),
        grid=(b, hk, cpb),
        in_specs=[pl.BlockSpec((g, 1, dk, tq), lambda bb, h, i: (h, lat_tiles + bb * cpb + i, 0, 0)),
                  pl.BlockSpec((None, None, l_len, dk), lambda bb, h, i: (bb, h, s_len // l_len, 0)),
                  pl.BlockSpec((None, None, None, dvp, tk), lambda bb, h, i: (bb, h, nk - 1, 0, 0))],
        out_specs=pl.BlockSpec((tq, g * dv), lambda bb, h, i: (bb * cpb + i, h)),
        out_shape=jax.ShapeDtypeStruct((b * l_len, hk * g * dv), BF16),
        compiler_params=_cparams(3), name=name,
    )(qt, k, vt)


def _na_body(q_ref, k_ref, v_ref, bias_ref, o_ref, *, rows_per_step, n_grid_rows, kh, s_len, l_len, lat_steps,
             with_ctx):
    w = GRID_W
    j = pl.program_id(2)
    kc = k_ref[s_len:s_len + l_len, :]
    vc = v_ref[s_len:s_len + l_len, :]
    nt = (((1,), (1,)), ((), ()))

    def latent_step():
        span = kh + rows_per_step
        r0 = j * rows_per_step
        u0 = jnp.clip(r0 - kh // 2, 0, n_grid_rows - span)
        koff = pl.multiple_of(u0 * w, w)
        kw = k_ref[pl.ds(koff, span * w), :]
        vw = v_ref[pl.ds(koff, span * w), :]
        q = q_ref[...]
        s_loc = lax.dot_general(q, kw, nt, preferred_element_type=F32) + bias_ref[(r0 - u0) // rows_per_step]
        s_ctx = lax.dot_general(q, kc, nt, preferred_element_type=F32)
        m = jnp.maximum(jnp.max(s_loc, axis=-1, keepdims=True), jnp.max(s_ctx, axis=-1, keepdims=True))
        p_loc = jnp.exp(s_loc - m)
        p_ctx = jnp.exp(s_ctx - m)
        denom = jnp.sum(p_loc, axis=-1, keepdims=True) + jnp.sum(p_ctx, axis=-1, keepdims=True)
        o = (jnp.dot(p_loc.astype(BF16), vw, preferred_element_type=F32)
             + jnp.dot(p_ctx.astype(BF16), vc, preferred_element_type=F32))
        o_ref[...] = (o / denom).astype(o_ref.dtype)

    def ctx_step():
        s = lax.dot_general(q_ref[...], kc, nt, preferred_element_type=F32)
        p = jnp.exp(s - jnp.max(s, axis=-1, keepdims=True))
        o = jnp.dot(p.astype(BF16), vc, preferred_element_type=F32)
        o_ref[...] = (o / jnp.sum(p, axis=-1, keepdims=True)).astype(o_ref.dtype)

    if with_ctx:
        pl.when(j < lat_steps)(latent_step)
        pl.when(j >= lat_steps)(ctx_step)
    else:
        latent_step()


def _na_bias(rpb, kh, n_grid_rows, rps):
    span = kh + rps
    col = np.arange(GRID_W)
    start_c = np.clip(col - NA_KW // 2, 0, GRID_W - NA_KW)
    col_mask = (col[None, :] >= start_c[:, None]) & (col[None, :] < start_c[:, None] + NA_KW)
    dc_idx = np.clip(col[None, :] - col[:, None] + NA_KW - 1, 0, 2 * NA_KW - 2)
    n_var = kh // rps + 1
    dr_idx = np.full((n_var, rps, span), -1)
    seen = set()
    for r0 in range(0, n_grid_rows, rps):
        u0 = int(np.clip(r0 - kh // 2, 0, n_grid_rows - span))
        var = (r0 - u0) // rps
        assert (r0 - u0) % rps == 0 and 0 <= var < n_var
        table = np.full((rps, span), -1)
        for qr in range(rps):
            r = r0 + qr
            start = int(np.clip(r - kh // 2, 0, n_grid_rows - kh))
            for kr in range(span):
                if start <= u0 + kr < start + kh:
                    table[qr, kr] = u0 + kr - r + NA_KH - 1
        assert var not in seen or np.array_equal(dr_idx[var], table)
        seen.add(var)
        dr_idx[var] = table
    row_mask = dr_idx >= 0
    oh_r = (dr_idx[..., None] == np.arange(2 * NA_KH - 1)).astype(np.float32)
    oh_c = (dc_idx[:, :, None] == np.arange(2 * NA_KW - 1)[None, None, :]).astype(np.float32)
    bias = jnp.einsum("vrja,hab,qkb->hvrqjk", oh_r, rpb.astype(F32), oh_c, precision=lax.Precision.HIGHEST)
    mask = row_mask[None, :, :, None, :, None] & col_mask[None, None, None, :, None, :]
    bias = jnp.where(mask, bias, NEG_INF)
    return bias.reshape(rpb.shape[0], n_var, rps * GRID_W, span * GRID_W)


def _na(q, k, v, rpb, *, dims, with_ctx):
    b, s, l, hn = dims["B"], dims["S"], dims["L"], dims["HN"]
    r_tot = s // GRID_W
    kh = min(NA_KH, r_tot)
    tq = ROW_TILE
    rps = tq // GRID_W
    assert r_tot >= kh + rps and kh % rps == 0 and (kh // 2) % rps == 0
    bias = _na_bias(rpb, kh, r_tot, rps)
    d = HEAD_DIM
    spb = r_tot // rps
    cpb = l // tq if with_ctx else 0
    lat_tiles = b * spb
    rows = b * (s + (l if with_ctx else 0))

    def qtile(bb, j):
        return jnp.where(j < spb, bb * spb + j, lat_tiles + bb * cpb + (j - spb))

    return pl.pallas_call(
        functools.partial(_na_body, rows_per_step=rps, n_grid_rows=r_tot, kh=kh, s_len=s, l_len=l, lat_steps=spb,
                          with_ctx=with_ctx),
        grid=(b, hn, spb + cpb),
        in_specs=[pl.BlockSpec((None, tq, d), lambda bb, h, j: (h, qtile(bb, j), 0)),
                  pl.BlockSpec((None, None, s + l, d), lambda bb, h, j: (bb, h, 0, 0)),
                  pl.BlockSpec((None, None, s + l, d), lambda bb, h, j: (bb, h, 0, 0)),
                  pl.BlockSpec((None,) + bias.shape[1:], lambda bb, h, j: (h, 0, 0, 0))],
        out_specs=pl.BlockSpec((tq, d), lambda bb, h, j: (qtile(bb, j), h)),
        out_shape=jax.ShapeDtypeStruct((rows, hn * d), BF16),
        compiler_params=_cparams(3), name="na",
    )(q, k, v, bias)


ROUTE_DST_BITS = 16


def _moe_body(te_ref, tv_ref, route_ref, h_ref, w1_ref, w3_ref, w2_ref, out_ref,
              xg, xb, acc_ref, ys, gsem, ssem, *, nf, tm):
    i, f = pl.program_id(0), pl.program_id(1)
    n_tiles = pl.num_programs(0)
    valid = tv_ref[i] > 0
    chunk = tm // nf
    dump1 = out_ref.shape[0] - tm

    def src_row(p):
        return lax.shift_right_logical(route_ref[p], ROUTE_DST_BITS)

    def dst_row(p):
        return route_ref[p] & ((1 << ROUTE_DST_BITS) - 1)

    def gather_copy(r, row):
        return pltpu.make_async_copy(h_ref.at[pl.ds(row, 1)], xg.at[pl.ds(r, 1)], gsem)

    def scatter_copy(slot, r, row):
        return pltpu.make_async_copy(ys.at[slot, pl.ds(r, 1)], out_ref.at[pl.ds(row, 1)], ssem.at[slot])

    def wait_gather():
        pltpu.make_async_copy(h_ref.at[pl.ds(0, tm)], xg, gsem).wait()

    def wait_scatter(slot):
        pltpu.make_async_copy(ys.at[slot], out_ref.at[pl.ds(0, tm)], ssem.at[slot]).wait()

    def issue_chunks():
        g_tile = jnp.minimum(i + 1, n_tiles - 1)
        s_tile = jnp.maximum(i - 1, 0)
        s_slot = (i + 1) % 2
        for r in range(chunk):
            rr = f * chunk + r
            gather_copy(rr, src_row(g_tile * tm + rr)).start()
            row = jnp.where(i > 0, dst_row(s_tile * tm + rr), dump1 + rr)
            scatter_copy(s_slot, rr, row).start()

    @pl.when(jnp.logical_and(i == 0, f == 0))
    def _():
        ys[...] = jnp.zeros_like(ys)

        def go(r, c):
            gather_copy(r, src_row(r)).start()
            return c
        lax.fori_loop(0, tm, go, 0)

    @pl.when(f == 0)
    def _():
        wait_gather()
        xb[...] = xg[...].astype(BF16)
        acc_ref[...] = jnp.zeros_like(acc_ref)

    @pl.when(valid)
    def _():
        issue_chunks()
        x = xb[...]
        h1 = jnp.dot(x, w1_ref[...], preferred_element_type=F32)
        h3 = jnp.dot(x, w3_ref[...], preferred_element_type=F32)
        hm = (_silu(h1) * h3).astype(BF16)
        acc_ref[...] += jnp.dot(hm, w2_ref[...], preferred_element_type=F32)

    @pl.when(jnp.logical_not(valid))
    def _():
        issue_chunks()

    @pl.when(f == nf - 1)
    def _():
        @pl.when(i > 0)
        def _():
            wait_scatter(i % 2)

        ys[i % 2] = acc_ref[...]

        @pl.when(i == n_tiles - 1)
        def _():
            def go(r, c):
                scatter_copy(i % 2, r, dst_row(i * tm + r)).start()
                return c
            lax.fori_loop(0, tm, go, 0)
            wait_scatter(i % 2)
            wait_scatter((i + 1) % 2)
            wait_gather()


def _moe_ffn(h, w1, w3, w2, plan, *, n_out_rows, tf):
    route, tile_expert, tile_valid = plan
    tm = MOE_TILE_M
    p_tot = route.shape[0]
    d = h.shape[1]
    f_dim = w1.shape[-1]
    nf = f_dim // tf
    assert tm % nf == 0 and n_out_rows + 2 * tm <= 1 << ROUTE_DST_BITS and h.shape[0] < 1 << (31 - ROUTE_DST_BITS)

    def fidx(i, f, tv):
        return jnp.where(tv[i] > 0, f, nf - 1)

    return pl.pallas_call(
        functools.partial(_moe_body, nf=nf, tm=tm),
        grid_spec=pltpu.PrefetchScalarGridSpec(
            num_scalar_prefetch=3, grid=(p_tot // tm, nf),
            in_specs=[pl.BlockSpec(memory_space=pl.ANY),
                      pl.BlockSpec((None, d, tf), lambda i, f, te, tv, rt: (te[i], 0, fidx(i, f, tv))),
                      pl.BlockSpec((None, d, tf), lambda i, f, te, tv, rt: (te[i], 0, fidx(i, f, tv))),
                      pl.BlockSpec((None, tf, d), lambda i, f, te, tv, rt: (te[i], fidx(i, f, tv), 0))],
            out_specs=pl.BlockSpec(memory_space=pl.ANY),
            scratch_shapes=[pltpu.VMEM((tm, d), F32), pltpu.VMEM((tm, d), BF16), pltpu.VMEM((tm, d), F32),
                            pltpu.VMEM((2, tm, d), F32), pltpu.SemaphoreType.DMA(()),
                            pltpu.SemaphoreType.DMA((2,))]),
        out_shape=jax.ShapeDtypeStruct((n_out_rows + 2 * tm, d), F32),
        compiler_params=_cparams(2), name="moe_ffn",
    )(tile_expert, tile_valid, route, h, w1, w3, w2)


def _moe_plan(e_idx, n_experts, tm):
    flat_e = e_idx.reshape(-1)
    n_slots = flat_e.shape[0]
    onehot = (flat_e[:, None] == jnp.arange(n_experts)[None, :]).astype(jnp.int32)
    csum = jnp.cumsum(onehot, axis=0)
    rank = jnp.sum(csum * onehot, axis=1) - 1
    counts = csum[-1]
    tiles_per = (counts + tm - 1) // tm
    cum_tiles = jnp.cumsum(tiles_per)
    row_start = (cum_tiles - tiles_per) * tm
    dest = jnp.sum(row_start[None, :] * onehot, axis=1) + rank
    n_tiles = n_slots // tm + n_experts
    n_tiles += n_tiles % 2 == 0
    p_tot = n_tiles * tm
    te = jnp.sum((jnp.arange(n_tiles)[:, None] >= cum_tiles[None, :]).astype(jnp.int32), axis=1)
    tile_valid = (te < n_experts).astype(jnp.int32)
    tile_expert = jnp.minimum(te, n_experts - 1)
    slot = jnp.arange(n_slots, dtype=jnp.int32)
    n_tok = n_slots // TOP_K
    pos = jnp.arange(p_tot, dtype=jnp.int32)
    dump = n_slots + ((pos // tm) % 2) * tm + pos % tm
    word = (slot // TOP_K) * (1 << ROUTE_DST_BITS) + (slot % TOP_K) * n_tok + slot // TOP_K
    route = dump.at[dest].set(word)
    return route, tile_expert, tile_valid


def _rope_tables(n_batch, s_len, l_len):
    t = np.arange(s_len)
    rows, cols = t // GRID_W, t % GRID_W

    def table(width):
        half = width // 2
        quarter = half // 2
        lane = np.arange(width)
        pos = np.where((lane // half)[None, :] == 0, rows[:, None], cols[:, None]).astype(np.float64)
        fi = (lane % half) % quarter
        inv = ROPE_THETA ** (-(2.0 * fi) / half)
        ang = (pos.astype(np.float32) * inv.astype(np.float32)[None, :]).astype(np.float32)
        sign = np.where((lane % half) < quarter, -1.0, 1.0)
        return np.cos(ang).astype(np.float32), (np.sin(ang) * sign[None, :]).astype(np.float32)

    cg, sg = table(HEAD_DIM)
    cm, sm = table(MLA_ROPE)
    cm, sm = np.tile(cm, (1, 2)), np.tile(sm, (1, 2))

    def flat(lat, fill):
        return np.concatenate([np.tile(lat, (n_batch, 1)),
                               np.full((n_batch * l_len, LANES), fill, np.float32)], axis=0)

    return tuple(jnp.asarray(a) for a in (flat(cg, 1.0), flat(sg, 0.0), flat(cm, 1.0), flat(sm, 0.0)))


def _round_up(a, m):
    return (a + m - 1) // m * m


def _pick_tile(n, pref):
    for t in pref:
        if n % t == 0:
            return t
    return n


def kernel(x, c, ctx, c_ctx, w_ada, b_ada, g_pre_mix, g_post_mix, g_pre_ffn, g_post_ffn, w_in, g_q_a, g_kv_a, w_q_up, w_kv_up, g_q_gqa, g_k_gqa, na_rpb, g_grp, w_out, ffn_w1, ffn_w3, ffn_w2, router, moe_w1, moe_w3, moe_w2):
    n_batch, s_len, d = x.shape
    l_len = ctx.shape[1]
    depth = w_ada.shape[0]
    q_lora, kv_lora = g_q_a.shape[-1], g_kv_a.shape[-1]
    dk_m = MLA_NOPE + MLA_ROPE
    hm = w_q_up.shape[-1] // dk_m
    hg = 3 * d // (8 * HEAD_DIM)
    hkv = hg // 3
    hn = d // (4 * HEAD_DIM)
    g_gqa = hg // hkv
    n_lat, n_ctx = n_batch * s_len, n_batch * l_len
    t_tot = n_lat + n_ctx
    assert s_len % MM_TILE_M == 0 and n_ctx % MM_TILE_M == 0 and l_len % ROW_TILE == 0
    assert hm % 2 == 0 and n_batch + 1 <= SUBLANES

    off_qg = q_lora + kv_lora
    off_kvg = off_qg + hg * HEAD_DIM
    off_n = off_kvg + 2 * hkv * HEAD_DIM
    n_main = off_n + 3 * hn * HEAD_DIM
    dims = dict(B=n_batch, S=s_len, L=l_len, HM=hm, HG=hg, HKV=hkv, HN=hn,
                OFF_QG=off_qg, OFF_KVG=off_kvg, OFF_N=off_n)

    tables = _rope_tables(n_batch, s_len, l_len)
    tk_all = s_len + l_len
    tk_m = _pick_tile(tk_all, (768, 512, 256))
    tk_g = tk_m

    mods = _ada(jnp.concatenate([c_ctx[None, :], c], axis=0), w_ada, b_ada)

    lat_tiles, all_tiles = n_lat // ROW_TILE, t_tot // ROW_TILE
    tiles_per_b = s_len // ROW_TILE

    def group_of(i):
        return jnp.where(i < lat_tiles, 1 + i // tiles_per_b, 0)

    def modtab_of(i):
        return mods[i, :n_batch + 1].reshape((n_batch + 1) * 6, 1, d)

    xs = (x.reshape(n_lat, d), ctx.reshape(n_ctx, d))
    modtab = modtab_of(0)
    _, h, _ = _row_call(xs, n_tiles=all_tiles, group_fn=group_of, modtab=modtab,
                        nxt=(g_pre_mix[0], 0, 1, modtab), name="row_in")

    for i in range(depth):
        last = i == depth - 1
        modtab = modtab_of(i)
        m_out = n_lat if last else t_tot
        out_tiles = m_out // ROW_TILE

        wi = w_in[i]
        c_kpe = q_lora + kv_lora
        w_main = jnp.concatenate([wi[:, :c_kpe], wi[:, c_kpe + MLA_ROPE:]], axis=1).astype(BF16)
        w_kpe = jnp.zeros((d, LANES), BF16).at[:, :MLA_ROPE].set(wi[:, c_kpe:c_kpe + MLA_ROPE].astype(BF16))
        u = _matmul(h, w_main, m_rows=t_tot, tn=_pick_tile(n_main, (1024, 512, 256, 128)), name="mm_in")
        kpe = _matmul(h, w_kpe, m_rows=t_tot, tn=LANES, name="mm_kpe")
        wq = w_q_up[i].reshape(q_lora, hm, dk_m)
        wq = jnp.concatenate([wq[:, :, :MLA_NOPE].reshape(q_lora, hm * MLA_NOPE),
                              wq[:, :, MLA_NOPE:].reshape(q_lora, hm * MLA_ROPE)], axis=1).astype(BF16)
        a_cols = _round_up(off_qg, LANES)
        qraw = _matmul(u, wq, m_rows=t_tot, tn=_pick_tile(hm * dk_m, (768, 384, 128)), a_block_cols=a_cols,
                       a_lo=0, a_hi=q_lora, rms_g=g_q_a[i], name="mm_qup")
        kvraw = _matmul(u, w_kv_up[i].astype(BF16), m_rows=t_tot,
                        tn=_pick_tile(hm * (MLA_NOPE + MLA_V), (1024, 768, 512, 256)), a_block_cols=a_cols,
                        a_lo=q_lora, a_hi=q_lora + kv_lora, rms_g=g_kv_a[i], name="mm_kvup")
        qm, km, vm, qg, kg, vg, qn, kn, vn = _prep(u, qraw, kvraw, kpe, tables, g_q_gqa[i], g_k_gqa[i],
                                                   dims=dims, tk_m=tk_m, tk_g=tk_g)

        q_tiles_per_b = s_len // ROW_TILE
        o_m = (_attention(qm, km, vm, dims=dims, g=1, tpi=MLA_TILES_PER_ITEM,
                          nt=q_tiles_per_b // MLA_TILES_PER_ITEM, unroll=4, name="attn_mla"),)
        o_g = (_attention(qg, kg, vg, dims=dims, g=g_gqa, tpi=1, nt=_pick_tile(q_tiles_per_b, (16, 8, 4, 2)),
                          unroll=1, name="attn_gqa"),)
        if not last:
            o_m += (_attention_ctx(qm, km, vm, dims=dims, g=1, name="attn_mla_ctx"),)
            o_g += (_attention_ctx(qg, kg, vg, dims=dims, g=g_gqa, name="attn_gqa_ctx"),)
        o_n = (_na(qn, kn, vn, na_rpb[i], dims=dims, with_ctx=not last),)

        y = _matmul_groups((o_m, o_g, o_n), g_grp[i], w_out[i].astype(BF16), m_rows=m_out,
                           tn=_pick_tile(d, (1024, 512, 256)), name="mm_out")

        j = i // 2
        moe = i % 2 == 1
        res = ([(y, 0)], 2, g_post_mix[i], None)
        nxt = (g_pre_ffn[i], 3, 4, modtab)
        if moe:
            x_all, hf, rt = _row_call(xs, n_tiles=out_tiles, group_fn=group_of, modtab=modtab, res=res, nxt=nxt,
                                      router=router[j], h_dtype=F32, name="row_mix")
            n_experts = router.shape[-1]
            e_idx = rt[:, TOP_K:2 * TOP_K].astype(jnp.int32)
            plan = _moe_plan(e_idx, n_experts, MOE_TILE_M)
            f_weights = rt
            f_out = _moe_ffn(hf, moe_w1[j].astype(BF16), moe_w3[j].astype(BF16), moe_w2[j].astype(BF16), plan,
                             n_out_rows=TOP_K * m_out, tf=_pick_tile(moe_w1.shape[-1], (256, 128)))
            f_blocks = [(f_out, k * out_tiles) for k in range(TOP_K)]
        else:
            x_all, h, _ = _row_call(xs, n_tiles=out_tiles, group_fn=group_of, modtab=modtab, res=res, nxt=nxt,
                                    name="row_mix")
            f_dim = ffn_w1.shape[-1]
            f_pad = _round_up(f_dim, 1024)
            pad = ((0, 0), (0, f_pad - f_dim))
            w1 = jnp.pad(ffn_w1[j].astype(BF16), pad)
            w3 = jnp.pad(ffn_w3[j].astype(BF16), pad)
            w2 = jnp.pad(ffn_w2[j].astype(BF16), (pad[1], pad[0]))
            gact = _glu(h, w1, w3, m_rows=m_out, tf=1024, name="ffn_glu")
            f_out = _matmul(gact, w2, m_rows=m_out, tn=_pick_tile(d, (512, 256)), name="ffn_down")
            f_blocks = [(f_out, 0)]
            f_weights = None
        xs = (x_all,)

        res = (f_blocks, 5, g_post_ffn[i], f_weights)
        if last:
            x_all, _, _ = _row_call(xs, n_tiles=out_tiles, group_fn=group_of, modtab=modtab, res=res, name="row_ffn")
        else:
            x_all, h, _ = _row_call(xs, n_tiles=out_tiles, group_fn=group_of, modtab=modtab, res=res,
                                    nxt=(g_pre_mix[i + 1], 0, 1, modtab_of(i + 1)), name="row_ffn")
        xs = (x_all,)
    return xs[0][:n_lat].reshape(n_batch, s_len, d)
```

```python
import functools
import math

import jax
import jax.numpy as jnp
import numpy as np
from jax import lax
from jax.experimental import pallas as pl
from jax.experimental.pallas import tpu as pltpu

F32 = jnp.float32
BF16 = jnp.bfloat16

GRID_W = 64
HEAD_DIM = 128
ROPE_THETA = 10000.0
EPS = 1e-6
NEG_INF = -1e30
MLA_NOPE = 128
MLA_ROPE = 64
MLA_V = 128
NA_KH = 8
NA_KW = 16
TOP_K = 2
LOG2E = math.log2(math.e)

LANES = 128
SUBLANES = 8
BF16_SUBLANES = 16
VMEM_LIMIT_BYTES = 56 * 1024 * 1024

ROW_TILE = 256
MM_TILE_M = 512
MOE_TILE_M = 512
V_PAD_ROWS = BF16_SUBLANES
MLA_TILES_PER_ITEM = 1


def _cparams(n_axes):
    return pltpu.CompilerParams(dimension_semantics=("arbitrary",) * n_axes,
                                vmem_limit_bytes=VMEM_LIMIT_BYTES)


def _rms(v, g):
    ms = jnp.mean(v * v, axis=-1, keepdims=True)
    return v * lax.rsqrt(ms + EPS) * g


def _silu(v):
    return v * jax.nn.sigmoid(v)


ADA_K_CHUNK = 64


def _ada_body(ct_ref, w_ref, b_ref, o_ref, sb_ref, *, n_rows):
    d, tn = w_ref.shape

    @pl.when(jnp.logical_and(pl.program_id(0) == 0, pl.program_id(1) == 0))
    def _():
        s = _silu(ct_ref[...])
        for r in range(n_rows):
            sb_ref[r] = jnp.broadcast_to(s[:, r:r + 1], (d, LANES))

    def step(i, accs):
        off = pl.multiple_of(i * ADA_K_CHUNK, ADA_K_CHUNK)
        w = w_ref[pl.ds(off, ADA_K_CHUNK), :]
        out = []
        for r in range(n_rows):
            sb = sb_ref[r, pl.ds(off, ADA_K_CHUNK), :]
            prod = w * jnp.concatenate([sb] * (tn // LANES), axis=1)
            part = prod[0:SUBLANES]
            for j in range(1, ADA_K_CHUNK // SUBLANES):
                part = part + prod[j * SUBLANES:(j + 1) * SUBLANES]
            out.append(accs[r] + part)
        return tuple(out)

    accs = lax.fori_loop(0, d // ADA_K_CHUNK, step,
                         tuple(jnp.zeros((SUBLANES, tn), F32) for _ in range(n_rows)), unroll=2)
    rows = [jnp.sum(a, axis=0, keepdims=True) for a in accs]
    rows.append(jnp.zeros((SUBLANES - n_rows, tn), F32))
    o_ref[...] = jnp.concatenate(rows, axis=0) + b_ref[...]


def _ada(cond, w_ada, b_ada):
    n_rows, d = cond.shape
    depth, _, n = w_ada.shape
    tn = 512
    ct = jnp.zeros((d, SUBLANES), F32).at[:, :n_rows].set(cond.T)
    return pl.pallas_call(
        functools.partial(_ada_body, n_rows=n_rows),
        grid=(depth, n // tn),
        in_specs=[pl.BlockSpec((d, SUBLANES), lambda l, j: (0, 0)),
                  pl.BlockSpec((None, d, tn), lambda l, j: (l, 0, j)),
                  pl.BlockSpec((None, 1, tn), lambda l, j: (l, 0, j))],
        out_specs=pl.BlockSpec((None, SUBLANES, tn), lambda l, j: (l, 0, j)),
        out_shape=jax.ShapeDtypeStruct((depth, SUBLANES, n), F32),
        scratch_shapes=[pltpu.VMEM((n_rows, d, LANES), F32)],
        compiler_params=_cparams(2),
        name="ada",
    )(ct, w_ada, b_ada.reshape(depth, 1, n))


def _row_body(*refs, n_x, x_split, n_y, y_weighted, has_next, router, n_experts):
    it = iter(refs)
    x_refs = [next(it) for _ in range(n_x)]
    has_res = n_y > 0
    if has_res:
        y_refs = [next(it) for _ in range(n_y)]
        if y_weighted:
            yw_ref = next(it)
        gate_ref, gpost_ref = next(it), next(it)
    if has_next:
        gpre_ref, shift_ref, scale_ref = next(it), next(it), next(it)
    if router:
        rhi_ref, rlo_ref = next(it), next(it)
    if has_res:
        xo_ref = next(it)
    if has_next:
        h_ref = next(it)
    if router:
        rt_ref = next(it)

    if n_x == 1:
        x = x_refs[0][...]
    else:
        x = jnp.where(pl.program_id(0) < x_split, x_refs[0][...], x_refs[1][...])
    if has_res:
        if y_weighted:
            y = yw_ref[:, 0:1] * y_refs[0][...].astype(F32)
            for k in range(1, n_y):
                y = y + yw_ref[:, k:k + 1] * y_refs[k][...].astype(F32)
        else:
            y = y_refs[0][...].astype(F32)
            for y_ref in y_refs[1:]:
                y = y + y_ref[...].astype(F32)
        x = x + gate_ref[...] * _rms(y, gpost_ref[...])
        xo_ref[...] = x
    if has_next:
        h = _rms(x, gpre_ref[...]) * (1.0 + scale_ref[...]) + shift_ref[...]
        h_ref[...] = h.astype(h_ref.dtype)
    if router:
        hi = h.astype(BF16)
        lo = (h - hi.astype(F32)).astype(BF16)
        logits = (jnp.dot(hi, rhi_ref[...], preferred_element_type=F32)
                  + jnp.dot(hi, rlo_ref[...], preferred_element_type=F32)
                  + jnp.dot(lo, rhi_ref[...], preferred_element_type=F32))
        lane = lax.broadcasted_iota(jnp.int32, logits.shape, 1)
        lg = jnp.where(lane < n_experts, logits, -jnp.inf)
        v1 = jnp.max(lg, axis=-1, keepdims=True)
        i1 = jnp.min(jnp.where(lg == v1, lane, LANES), axis=-1, keepdims=True)
        lg2 = jnp.where(lane == i1, -jnp.inf, lg)
        v2 = jnp.max(lg2, axis=-1, keepdims=True)
        i2 = jnp.min(jnp.where(lg2 == v2, lane, LANES), axis=-1, keepdims=True)
        e = jnp.exp(v2 - v1)
        g1 = 1.0 / (1.0 + e)
        g2 = e / (1.0 + e)
        rt_ref[...] = jnp.where(lane == 0, g1,
                                jnp.where(lane == 1, g2,
                                          jnp.where(lane == 2, i1.astype(F32),
                                                    jnp.where(lane == 3, i2.astype(F32), 0.0))))


def _row_call(xs, *, n_tiles, group_fn, modtab, res=None, nxt=None, router=None, h_dtype=BF16, name="row"):
    tm = ROW_TILE
    d = xs[0].shape[-1]
    has_res, has_next = res is not None, nxt is not None

    def modspec(k):
        return pl.BlockSpec((None, 1, d), lambda i: (group_fn(i) * 6 + k, 0, 0))

    vecspec = pl.BlockSpec((1, d), lambda i: (0, 0))
    x_split = 0
    if len(xs) == 1:
        ins, in_specs = [xs[0]], [pl.BlockSpec((tm, d), lambda i: (i, 0))]
    else:
        x_split = xs[0].shape[0] // tm
        ins = list(xs)
        in_specs = [pl.BlockSpec((tm, d), lambda i: (jnp.minimum(i, x_split - 1), 0)),
                    pl.BlockSpec((tm, d), lambda i: (jnp.maximum(i - x_split, 0), 0))]
    n_y = 0
    y_weights = None
    if has_res:
        y_blocks, gate_k, g_post, y_weights = res
        n_y = len(y_blocks)
        for arr, tile0 in y_blocks:
            ins.append(arr)
            in_specs.append(pl.BlockSpec((tm, d), lambda i, tile0=tile0: (i + tile0, 0)))
        if y_weights is not None:
            ins.append(y_weights)
            in_specs.append(pl.BlockSpec((tm, y_weights.shape[1]), lambda i: (i, 0)))
        ins += [modtab, g_post.reshape(1, d)]
        in_specs += [modspec(gate_k), vecspec]
    if has_next:
        g_pre, shift_k, scale_k, modtab_n = nxt
        ins += [g_pre.reshape(1, d), modtab_n, modtab_n]
        in_specs += [vecspec, modspec(shift_k), modspec(scale_k)]
    n_experts = 0
    if router is not None:
        n_experts = router.shape[-1]
        rp = jnp.zeros((d, LANES), F32).at[:, :n_experts].set(router)
        rhi = rp.astype(BF16)
        rlo = (rp - rhi.astype(F32)).astype(BF16)
        ins += [rhi, rlo]
        in_specs += [pl.BlockSpec((d, LANES), lambda i: (0, 0))] * 2
    out_shape, out_specs = [], []
    rowspec = pl.BlockSpec((tm, d), lambda i: (i, 0))
    if has_res:
        out_shape.append(jax.ShapeDtypeStruct((n_tiles * tm, d), F32))
        out_specs.append(rowspec)
    if has_next:
        out_shape.append(jax.ShapeDtypeStruct((n_tiles * tm, d), h_dtype))
        out_specs.append(rowspec)
    if router is not None:
        out_shape.append(jax.ShapeDtypeStruct((n_tiles * tm, LANES), F32))
        out_specs.append(pl.BlockSpec((tm, LANES), lambda i: (i, 0)))
    outs = pl.pallas_call(
        functools.partial(_row_body, n_x=len(xs), x_split=x_split, n_y=n_y, y_weighted=y_weights is not None,
                          has_next=has_next,
                          router=router is not None, n_experts=n_experts),
        grid=(n_tiles,), in_specs=in_specs, out_specs=out_specs, out_shape=out_shape,
        compiler_params=_cparams(1), name=name,
    )(*ins)
    outs = list(outs)
    x_new = outs.pop(0) if has_res else None
    h = outs.pop(0) if has_next else None
    rt = outs.pop(0) if router is not None else None
    return x_new, h, rt


def _mm_body(*refs, a_lo, a_hi, rms):
    if rms:
        a_ref, g_ref, w_ref, o_ref = refs
    else:
        a_ref, w_ref, o_ref = refs
    a = a_ref[:, a_lo:a_hi]
    if rms:
        a = _rms(a.astype(F32), g_ref[...]).astype(BF16)
    o_ref[...] = jnp.dot(a, w_ref[...], preferred_element_type=F32).astype(o_ref.dtype)


def _matmul(a, w, *, m_rows, tn, out_dtype=BF16, a_block_cols=None, a_lo=0, a_hi=None, rms_g=None, name="mm"):
    tm = MM_TILE_M
    k, n = w.shape
    if a_block_cols is None:
        a_block_cols = a.shape[1]
    if a_hi is None:
        a_hi = a_lo + k
    ins, in_specs = [a], [pl.BlockSpec((tm, a_block_cols), lambda j, i: (i, 0))]
    if rms_g is not None:
        ins.append(rms_g.reshape(1, k))
        in_specs.append(pl.BlockSpec((1, k), lambda j, i: (0, 0)))
    ins.append(w)
    in_specs.append(pl.BlockSpec((k, tn), lambda j, i: (0, j)))
    return pl.pallas_call(
        functools.partial(_mm_body, a_lo=a_lo, a_hi=a_hi, rms=rms_g is not None),
        grid=(n // tn, m_rows // tm), in_specs=in_specs,
        out_specs=pl.BlockSpec((tm, tn), lambda j, i: (i, j)),
        out_shape=jax.ShapeDtypeStruct((m_rows, n), out_dtype),
        compiler_params=_cparams(2), name=name,
    )(*ins)


def _mm_groups_body(*refs, parts_per_group, split_tile):
    n_a = sum(parts_per_group)
    a_refs = refs[:n_a]
    g_ref, w_ref, o_ref = refs[n_a:]
    parts, lo, pos = [], 0, 0
    for n_parts in parts_per_group:
        if n_parts == 1:
            a = a_refs[pos][...]
        else:
            a = jnp.where(pl.program_id(1) < split_tile, a_refs[pos][...], a_refs[pos + 1][...])
        pos += n_parts
        width = a.shape[1]
        parts.append(_rms(a.astype(F32), g_ref[:, lo:lo + width]).astype(BF16))
        lo += width
    a = jnp.concatenate(parts, axis=1)
    o_ref[...] = jnp.dot(a, w_ref[...], preferred_element_type=F32).astype(o_ref.dtype)


def _matmul_groups(groups, g, w, *, m_rows, tn, name):
    tm = MM_TILE_M
    k, n = w.shape
    split_tile = 0
    ins, in_specs = [], []
    for grp in groups:
        if len(grp) == 1:
            in_specs.append(pl.BlockSpec((tm, grp[0].shape[1]), lambda j, i: (i, 0)))
        else:
            split_tile = grp[0].shape[0] // tm
            st = split_tile
            in_specs.append(pl.BlockSpec((tm, grp[0].shape[1]), lambda j, i, st=st: (jnp.minimum(i, st - 1), 0)))
            in_specs.append(pl.BlockSpec((tm, grp[1].shape[1]), lambda j, i, st=st: (jnp.maximum(i - st, 0), 0)))
        ins += list(grp)
    in_specs += [pl.BlockSpec((1, k), lambda j, i: (0, 0)), pl.BlockSpec((k, tn), lambda j, i: (0, j))]
    return pl.pallas_call(
        functools.partial(_mm_groups_body, parts_per_group=tuple(len(grp) for grp in groups), split_tile=split_tile),
        grid=(n // tn, m_rows // tm), in_specs=in_specs,
        out_specs=pl.BlockSpec((tm, tn), lambda j, i: (i, j)),
        out_shape=jax.ShapeDtypeStruct((m_rows, n), BF16),
        compiler_params=_cparams(2), name=name,
    )(*ins, g.reshape(1, k), w)


def _glu_body(a_ref, w1_ref, w3_ref, o_ref):
    a = a_ref[...]
    h1 = jnp.dot(a, w1_ref[...], preferred_element_type=F32)
    h3 = jnp.dot(a, w3_ref[...], preferred_element_type=F32)
    o_ref[...] = (_silu(h1) * h3).astype(o_ref.dtype)


def _glu(a, w1, w3, *, m_rows, tf, name="glu"):
    tm = MM_TILE_M
    k, f = w1.shape
    return pl.pallas_call(
        _glu_body, grid=(f // tf, m_rows // tm),
        in_specs=[pl.BlockSpec((tm, k), lambda j, i: (i, 0)),
                  pl.BlockSpec((k, tf), lambda j, i: (0, j)),
                  pl.BlockSpec((k, tf), lambda j, i: (0, j))],
        out_specs=pl.BlockSpec((tm, tf), lambda j, i: (i, j)),
        out_shape=jax.ShapeDtypeStruct((m_rows, f), BF16),
        compiler_params=_cparams(2), name=name,
    )(a, w1, w3)


def _rope(x, c, s, half):
    lane = lax.broadcasted_iota(jnp.int32, x.shape, 1)
    first = (lane % (2 * half)) < half
    swapped = jnp.where(first, pltpu.roll(x, LANES - half, 1), pltpu.roll(x, half, 1))
    return x * c + swapped * s


def _prep_body(u_ref, qraw_ref, kvraw_ref, kpe_ref, cg_ref, sg_ref, cm_ref, sm_ref, gq_ref, gk_ref,
               qm_o, km_o, vm_o, qg_o, kg_o, vg_o, qn_o, kn_o, vn_o,
               *, hm, hg, hkv, hn, off_qg, off_kvg, off_n):
    cg, sg, cm, sm = cg_ref[...], sg_ref[...], cm_ref[...], sm_ref[...]
    scale_m = (MLA_NOPE + MLA_ROPE) ** -0.5 * LOG2E
    scale_g = HEAD_DIM ** -0.5 * LOG2E
    scale_n = HEAD_DIM ** -0.5
    d = HEAD_DIM
    tm = u_ref.shape[0]
    ones_rows = jnp.where(lax.broadcasted_iota(jnp.int32, (V_PAD_ROWS, tm), 0) == 0, 1.0, 0.0).astype(BF16)

    def t_bf16(v):
        return v.astype(F32).T.astype(BF16)

    kpe = _rope(kpe_ref[...].astype(F32), cm, sm, MLA_ROPE // 4)[:, :MLA_ROPE].astype(BF16)
    pe0 = hm * MLA_NOPE
    for h in range(hm):
        qm_o[h, 0:MLA_NOPE, :] = t_bf16(qraw_ref[:, h * MLA_NOPE:(h + 1) * MLA_NOPE].astype(F32) * scale_m)
        km_o[h, :, 0:MLA_NOPE] = kvraw_ref[:, h * 2 * d:h * 2 * d + MLA_NOPE]
        km_o[h, :, MLA_NOPE:MLA_NOPE + MLA_ROPE] = kpe
        vm_o[h, 0:MLA_V, :] = t_bf16(kvraw_ref[:, h * 2 * d + MLA_NOPE:(h + 1) * 2 * d])
        vm_o[h, MLA_V:MLA_V + V_PAD_ROWS, :] = ones_rows
    for j in range(hm // 2):
        pe = _rope(qraw_ref[:, pe0 + j * LANES:pe0 + (j + 1) * LANES].astype(F32), cm, sm, MLA_ROPE // 4)
        pe_t = t_bf16(pe * scale_m)
        qm_o[2 * j, MLA_NOPE:MLA_NOPE + MLA_ROPE, :] = pe_t[:MLA_ROPE]
        qm_o[2 * j + 1, MLA_NOPE:MLA_NOPE + MLA_ROPE, :] = pe_t[MLA_ROPE:]
    for h in range(hg):
        q = _rms(u_ref[:, off_qg + h * d:off_qg + (h + 1) * d].astype(F32), gq_ref[...])
        qg_o[h] = t_bf16(_rope(q, cg, sg, d // 4) * scale_g)
    for h in range(hkv):
        k = _rms(u_ref[:, off_kvg + h * d:off_kvg + (h + 1) * d].astype(F32), gk_ref[...])
        kg_o[h] = _rope(k, cg, sg, d // 4).astype(BF16)
        vg_o[h, 0:d, :] = t_bf16(u_ref[:, off_kvg + (hkv + h) * d:off_kvg + (hkv + h + 1) * d])
        vg_o[h, d:d + V_PAD_ROWS, :] = ones_rows
    for h in range(hn):
        qn_o[h] = (u_ref[:, off_n + h * d:off_n + (h + 1) * d].astype(F32) * scale_n).astype(BF16)
        kn_o[h] = u_ref[:, off_n + (hn + h) * d:off_n + (hn + h + 1) * d]
        vn_o[h] = u_ref[:, off_n + (2 * hn + h) * d:off_n + (2 * hn + h + 1) * d]


def _prep(u, qraw, kvraw, kpe, tables, g_q, g_k, *, dims, tk_m, tk_g):
    tm = ROW_TILE
    t_tot = u.shape[0]
    b, s, l = dims["B"], dims["S"], dims["L"]
    hm, hg, hkv, hn = dims["HM"], dims["HG"], dims["HKV"], dims["HN"]
    tk_len = s + l
    n_lat = b * s // tm
    spb, lpb = s // tm, l // tm

    def kmap(i):
        lat = i < n_lat
        j = i - n_lat
        bb = jnp.where(lat, i // spb, j // lpb)
        pos = jnp.where(lat, i % spb, spb + j % lpb)
        return bb, pos

    def kspec(h, dk):
        def im(i):
            bb, pos = kmap(i)
            return (bb, 0, pos, 0)
        return pl.BlockSpec((None, h, tm, dk), im)

    def vtspec(h, dv, tk):
        per = tk // tm

        def im(i):
            bb, pos = kmap(i)
            return (bb, 0, pos // per, 0, pos % per)
        return pl.BlockSpec((None, h, None, dv + V_PAD_ROWS, tm), im)

    def qtspec(h, dk):
        return pl.BlockSpec((h, None, dk, tm), lambda i: (0, i, 0, 0))

    def qspec(h, dk):
        return pl.BlockSpec((h, tm, dk), lambda i: (0, i, 0))

    def full(arr):
        return pl.BlockSpec((tm, arr.shape[1]), lambda i: (i, 0))

    dk_m = MLA_NOPE + MLA_ROPE
    d = HEAD_DIM
    out_shape = [
        jax.ShapeDtypeStruct((hm, t_tot // tm, dk_m, tm), BF16), jax.ShapeDtypeStruct((b, hm, tk_len, dk_m), BF16),
        jax.ShapeDtypeStruct((b, hm, tk_len // tk_m, MLA_V + V_PAD_ROWS, tk_m), BF16),
        jax.ShapeDtypeStruct((hg, t_tot // tm, d, tm), BF16), jax.ShapeDtypeStruct((b, hkv, tk_len, d), BF16),
        jax.ShapeDtypeStruct((b, hkv, tk_len // tk_g, d + V_PAD_ROWS, tk_g), BF16),
        jax.ShapeDtypeStruct((hn, t_tot, d), BF16), jax.ShapeDtypeStruct((b, hn, tk_len, d), BF16),
        jax.ShapeDtypeStruct((b, hn, tk_len, d), BF16),
    ]
    out_specs = [qtspec(hm, dk_m), kspec(hm, dk_m), vtspec(hm, MLA_V, tk_m),
                 qtspec(hg, d), kspec(hkv, d), vtspec(hkv, d, tk_g),
                 qspec(hn, d), kspec(hn, d), kspec(hn, d)]
    vec = pl.BlockSpec((1, d), lambda i: (0, 0))
    return pl.pallas_call(
        functools.partial(_prep_body, hm=hm, hg=hg, hkv=hkv, hn=hn,
                          off_qg=dims["OFF_QG"], off_kvg=dims["OFF_KVG"], off_n=dims["OFF_N"]),
        grid=(t_tot // tm,),
        in_specs=[full(u), full(qraw), full(kvraw), full(kpe)] + [full(t) for t in tables] + [vec, vec],
        out_specs=out_specs, out_shape=out_shape,
        compiler_params=_cparams(1), name="prep",
    )(u, qraw, kvraw, kpe, *tables, g_q.reshape(1, d), g_k.reshape(1, d))


M_INIT = -1e30


def _q_tile(qt_ref, g, tpi, t):
    parts = [qt_ref[gi, t * tpi + j] for gi in range(g) for j in range(tpi)]
    return parts[0] if len(parts) == 1 else jnp.concatenate(parts, axis=1)


def _attn_body(qt_ref, k_ref, vt_ref, o_ref, s0, s1, p0, p1, a0, a1, m_sc, acc_sc,
               *, g, tpi, tq, tk, nk, dv, nt, unroll):
    s_buf, p_buf, a_buf = (s0, s1), (p0, p1), (a0, a1)
    n_items = nt * nk

    def split(it):
        if isinstance(it, int):
            return it // nk, it % nk
        return lax.div(it, jnp.int32(nk)), lax.rem(it, jnp.int32(nk))

    def scores(it):
        t, c = split(it)
        off = c * tk if isinstance(c, int) else pl.multiple_of(c * tk, tk)
        return jnp.dot(k_ref[pl.ds(off, tk), :], _q_tile(qt_ref, g, tpi, t), preferred_element_type=F32)

    def softmax(it, slot):
        t, _ = split(it)
        s = s_buf[slot][...]
        m_prev = m_sc[t]
        m_new = jnp.maximum(m_prev, jnp.max(s, axis=0, keepdims=True))
        p_buf[slot][...] = jnp.exp2(s - m_new).astype(BF16)
        a_buf[slot][...] = jnp.exp2(m_prev - m_new)
        m_sc[t] = m_new

    def values(it, slot):
        t, c = split(it)
        acc_sc[t] = a_buf[slot][...] * acc_sc[t] + jnp.dot(vt_ref[c], p_buf[slot][...], preferred_element_type=F32)

    def stage(it, slot):
        s_buf[1 - slot][...] = scores(it + 1)
        softmax(it, slot)
        values(it - 1, 1 - slot)

    m_sc[...] = jnp.full_like(m_sc, M_INIT)
    acc_sc[...] = jnp.zeros_like(acc_sc)
    s_buf[0][...] = scores(0)
    softmax(0, 0)
    if n_items > 1:
        s_buf[1][...] = scores(1)
        n_steady = n_items - 2

        def pair(u, carry):
            it = 2 * u + 1
            stage(it, 1)
            stage(it + 1, 0)
            return carry

        lax.fori_loop(0, n_steady // 2, pair, 0, unroll=unroll)
        if n_steady % 2:
            stage(n_items - 2, 1)
        softmax(n_items - 1, (n_items - 1) % 2)
        values(n_items - 2, (n_items - 2) % 2)
    values(n_items - 1, (n_items - 1) % 2)

    def finish(t, carry):
        acc = acc_sc[t]
        o_t = acc[0:dv] / acc[dv:dv + 1]
        for gi in range(g):
            for j in range(tpi):
                row = pl.multiple_of((t * tpi + j) * tq, tq)
                col = (gi * tpi + j) * tq
                o_ref[pl.ds(row, tq), gi * dv:(gi + 1) * dv] = o_t[:, col:col + tq].T.astype(o_ref.dtype)
        return carry

    lax.fori_loop(0, nt, finish, 0)


def _attention(qt, k, vt, *, dims, g, tpi, nt, unroll, name):
    b, s_len, l_len = dims["B"], dims["S"], dims["L"]
    hk = k.shape[1]
    dk, tq = qt.shape[2:]
    nk, dvp, tk = vt.shape[2:]
    dv = dvp - V_PAD_ROWS
    steps_per_b = s_len // (nt * tpi * tq)
    n = g * tpi * tq
    return pl.pallas_call(
        functools.partial(_attn_body, g=g, tpi=tpi, tq=tq, tk=tk, nk=nk, dv=dv, nt=nt, unroll=unroll),
        grid=(b, hk, steps_per_b),
        in_specs=[pl.BlockSpec((g, nt * tpi, dk, tq), lambda bb, h, i: (h, bb * steps_per_b + i, 0, 0)),
                  pl.BlockSpec((None, None, s_len + l_len, dk), lambda bb, h, i: (bb, h, 0, 0)),
                  pl.BlockSpec((None, None, nk, dvp, tk), lambda bb, h, i: (bb, h, 0, 0, 0))],
        out_specs=pl.BlockSpec((nt * tpi * tq, g * dv), lambda bb, h, i: (bb * steps_per_b + i, h)),
        out_shape=jax.ShapeDtypeStruct((b * s_len, hk * g * dv), BF16),
        scratch_shapes=[pltpu.VMEM((tk, n), F32), pltpu.VMEM((tk, n), F32),
                        pltpu.VMEM((tk, n), BF16), pltpu.VMEM((tk, n), BF16),
                        pltpu.VMEM((1, n), F32), pltpu.VMEM((1, n), F32),
                        pltpu.VMEM((nt, 1, n), F32), pltpu.VMEM((nt, dvp, n), F32)],
        compiler_params=_cparams(3), name=name,
    )(qt, k, vt)


def _attn_ctx_body(qt_ref, k_ref, vt_ref, o_ref, *, g, tq, dv, ctx_rows):
    qt = _q_tile(qt_ref, g, 1, 0)
    tk = vt_ref.shape[-1]
    s = jnp.dot(k_ref[...], qt, preferred_element_type=F32)
    p = jnp.exp2(s - jnp.max(s, axis=0, keepdims=True)).astype(BF16)
    acc = jnp.dot(vt_ref[:, tk - ctx_rows:tk], p, preferred_element_type=F32)
    o_t = acc[0:dv] / acc[dv:dv + 1]
    for gi in range(g):
        o_ref[:, gi * dv:(gi + 1) * dv] = o_t[:, gi * tq:(gi + 1) * tq].T.astype(o_ref.dtype)


def _attention_ctx(qt, k, vt, *, dims, g, name):
    b, s_len, l_len = dims["B"], dims["S"], dims["L"]
    hk = k.shape[1]
    dk, tq = qt.shape[2:]
    nk, dvp, tk = vt.shape[2:]
    dv = dvp - V_PAD_ROWS
    cpb = l_len // tq
    lat_tiles = b * s_len // tq
    assert l_len % tq == 0 and l_len <= tk and s_len % l_len == 0
    return pl.pallas_call(
        functools.partial(_attn_ctx_body, g=g, tq=tq, dv=dv, ctx_rows=l_len),
        grid=(b, hk, cpb),
        in_specs=[pl.BlockSpec((g, 1, dk, tq), lambda bb, h, i: (h, lat_tiles + bb * cpb + i, 0, 0)),
                  pl.BlockSpec((None, None, l_len, dk), lambda bb, h, i: (bb, h, s_len // l_len, 0)),
                  pl.BlockSpec((None, None, None, dvp, tk), lambda bb, h, i: (bb, h, nk - 1, 0, 0))],
        out_specs=pl.BlockSpec((tq, g * dv), lambda bb, h, i: (bb * cpb + i, h)),
        out_shape=jax.ShapeDtypeStruct((b * l_len, hk * g * dv), BF16),
        compiler_params=_cparams(3), name=name,
    )(qt, k, vt)


def _na_body(q_ref, k_ref, v_ref, bias_ref, o_ref, *, rows_per_step, n_grid_rows, kh, s_len, l_len, lat_steps,
             with_ctx):
    w = GRID_W
    j = pl.program_id(2)
    kc = k_ref[s_len:s_len + l_len, :]
    vc = v_ref[s_len:s_len + l_len, :]
    nt = (((1,), (1,)), ((), ()))

    def latent_step():
        span = kh + rows_per_step
        r0 = j * rows_per_step
        u0 = jnp.clip(r0 - kh // 2, 0, n_grid_rows - span)
        koff = pl.multiple_of(u0 * w, w)
        kw = k_ref[pl.ds(koff, span * w), :]
        vw = v_ref[pl.ds(koff, span * w), :]
        q = q_ref[...]
        s_loc = lax.dot_general(q, kw, nt, preferred_element_type=F32) + bias_ref[(r0 - u0) // rows_per_step]
        s_ctx = lax.dot_general(q, kc, nt, preferred_element_type=F32)
        m = jnp.maximum(jnp.max(s_loc, axis=-1, keepdims=True), jnp.max(s_ctx, axis=-1, keepdims=True))
        p_loc = jnp.exp(s_loc - m)
        p_ctx = jnp.exp(s_ctx - m)
        denom = jnp.sum(p_loc, axis=-1, keepdims=True) + jnp.sum(p_ctx, axis=-1, keepdims=True)
        o = (jnp.dot(p_loc.astype(BF16), vw, preferred_element_type=F32)
             + jnp.dot(p_ctx.astype(BF16), vc, preferred_element_type=F32))
        o_ref[...] = (o / denom).astype(o_ref.dtype)

    def ctx_step():
        s = lax.dot_general(q_ref[...], kc, nt, preferred_element_type=F32)
        p = jnp.exp(s - jnp.max(s, axis=-1, keepdims=True))
        o = jnp.dot(p.astype(BF16), vc, preferred_element_type=F32)
        o_ref[...] = (o / jnp.sum(p, axis=-1, keepdims=True)).astype(o_ref.dtype)

    if with_ctx:
        pl.when(j < lat_steps)(latent_step)
        pl.when(j >= lat_steps)(ctx_step)
    else:
        latent_step()


def _na_bias(rpb, kh, n_grid_rows, rps):
    span = kh + rps
    col = np.arange(GRID_W)
    start_c = np.clip(col - NA_KW // 2, 0, GRID_W - NA_KW)
    col_mask = (col[None, :] >= start_c[:, None]) & (col[None, :] < start_c[:, None] + NA_KW)
    dc_idx = np.clip(col[None, :] - col[:, None] + NA_KW - 1, 0, 2 * NA_KW - 2)
    n_var = kh // rps + 1
    dr_idx = np.full((n_var, rps, span), -1)
    seen = set()
    for r0 in range(0, n_grid_rows, rps):
        u0 = int(np.clip(r0 - kh // 2, 0, n_grid_rows - span))
        var = (r0 - u0) // rps
        assert (r0 - u0) % rps == 0 and 0 <= var < n_var
        table = np.full((rps, span), -1)
        for qr in range(rps):
            r = r0 + qr
            start = int(np.clip(r - kh // 2, 0, n_grid_rows - kh))
            for kr in range(span):
                if start <= u0 + kr < start + kh:
                    table[qr, kr] = u0 + kr - r + NA_KH - 1
        assert var not in seen or np.array_equal(dr_idx[var], table)
        seen.add(var)
        dr_idx[var] = table
    row_mask = dr_idx >= 0
    oh_r = (dr_idx[..., None] == np.arange(2 * NA_KH - 1)).astype(np.float32)
    oh_c = (dc_idx[:, :, None] == np.arange(2 * NA_KW - 1)[None, None, :]).astype(np.float32)
    bias = jnp.einsum("vrja,hab,qkb->hvrqjk", oh_r, rpb.astype(F32), oh_c, precision=lax.Precision.HIGHEST)
    mask = row_mask[None, :, :, None, :, None] & col_mask[None, None, None, :, None, :]
    bias = jnp.where(mask, bias, NEG_INF)
    return bias.reshape(rpb.shape[0], n_var, rps * GRID_W, span * GRID_W)


def _na(q, k, v, rpb, *, dims, with_ctx):
    b, s, l, hn = dims["B"], dims["S"], dims["L"], dims["HN"]
    r_tot = s // GRID_W
    kh = min(NA_KH, r_tot)
    tq = ROW_TILE
    rps = tq // GRID_W
    assert r_tot >= kh + rps and kh % rps == 0 and (kh // 2) % rps == 0
    bias = _na_bias(rpb, kh, r_tot, rps)
    d = HEAD_DIM
    spb = r_tot // rps
    cpb = l // tq if with_ctx else 0
    lat_tiles = b * spb
    rows = b * (s + (l if with_ctx else 0))

    def qtile(bb, j):
        return jnp.where(j < spb, bb * spb + j, lat_tiles + bb * cpb + (j - spb))

    return pl.pallas_call(
        functools.partial(_na_body, rows_per_step=rps, n_grid_rows=r_tot, kh=kh, s_len=s, l_len=l, lat_steps=spb,
                          with_ctx=with_ctx),
        grid=(b, hn, spb + cpb),
        in_specs=[pl.BlockSpec((None, tq, d), lambda bb, h, j: (h, qtile(bb, j), 0)),
                  pl.BlockSpec((None, None, s + l, d), lambda bb, h, j: (bb, h, 0, 0)),
                  pl.BlockSpec((None, None, s + l, d), lambda bb, h, j: (bb, h, 0, 0)),
                  pl.BlockSpec((None,) + bias.shape[1:], lambda bb, h, j: (h, 0, 0, 0))],
        out_specs=pl.BlockSpec((tq, d), lambda bb, h, j: (qtile(bb, j), h)),
        out_shape=jax.ShapeDtypeStruct((rows, hn * d), BF16),
        compiler_params=_cparams(3), name="na",
    )(q, k, v, bias)


ROUTE_DST_BITS = 16


def _moe_body(te_ref, tv_ref, route_ref, h_ref, w1_ref, w3_ref, w2_ref, out_ref,
              xg, xb, acc_ref, ys, gsem, ssem, *, nf, tm):
    i, f = pl.program_id(0), pl.program_id(1)
    n_tiles = pl.num_programs(0)
    valid = tv_ref[i] > 0
    chunk = tm // nf
    dump1 = out_ref.shape[0] - tm

    def src_row(p):
        return lax.shift_right_logical(route_ref[p], ROUTE_DST_BITS)

    def dst_row(p):
        return route_ref[p] & ((1 << ROUTE_DST_BITS) - 1)

    def gather_copy(r, row):
        return pltpu.make_async_copy(h_ref.at[pl.ds(row, 1)], xg.at[pl.ds(r, 1)], gsem)

    def scatter_copy(slot, r, row):
        return pltpu.make_async_copy(ys.at[slot, pl.ds(r, 1)], out_ref.at[pl.ds(row, 1)], ssem.at[slot])

    def wait_gather():
        pltpu.make_async_copy(h_ref.at[pl.ds(0, tm)], xg, gsem).wait()

    def wait_scatter(slot):
        pltpu.make_async_copy(ys.at[slot], out_ref.at[pl.ds(0, tm)], ssem.at[slot]).wait()

    def issue_chunks():
        g_tile = jnp.minimum(i + 1, n_tiles - 1)
        s_tile = jnp.maximum(i - 1, 0)
        s_slot = (i + 1) % 2
        for r in range(chunk):
            rr = f * chunk + r
            gather_copy(rr, src_row(g_tile * tm + rr)).start()
            row = jnp.where(i > 0, dst_row(s_tile * tm + rr), dump1 + rr)
            scatter_copy(s_slot, rr, row).start()

    @pl.when(jnp.logical_and(i == 0, f == 0))
    def _():
        ys[...] = jnp.zeros_like(ys)

        def go(r, c):
            gather_copy(r, src_row(r)).start()
            return c
        lax.fori_loop(0, tm, go, 0)

    @pl.when(f == 0)
    def _():
        wait_gather()
        xb[...] = xg[...].astype(BF16)
        acc_ref[...] = jnp.zeros_like(acc_ref)

    @pl.when(valid)
    def _():
        issue_chunks()
        x = xb[...]
        h1 = jnp.dot(x, w1_ref[...], preferred_element_type=F32)
        h3 = jnp.dot(x, w3_ref[...], preferred_element_type=F32)
        hm = (_silu(h1) * h3).astype(BF16)
        acc_ref[...] += jnp.dot(hm, w2_ref[...], preferred_element_type=F32)

    @pl.when(jnp.logical_not(valid))
    def _():
        issue_chunks()

    @pl.when(f == nf - 1)
    def _():
        @pl.when(i > 0)
        def _():
            wait_scatter(i % 2)

        ys[i % 2] = acc_ref[...]

        @pl.when(i == n_tiles - 1)
        def _():
            def go(r, c):
                scatter_copy(i % 2, r, dst_row(i * tm + r)).start()
                return c
            lax.fori_loop(0, tm, go, 0)
            wait_scatter(i % 2)
            wait_scatter((i + 1) % 2)
            wait_gather()


def _moe_ffn(h, w1, w3, w2, plan, *, n_out_rows, tf):
    route, tile_expert, tile_valid = plan
    tm = MOE_TILE_M
    p_tot = route.shape[0]
    d = h.shape[1]
    f_dim = w1.shape[-1]
    nf = f_dim // tf
    assert tm % nf == 0 and n_out_rows + 2 * tm <= 1 << ROUTE_DST_BITS and h.shape[0] < 1 << (31 - ROUTE_DST_BITS)

    def fidx(i, f, tv):
        return jnp.where(tv[i] > 0, f, nf - 1)

    return pl.pallas_call(
        functools.partial(_moe_body, nf=nf, tm=tm),
        grid_spec=pltpu.PrefetchScalarGridSpec(
            num_scalar_prefetch=3, grid=(p_tot // tm, nf),
            in_specs=[pl.BlockSpec(memory_space=pl.ANY),
                      pl.BlockSpec((None, d, tf), lambda i, f, te, tv, rt: (te[i], 0, fidx(i, f, tv))),
                      pl.BlockSpec((None, d, tf), lambda i, f, te, tv, rt: (te[i], 0, fidx(i, f, tv))),
                      pl.BlockSpec((None, tf, d), lambda i, f, te, tv, rt: (te[i], fidx(i, f, tv), 0))],
            out_specs=pl.BlockSpec(memory_space=pl.ANY),
            scratch_shapes=[pltpu.VMEM((tm, d), F32), pltpu.VMEM((tm, d), BF16), pltpu.VMEM((tm, d), F32),
                            pltpu.VMEM((2, tm, d), F32), pltpu.SemaphoreType.DMA(()),
                            pltpu.SemaphoreType.DMA((2,))]),
        out_shape=jax.ShapeDtypeStruct((n_out_rows + 2 * tm, d), F32),
        compiler_params=_cparams(2), name="moe_ffn",
    )(tile_expert, tile_valid, route, h, w1, w3, w2)


def _moe_plan(e_idx, n_experts, tm):
    flat_e = e_idx.reshape(-1)
    n_slots = flat_e.shape[0]
    onehot = (flat_e[:, None] == jnp.arange(n_experts)[None, :]).astype(jnp.int32)
    csum = jnp.cumsum(onehot, axis=0)
    rank = jnp.sum(csum * onehot, axis=1) - 1
    counts = csum[-1]
    tiles_per = (counts + tm - 1) // tm
    cum_tiles = jnp.cumsum(tiles_per)
    row_start = (cum_tiles - tiles_per) * tm
    dest = jnp.sum(row_start[None, :] * onehot, axis=1) + rank
    n_tiles = n_slots // tm + n_experts
    n_tiles += n_tiles % 2 == 0
    p_tot = n_tiles * tm
    te = jnp.sum((jnp.arange(n_tiles)[:, None] >= cum_tiles[None, :]).astype(jnp.int32), axis=1)
    tile_valid = (te < n_experts).astype(jnp.int32)
    tile_expert = jnp.minimum(te, n_experts - 1)
    slot = jnp.arange(n_slots, dtype=jnp.int32)
    n_tok = n_slots // TOP_K
    pos = jnp.arange(p_tot, dtype=jnp.int32)
    dump = n_slots + ((pos // tm) % 2) * tm + pos % tm
    word = (slot // TOP_K) * (1 << ROUTE_DST_BITS) + (slot % TOP_K) * n_tok + slot // TOP_K
    route = dump.at[dest].set(word)
    return route, tile_expert, tile_valid


def _rope_tables(n_batch, s_len, l_len):
    t = np.arange(s_len)
    rows, cols = t // GRID_W, t % GRID_W

    def table(width):
        half = width // 2
        quarter = half // 2
        lane = np.arange(width)
        pos = np.where((lane // half)[None, :] == 0, rows[:, None], cols[:, None]).astype(np.float64)
        fi = (lane % half) % quarter
        inv = ROPE_THETA ** (-(2.0 * fi) / half)
        ang = (pos.astype(np.float32) * inv.astype(np.float32)[None, :]).astype(np.float32)
        sign = np.where((lane % half) < quarter, -1.0, 1.0)
        return np.cos(ang).astype(np.float32), (np.sin(ang) * sign[None, :]).astype(np.float32)

    cg, sg = table(HEAD_DIM)
    cm, sm = table(MLA_ROPE)
    cm, sm = np.tile(cm, (1, 2)), np.tile(sm, (1, 2))

    def flat(lat, fill):
        return np.concatenate([np.tile(lat, (n_batch, 1)),
                               np.full((n_batch * l_len, LANES), fill, np.float32)], axis=0)

    return tuple(jnp.asarray(a) for a in (flat(cg, 1.0), flat(sg, 0.0), flat(cm, 1.0), flat(sm, 0.0)))


def _round_up(a, m):
    return (a + m - 1) // m * m


def _pick_tile(n, pref):
    for t in pref:
        if n % t == 0:
            return t
    return n


def kernel(x, c, ctx, c_ctx, w_ada, b_ada, g_pre_mix, g_post_mix, g_pre_ffn, g_post_ffn, w_in, g_q_a, g_kv_a, w_q_up, w_kv_up, g_q_gqa, g_k_gqa, na_rpb, g_grp, w_out, ffn_w1, ffn_w3, ffn_w2, router, moe_w1, moe_w3, moe_w2):
    n_batch, s_len, d = x.shape
    l_len = ctx.shape[1]
    depth = w_ada.shape[0]
    q_lora, kv_lora = g_q_a.shape[-1], g_kv_a.shape[-1]
    dk_m = MLA_NOPE + MLA_ROPE
    hm = w_q_up.shape[-1] // dk_m
    hg = 3 * d // (8 * HEAD_DIM)
    hkv = hg // 3
    hn = d // (4 * HEAD_DIM)
    g_gqa = hg // hkv
    n_lat, n_ctx = n_batch * s_len, n_batch * l_len
    t_tot = n_lat + n_ctx
    assert s_len % MM_TILE_M == 0 and n_ctx % MM_TILE_M == 0 and l_len % ROW_TILE == 0
    assert hm % 2 == 0 and n_batch + 1 <= SUBLANES

    off_qg = q_lora + kv_lora
    off_kvg = off_qg + hg * HEAD_DIM
    off_n = off_kvg + 2 * hkv * HEAD_DIM
    n_main = off_n + 3 * hn * HEAD_DIM
    dims = dict(B=n_batch, S=s_len, L=l_len, HM=hm, HG=hg, HKV=hkv, HN=hn,
                OFF_QG=off_qg, OFF_KVG=off_kvg, OFF_N=off_n)

    tables = _rope_tables(n_batch, s_len, l_len)
    tk_all = s_len + l_len
    tk_m = _pick_tile(tk_all, (768, 512, 256))
    tk_g = tk_m

    mods = _ada(jnp.concatenate([c_ctx[None, :], c], axis=0), w_ada, b_ada)

    lat_tiles, all_tiles = n_lat // ROW_TILE, t_tot // ROW_TILE
    tiles_per_b = s_len // ROW_TILE

    def group_of(i):
        return jnp.where(i < lat_tiles, 1 + i // tiles_per_b, 0)

    def modtab_of(i):
        return mods[i, :n_batch + 1].reshape((n_batch + 1) * 6, 1, d)

    xs = (x.reshape(n_lat, d), ctx.reshape(n_ctx, d))
    modtab = modtab_of(0)
    _, h, _ = _row_call(xs, n_tiles=all_tiles, group_fn=group_of, modtab=modtab,
                        nxt=(g_pre_mix[0], 0, 1, modtab), name="row_in")

    for i in range(depth):
        last = i == depth - 1
        modtab = modtab_of(i)
        m_out = n_lat if last else t_tot
        out_tiles = m_out // ROW_TILE

        wi = w_in[i]
        c_kpe = q_lora + kv_lora
        w_main = jnp.concatenate([wi[:, :c_kpe], wi[:, c_kpe + MLA_ROPE:]], axis=1).astype(BF16)
        w_kpe = jnp.zeros((d, LANES), BF16).at[:, :MLA_ROPE].set(wi[:, c_kpe:c_kpe + MLA_ROPE].astype(BF16))
        u = _matmul(h, w_main, m_rows=t_tot, tn=_pick_tile(n_main, (1024, 512, 256, 128)), name="mm_in")
        kpe = _matmul(h, w_kpe, m_rows=t_tot, tn=LANES, name="mm_kpe")
        wq = w_q_up[i].reshape(q_lora, hm, dk_m)
        wq = jnp.concatenate([wq[:, :, :MLA_NOPE].reshape(q_lora, hm * MLA_NOPE),
                              wq[:, :, MLA_NOPE:].reshape(q_lora, hm * MLA_ROPE)], axis=1).astype(BF16)
        a_cols = _round_up(off_qg, LANES)
        qraw = _matmul(u, wq, m_rows=t_tot, tn=_pick_tile(hm * dk_m, (768, 384, 128)), a_block_cols=a_cols,
                       a_lo=0, a_hi=q_lora, rms_g=g_q_a[i], name="mm_qup")
        kvraw = _matmul(u, w_kv_up[i].astype(BF16), m_rows=t_tot,
                        tn=_pick_tile(hm * (MLA_NOPE + MLA_V), (1024, 768, 512, 256)), a_block_cols=a_cols,
                        a_lo=q_lora, a_hi=q_lora + kv_lora, rms_g=g_kv_a[i], name="mm_kvup")
        qm, km, vm, qg, kg, vg, qn, kn, vn = _prep(u, qraw, kvraw, kpe, tables, g_q_gqa[i], g_k_gqa[i],
                                                   dims=dims, tk_m=tk_m, tk_g=tk_g)

        q_tiles_per_b = s_len // ROW_TILE
        o_m = (_attention(qm, km, vm, dims=dims, g=1, tpi=MLA_TILES_PER_ITEM,
                          nt=q_tiles_per_b // MLA_TILES_PER_ITEM, unroll=8, name="attn_mla"),)
        o_g = (_attention(qg, kg, vg, dims=dims, g=g_gqa, tpi=1, nt=_pick_tile(q_tiles_per_b, (16, 8, 4, 2)),
                          unroll=1, name="attn_gqa"),)
        if not last:
            o_m += (_attention_ctx(qm, km, vm, dims=dims, g=1, name="attn_mla_ctx"),)
            o_g += (_attention_ctx(qg, kg, vg, dims=dims, g=g_gqa, name="attn_gqa_ctx"),)
        o_n = (_na(qn, kn, vn, na_rpb[i], dims=dims, with_ctx=not last),)

        y = _matmul_groups((o_m, o_g, o_n), g_grp[i], w_out[i].astype(BF16), m_rows=m_out,
                           tn=_pick_tile(d, (1024, 512, 256)), name="mm_out")

        j = i // 2
        moe = i % 2 == 1
        res = ([(y, 0)], 2, g_post_mix[i], None)
        nxt = (g_pre_ffn[i], 3, 4, modtab)
        if moe:
            x_all, hf, rt = _row_call(xs, n_tiles=out_tiles, group_fn=group_of, modtab=modtab, res=res, nxt=nxt,
                                      router=router[j], h_dtype=F32, name="row_mix")
            n_experts = router.shape[-1]
            e_idx = rt[:, TOP_K:2 * TOP_K].astype(jnp.int32)
            plan = _moe_plan(e_idx, n_experts, MOE_TILE_M)
            f_weights = rt
            f_out = _moe_ffn(hf, moe_w1[j].astype(BF16), moe_w3[j].astype(BF16), moe_w2[j].astype(BF16), plan,
                             n_out_rows=TOP_K * m_out, tf=_pick_tile(moe_w1.shape[-1], (256, 128)))
            f_blocks = [(f_out, k * out_tiles) for k in range(TOP_K)]
        else:
            x_all, h, _ = _row_call(xs, n_tiles=out_tiles, group_fn=group_of, modtab=modtab, res=res, nxt=nxt,
                                    name="row_mix")
            f_dim = ffn_w1.shape[-1]
            f_pad = _round_up(f_dim, 1024)
            pad = ((0, 0), (0, f_pad - f_dim))
            w1 = jnp.pad(ffn_w1[j].astype(BF16), pad)
            w3 = jnp.pad(ffn_w3[j].astype(BF16), pad)
            w2 = jnp.pad(ffn_w2[j].astype(BF16), (pad[1], pad[0]))
            gact = _glu(h, w1, w3, m_rows=m_out, tf=1024, name="ffn_glu")
            f_out = _matmul(gact, w2, m_rows=m_out, tn=_pick_tile(d, (512, 256)), name="ffn_down")
            f_blocks = [(f_out, 0)]
            f_weights = None
        xs = (x_all,)

        res = (f_blocks, 5, g_post_ffn[i], f_weights)
        if last:
            x_all, _, _ = _row_call(xs, n_tiles=out_tiles, group_fn=group_of, modtab=modtab, res=res, name="row_ffn")
        else:
            x_all, h, _ = _row_call(xs, n_tiles=out_tiles, group_fn=group_of, modtab=modtab, res=res,
                                    nxt=(g_pre_mix[i + 1], 0, 1, modtab_of(i + 1)), name="row_ffn")
        xs = (x_all,)
    return xs[0][:n_lat].reshape(n_batch, s_len, d)
```

```python
import functools
import math

import jax
import jax.numpy as jnp
import numpy as np
from jax import lax
from jax.experimental import pallas as pl
from jax.experimental.pallas import tpu as pltpu

F32 = jnp.float32
BF16 = jnp.bfloat16

GRID_W = 64
HEAD_DIM = 128
ROPE_THETA = 10000.0
EPS = 1e-6
NEG_INF = -1e30
MLA_NOPE = 128
MLA_ROPE = 64
MLA_V = 128
NA_KH = 8
NA_KW = 16
TOP_K = 2
LOG2E = math.log2(math.e)

LANES = 128
SUBLANES = 8
BF16_SUBLANES = 16
VMEM_LIMIT_BYTES = 56 * 1024 * 1024

ROW_TILE = 256
MM_TILE_M = 512
MOE_TILE_M = 512
V_PAD_ROWS = BF16_SUBLANES
MLA_TILES_PER_ITEM = 1


def _cparams(n_axes):
    return pltpu.CompilerParams(dimension_semantics=("arbitrary",) * n_axes,
                                vmem_limit_bytes=VMEM_LIMIT_BYTES)


def _rms(v, g):
    ms = jnp.mean(v * v, axis=-1, keepdims=True)
    return v * lax.rsqrt(ms + EPS) * g


def _silu(v):
    return v * jax.nn.sigmoid(v)


ADA_K_CHUNK = 64


def _ada_body(ct_ref, w_ref, b_ref, o_ref, sb_ref, *, n_rows):
    d, tn = w_ref.shape

    @pl.when(jnp.logical_and(pl.program_id(0) == 0, pl.program_id(1) == 0))
    def _():
        s = _silu(ct_ref[...])
        for r in range(n_rows):
            sb_ref[r] = jnp.broadcast_to(s[:, r:r + 1], (d, LANES))

    def step(i, accs):
        off = pl.multiple_of(i * ADA_K_CHUNK, ADA_K_CHUNK)
        w = w_ref[pl.ds(off, ADA_K_CHUNK), :]
        out = []
        for r in range(n_rows):
            sb = sb_ref[r, pl.ds(off, ADA_K_CHUNK), :]
            prod = w * jnp.concatenate([sb] * (tn // LANES), axis=1)
            part = prod[0:SUBLANES]
            for j in range(1, ADA_K_CHUNK // SUBLANES):
                part = part + prod[j * SUBLANES:(j + 1) * SUBLANES]
            out.append(accs[r] + part)
        return tuple(out)

    accs = lax.fori_loop(0, d // ADA_K_CHUNK, step,
                         tuple(jnp.zeros((SUBLANES, tn), F32) for _ in range(n_rows)), unroll=2)
    rows = [jnp.sum(a, axis=0, keepdims=True) for a in accs]
    rows.append(jnp.zeros((SUBLANES - n_rows, tn), F32))
    o_ref[...] = jnp.concatenate(rows, axis=0) + b_ref[...]


def _ada(cond, w_ada, b_ada):
    n_rows, d = cond.shape
    depth, _, n = w_ada.shape
    tn = 512
    ct = jnp.zeros((d, SUBLANES), F32).at[:, :n_rows].set(cond.T)
    return pl.pallas_call(
        functools.partial(_ada_body, n_rows=n_rows),
        grid=(depth, n // tn),
        in_specs=[pl.BlockSpec((d, SUBLANES), lambda l, j: (0, 0)),
                  pl.BlockSpec((None, d, tn), lambda l, j: (l, 0, j)),
                  pl.BlockSpec((None, 1, tn), lambda l, j: (l, 0, j))],
        out_specs=pl.BlockSpec((None, SUBLANES, tn), lambda l, j: (l, 0, j)),
        out_shape=jax.ShapeDtypeStruct((depth, SUBLANES, n), F32),
        scratch_shapes=[pltpu.VMEM((n_rows, d, LANES), F32)],
        compiler_params=_cparams(2),
        name="ada",
    )(ct, w_ada, b_ada.reshape(depth, 1, n))


def _row_body(*refs, n_x, x_split, n_y, y_weighted, has_next, router, n_experts):
    it = iter(refs)
    x_refs = [next(it) for _ in range(n_x)]
    has_res = n_y > 0
    if has_res:
        y_refs = [next(it) for _ in range(n_y)]
        if y_weighted:
            yw_ref = next(it)
        gate_ref, gpost_ref = next(it), next(it)
    if has_next:
        gpre_ref, shift_ref, scale_ref = next(it), next(it), next(it)
    if router:
        rhi_ref, rlo_ref = next(it), next(it)
    if has_res:
        xo_ref = next(it)
    if has_next:
        h_ref = next(it)
    if router:
        rt_ref = next(it)

    if n_x == 1:
        x = x_refs[0][...]
    else:
        x = jnp.where(pl.program_id(0) < x_split, x_refs[0][...], x_refs[1][...])
    if has_res:
        if y_weighted:
            y = yw_ref[:, 0:1] * y_refs[0][...].astype(F32)
            for k in range(1, n_y):
                y = y + yw_ref[:, k:k + 1] * y_refs[k][...].astype(F32)
        else:
            y = y_refs[0][...].astype(F32)
            for y_ref in y_refs[1:]:
                y = y + y_ref[...].astype(F32)
        x = x + gate_ref[...] * _rms(y, gpost_ref[...])
        xo_ref[...] = x
    if has_next:
        h = _rms(x, gpre_ref[...]) * (1.0 + scale_ref[...]) + shift_ref[...]
        h_ref[...] = h.astype(h_ref.dtype)
    if router:
        hi = h.astype(BF16)
        lo = (h - hi.astype(F32)).astype(BF16)
        logits = (jnp.dot(hi, rhi_ref[...], preferred_element_type=F32)
                  + jnp.dot(hi, rlo_ref[...], preferred_element_type=F32)
                  + jnp.dot(lo, rhi_ref[...], preferred_element_type=F32))
        lane = lax.broadcasted_iota(jnp.int32, logits.shape, 1)
        lg = jnp.where(lane < n_experts, logits, -jnp.inf)
        v1 = jnp.max(lg, axis=-1, keepdims=True)
        i1 = jnp.min(jnp.where(lg == v1, lane, LANES), axis=-1, keepdims=True)
        lg2 = jnp.where(lane == i1, -jnp.inf, lg)
        v2 = jnp.max(lg2, axis=-1, keepdims=True)
        i2 = jnp.min(jnp.where(lg2 == v2, lane, LANES), axis=-1, keepdims=True)
        e = jnp.exp(v2 - v1)
        g1 = 1.0 / (1.0 + e)
        g2 = e / (1.0 + e)
        rt_ref[...] = jnp.where(lane == 0, g1,
                                jnp.where(lane == 1, g2,
                                          jnp.where(lane == 2, i1.astype(F32),
                                                    jnp.where(lane == 3, i2.astype(F32), 0.0))))


def _row_call(xs, *, n_tiles, group_fn, modtab, res=None, nxt=None, router=None, h_dtype=BF16, name="row"):
    tm = ROW_TILE
    d = xs[0].shape[-1]
    has_res, has_next = res is not None, nxt is not None

    def modspec(k):
        return pl.BlockSpec((None, 1, d), lambda i: (group_fn(i) * 6 + k, 0, 0))

    vecspec = pl.BlockSpec((1, d), lambda i: (0, 0))
    x_split = 0
    if len(xs) == 1:
        ins, in_specs = [xs[0]], [pl.BlockSpec((tm, d), lambda i: (i, 0))]
    else:
        x_split = xs[0].shape[0] // tm
        ins = list(xs)
        in_specs = [pl.BlockSpec((tm, d), lambda i: (jnp.minimum(i, x_split - 1), 0)),
                    pl.BlockSpec((tm, d), lambda i: (jnp.maximum(i - x_split, 0), 0))]
    n_y = 0
    y_weights = None
    if has_res:
        y_blocks, gate_k, g_post, y_weights = res
        n_y = len(y_blocks)
        for arr, tile0 in y_blocks:
            ins.append(arr)
            in_specs.append(pl.BlockSpec((tm, d), lambda i, tile0=tile0: (i + tile0, 0)))
        if y_weights is not None:
            ins.append(y_weights)
            in_specs.append(pl.BlockSpec((tm, y_weights.shape[1]), lambda i: (i, 0)))
        ins += [modtab, g_post.reshape(1, d)]
        in_specs += [modspec(gate_k), vecspec]
    if has_next:
        g_pre, shift_k, scale_k, modtab_n = nxt
        ins += [g_pre.reshape(1, d), modtab_n, modtab_n]
        in_specs += [vecspec, modspec(shift_k), modspec(scale_k)]
    n_experts = 0
    if router is not None:
        n_experts = router.shape[-1]
        rp = jnp.zeros((d, LANES), F32).at[:, :n_experts].set(router)
        rhi = rp.astype(BF16)
        rlo = (rp - rhi.astype(F32)).astype(BF16)
        ins += [rhi, rlo]
        in_specs += [pl.BlockSpec((d, LANES), lambda i: (0, 0))] * 2
    out_shape, out_specs = [], []
    rowspec = pl.BlockSpec((tm, d), lambda i: (i, 0))
    if has_res:
        out_shape.append(jax.ShapeDtypeStruct((n_tiles * tm, d), F32))
        out_specs.append(rowspec)
    if has_next:
        out_shape.append(jax.ShapeDtypeStruct((n_tiles * tm, d), h_dtype))
        out_specs.append(rowspec)
    if router is not None:
        out_shape.append(jax.ShapeDtypeStruct((n_tiles * tm, LANES), F32))
        out_specs.append(pl.BlockSpec((tm, LANES), lambda i: (i, 0)))
    outs = pl.pallas_call(
        functools.partial(_row_body, n_x=len(xs), x_split=x_split, n_y=n_y, y_weighted=y_weights is not None,
                          has_next=has_next,
                          router=router is not None, n_experts=n_experts),
        grid=(n_tiles,), in_specs=in_specs, out_specs=out_specs, out_shape=out_shape,
        compiler_params=_cparams(1), name=name,
    )(*ins)
    outs = list(outs)
    x_new = outs.pop(0) if has_res else None
    h = outs.pop(0) if has_next else None
    rt = outs.pop(0) if router is not None else None
    return x_new, h, rt


def _mm_body(*refs, a_lo, a_hi, rms):
    if rms:
        a_ref, g_ref, w_ref, o_ref = refs
    else:
        a_ref, w_ref, o_ref = refs
    a = a_ref[:, a_lo:a_hi]
    if rms:
        a = _rms(a.astype(F32), g_ref[...]).astype(BF16)
    o_ref[...] = jnp.dot(a, w_ref[...], preferred_element_type=F32).astype(o_ref.dtype)


def _matmul(a, w, *, m_rows, tn, out_dtype=BF16, a_block_cols=None, a_lo=0, a_hi=None, rms_g=None, name="mm"):
    tm = MM_TILE_M
    k, n = w.shape
    if a_block_cols is None:
        a_block_cols = a.shape[1]
    if a_hi is None:
        a_hi = a_lo + k
    ins, in_specs = [a], [pl.BlockSpec((tm, a_block_cols), lambda j, i: (i, 0))]
    if rms_g is not None:
        ins.append(rms_g.reshape(1, k))
        in_specs.append(pl.BlockSpec((1, k), lambda j, i: (0, 0)))
    ins.append(w)
    in_specs.append(pl.BlockSpec((k, tn), lambda j, i: (0, j)))
    return pl.pallas_call(
        functools.partial(_mm_body, a_lo=a_lo, a_hi=a_hi, rms=rms_g is not None),
        grid=(n // tn, m_rows // tm), in_specs=in_specs,
        out_specs=pl.BlockSpec((tm, tn), lambda j, i: (i, j)),
        out_shape=jax.ShapeDtypeStruct((m_rows, n), out_dtype),
        compiler_params=_cparams(2), name=name,
    )(*ins)


def _mm_groups_body(*refs, parts_per_group, split_tile):
    n_a = sum(parts_per_group)
    a_refs = refs[:n_a]
    g_ref, w_ref, o_ref = refs[n_a:]
    parts, lo, pos = [], 0, 0
    for n_parts in parts_per_group:
        if n_parts == 1:
            a = a_refs[pos][...]
        else:
            a = jnp.where(pl.program_id(1) < split_tile, a_refs[pos][...], a_refs[pos + 1][...])
        pos += n_parts
        width = a.shape[1]
        parts.append(_rms(a.astype(F32), g_ref[:, lo:lo + width]).astype(BF16))
        lo += width
    a = jnp.concatenate(parts, axis=1)
    o_ref[...] = jnp.dot(a, w_ref[...], preferred_element_type=F32).astype(o_ref.dtype)


def _matmul_groups(groups, g, w, *, m_rows, tn, name):
    tm = MM_TILE_M
    k, n = w.shape
    split_tile = 0
    ins, in_specs = [], []
    for grp in groups:
        if len(grp) == 1:
            in_specs.append(pl.BlockSpec((tm, grp[0].shape[1]), lambda j, i: (i, 0)))
        else:
            split_tile = grp[0].shape[0] // tm
            st = split_tile
            in_specs.append(pl.BlockSpec((tm, grp[0].shape[1]), lambda j, i, st=st: (jnp.minimum(i, st - 1), 0)))
            in_specs.append(pl.BlockSpec((tm, grp[1].shape[1]), lambda j, i, st=st: (jnp.maximum(i - st, 0), 0)))
        ins += list(grp)
    in_specs += [pl.BlockSpec((1, k), lambda j, i: (0, 0)), pl.BlockSpec((k, tn), lambda j, i: (0, j))]
    return pl.pallas_call(
        functools.partial(_mm_groups_body, parts_per_group=tuple(len(grp) for grp in groups), split_tile=split_tile),
        grid=(n // tn, m_rows // tm), in_specs=in_specs,
        out_specs=pl.BlockSpec((tm, tn), lambda j, i: (i, j)),
        out_shape=jax.ShapeDtypeStruct((m_rows, n), BF16),
        compiler_params=_cparams(2), name=name,
    )(*ins, g.reshape(1, k), w)


def _glu_body(a_ref, w1_ref, w3_ref, o_ref):
    a = a_ref[...]
    h1 = jnp.dot(a, w1_ref[...], preferred_element_type=F32)
    h3 = jnp.dot(a, w3_ref[...], preferred_element_type=F32)
    o_ref[...] = (_silu(h1) * h3).astype(o_ref.dtype)


def _glu(a, w1, w3, *, m_rows, tf, name="glu"):
    tm = MM_TILE_M
    k, f = w1.shape
    return pl.pallas_call(
        _glu_body, grid=(f // tf, m_rows // tm),
        in_specs=[pl.BlockSpec((tm, k), lambda j, i: (i, 0)),
                  pl.BlockSpec((k, tf), lambda j, i: (0, j)),
                  pl.BlockSpec((k, tf), lambda j, i: (0, j))],
        out_specs=pl.BlockSpec((tm, tf), lambda j, i: (i, j)),
        out_shape=jax.ShapeDtypeStruct((m_rows, f), BF16),
        compiler_params=_cparams(2), name=name,
    )(a, w1, w3)


def _rope(x, c, s, half):
    lane = lax.broadcasted_iota(jnp.int32, x.shape, 1)
    first = (lane % (2 * half)) < half
    swapped = jnp.where(first, pltpu.roll(x, LANES - half, 1), pltpu.roll(x, half, 1))
    return x * c + swapped * s


def _prep_body(u_ref, qraw_ref, kvraw_ref, kpe_ref, cg_ref, sg_ref, cm_ref, sm_ref, gq_ref, gk_ref,
               qm_o, km_o, vm_o, qg_o, kg_o, vg_o, qn_o, kn_o, vn_o,
               *, hm, hg, hkv, hn, off_qg, off_kvg, off_n):
    cg, sg, cm, sm = cg_ref[...], sg_ref[...], cm_ref[...], sm_ref[...]
    scale_m = (MLA_NOPE + MLA_ROPE) ** -0.5 * LOG2E
    scale_g = HEAD_DIM ** -0.5 * LOG2E
    scale_n = HEAD_DIM ** -0.5
    d = HEAD_DIM
    tm = u_ref.shape[0]
    ones_rows = jnp.where(lax.broadcasted_iota(jnp.int32, (V_PAD_ROWS, tm), 0) == 0, 1.0, 0.0).astype(BF16)

    def t_bf16(v):
        return v.astype(F32).T.astype(BF16)

    kpe = _rope(kpe_ref[...].astype(F32), cm, sm, MLA_ROPE // 4)[:, :MLA_ROPE].astype(BF16)
    pe0 = hm * MLA_NOPE
    for h in range(hm):
        qm_o[h, 0:MLA_NOPE, :] = t_bf16(qraw_ref[:, h * MLA_NOPE:(h + 1) * MLA_NOPE].astype(F32) * scale_m)
        km_o[h, :, 0:MLA_NOPE] = kvraw_ref[:, h * 2 * d:h * 2 * d + MLA_NOPE]
        km_o[h, :, MLA_NOPE:MLA_NOPE + MLA_ROPE] = kpe
        vm_o[h, 0:MLA_V, :] = t_bf16(kvraw_ref[:, h * 2 * d + MLA_NOPE:(h + 1) * 2 * d])
        vm_o[h, MLA_V:MLA_V + V_PAD_ROWS, :] = ones_rows
    for j in range(hm // 2):
        pe = _rope(qraw_ref[:, pe0 + j * LANES:pe0 + (j + 1) * LANES].astype(F32), cm, sm, MLA_ROPE // 4)
        pe_t = t_bf16(pe * scale_m)
        qm_o[2 * j, MLA_NOPE:MLA_NOPE + MLA_ROPE, :] = pe_t[:MLA_ROPE]
        qm_o[2 * j + 1, MLA_NOPE:MLA_NOPE + MLA_ROPE, :] = pe_t[MLA_ROPE:]
    for h in range(hg):
        q = _rms(u_ref[:, off_qg + h * d:off_qg + (h + 1) * d].astype(F32), gq_ref[...])
        qg_o[h] = t_bf16(_rope(q, cg, sg, d // 4) * scale_g)
    for h in range(hkv):
        k = _rms(u_ref[:, off_kvg + h * d:off_kvg + (h + 1) * d].astype(F32), gk_ref[...])
        kg_o[h] = _rope(k, cg, sg, d // 4).astype(BF16)
        vg_o[h, 0:d, :] = t_bf16(u_ref[:, off_kvg + (hkv + h) * d:off_kvg + (hkv + h + 1) * d])
        vg_o[h, d:d + V_PAD_ROWS, :] = ones_rows
    for h in range(hn):
        qn_o[h] = (u_ref[:, off_n + h * d:off_n + (h + 1) * d].astype(F32) * scale_n).astype(BF16)
        kn_o[h] = u_ref[:, off_n + (hn + h) * d:off_n + (hn + h + 1) * d]
        vn_o[h] = u_ref[:, off_n + (2 * hn + h) * d:off_n + (2 * hn + h + 1) * d]


def _prep(u, qraw, kvraw, kpe, tables, g_q, g_k, *, dims, tk_m, tk_g):
    tm = ROW_TILE
    t_tot = u.shape[0]
    b, s, l = dims["B"], dims["S"], dims["L"]
    hm, hg, hkv, hn = dims["HM"], dims["HG"], dims["HKV"], dims["HN"]
    tk_len = s + l
    n_lat = b * s // tm
    spb, lpb = s // tm, l // tm

    def kmap(i):
        lat = i < n_lat
        j = i - n_lat
        bb = jnp.where(lat, i // spb, j // lpb)
        pos = jnp.where(lat, i % spb, spb + j % lpb)
        return bb, pos

    def kspec(h, dk):
        def im(i):
            bb, pos = kmap(i)
            return (bb, 0, pos, 0)
        return pl.BlockSpec((None, h, tm, dk), im)

    def vtspec(h, dv, tk):
        per = tk // tm

        def im(i):
            bb, pos = kmap(i)
            return (bb, 0, pos // per, 0, pos % per)
        return pl.BlockSpec((None, h, None, dv + V_PAD_ROWS, tm), im)

    def qtspec(h, dk):
        return pl.BlockSpec((h, None, dk, tm), lambda i: (0, i, 0, 0))

    def qspec(h, dk):
        return pl.BlockSpec((h, tm, dk), lambda i: (0, i, 0))

    def full(arr):
        return pl.BlockSpec((tm, arr.shape[1]), lambda i: (i, 0))

    dk_m = MLA_NOPE + MLA_ROPE
    d = HEAD_DIM
    out_shape = [
        jax.ShapeDtypeStruct((hm, t_tot // tm, dk_m, tm), BF16), jax.ShapeDtypeStruct((b, hm, tk_len, dk_m), BF16),
        jax.ShapeDtypeStruct((b, hm, tk_len // tk_m, MLA_V + V_PAD_ROWS, tk_m), BF16),
        jax.ShapeDtypeStruct((hg, t_tot // tm, d, tm), BF16), jax.ShapeDtypeStruct((b, hkv, tk_len, d), BF16),
        jax.ShapeDtypeStruct((b, hkv, tk_len // tk_g, d + V_PAD_ROWS, tk_g), BF16),
        jax.ShapeDtypeStruct((hn, t_tot, d), BF16), jax.ShapeDtypeStruct((b, hn, tk_len, d), BF16),
        jax.ShapeDtypeStruct((b, hn, tk_len, d), BF16),
    ]
    out_specs = [qtspec(hm, dk_m), kspec(hm, dk_m), vtspec(hm, MLA_V, tk_m),
                 qtspec(hg, d), kspec(hkv, d), vtspec(hkv, d, tk_g),
                 qspec(hn, d), kspec(hn, d), kspec(hn, d)]
    vec = pl.BlockSpec((1, d), lambda i: (0, 0))
    return pl.pallas_call(
        functools.partial(_prep_body, hm=hm, hg=hg, hkv=hkv, hn=hn,
                          off_qg=dims["OFF_QG"], off_kvg=dims["OFF_KVG"], off_n=dims["OFF_N"]),
        grid=(t_tot // tm,),
        in_specs=[full(u), full(qraw), full(kvraw), full(kpe)] + [full(t) for t in tables] + [vec, vec],
        out_specs=out_specs, out_shape=out_shape,
        compiler_params=_cparams(1), name="prep",
    )(u, qraw, kvraw, kpe, *tables, g_q.reshape(1, d), g_k.reshape(1, d))


M_INIT = -1e30


def _q_tile(qt_ref, g, tpi, t):
    parts = [qt_ref[gi, t * tpi + j] for gi in range(g) for j in range(tpi)]
    return parts[0] if len(parts) == 1 else jnp.concatenate(parts, axis=1)


def _attn_body(qt_ref, k_ref, vt_ref, o_ref, s0, s1, p0, p1, a0, a1, m_sc, acc_sc,
               *, g, tpi, tq, tk, nk, dv, nt, unroll):
    s_buf, p_buf, a_buf = (s0, s1), (p0, p1), (a0, a1)
    n_items = nt * nk

    def split(it):
        if isinstance(it, int):
            return it // nk, it % nk
        return lax.div(it, jnp.int32(nk)), lax.rem(it, jnp.int32(nk))

    def scores(it):
        t, c = split(it)
        off = c * tk if isinstance(c, int) else pl.multiple_of(c * tk, tk)
        return jnp.dot(k_ref[pl.ds(off, tk), :], _q_tile(qt_ref, g, tpi, t), preferred_element_type=F32)

    def softmax(it, slot):
        t, _ = split(it)
        s = s_buf[slot][...]
        m_prev = m_sc[t]
        m_new = jnp.maximum(m_prev, jnp.max(s, axis=0, keepdims=True))
        p_buf[slot][...] = jnp.exp2(s - m_new).astype(BF16)
        a_buf[slot][...] = jnp.exp2(m_prev - m_new)
        m_sc[t] = m_new

    def values(it, slot):
        t, c = split(it)
        acc_sc[t] = a_buf[slot][...] * acc_sc[t] + jnp.dot(vt_ref[c], p_buf[slot][...], preferred_element_type=F32)

    def stage(it, slot):
        s_buf[1 - slot][...] = scores(it + 1)
        softmax(it, slot)
        values(it - 1, 1 - slot)

    m_sc[...] = jnp.full_like(m_sc, M_INIT)
    acc_sc[...] = jnp.zeros_like(acc_sc)
    s_buf[0][...] = scores(0)
    softmax(0, 0)
    if n_items > 1:
        s_buf[1][...] = scores(1)
        n_steady = n_items - 2

        def pair(u, carry):
            it = 2 * u + 1
            stage(it, 1)
            stage(it + 1, 0)
            return carry

        lax.fori_loop(0, n_steady // 2, pair, 0, unroll=unroll)
        if n_steady % 2:
            stage(n_items - 2, 1)
        softmax(n_items - 1, (n_items - 1) % 2)
        values(n_items - 2, (n_items - 2) % 2)
    values(n_items - 1, (n_items - 1) % 2)

    def finish(t, carry):
        acc = acc_sc[t]
        o_t = acc[0:dv] / acc[dv:dv + 1]
        for gi in range(g):
            for j in range(tpi):
                row = pl.multiple_of((t * tpi + j) * tq, tq)
                col = (gi * tpi + j) * tq
                o_ref[pl.ds(row, tq), gi * dv:(gi + 1) * dv] = o_t[:, col:col + tq].T.astype(o_ref.dtype)
        return carry

    lax.fori_loop(0, nt, finish, 0)


def _attention(qt, k, vt, *, dims, g, tpi, nt, unroll, name):
    b, s_len, l_len = dims["B"], dims["S"], dims["L"]
    hk = k.shape[1]
    dk, tq = qt.shape[2:]
    nk, dvp, tk = vt.shape[2:]
    dv = dvp - V_PAD_ROWS
    steps_per_b = s_len // (nt * tpi * tq)
    n = g * tpi * tq
    return pl.pallas_call(
        functools.partial(_attn_body, g=g, tpi=tpi, tq=tq, tk=tk, nk=nk, dv=dv, nt=nt, unroll=unroll),
        grid=(b, hk, steps_per_b),
        in_specs=[pl.BlockSpec((g, nt * tpi, dk, tq), lambda bb, h, i: (h, bb * steps_per_b + i, 0, 0)),
                  pl.BlockSpec((None, None, s_len + l_len, dk), lambda bb, h, i: (bb, h, 0, 0)),
                  pl.BlockSpec((None, None, nk, dvp, tk), lambda bb, h, i: (bb, h, 0, 0, 0))],
        out_specs=pl.BlockSpec((nt * tpi * tq, g * dv), lambda bb, h, i: (bb * steps_per_b + i, h)),
        out_shape=jax.ShapeDtypeStruct((b * s_len, hk * g * dv), BF16),
        scratch_shapes=[pltpu.VMEM((tk, n), F32), pltpu.VMEM((tk, n), F32),
                        pltpu.VMEM((tk, n), BF16), pltpu.VMEM((tk, n), BF16),
                        pltpu.VMEM((1, n), F32), pltpu.VMEM((1, n), F32),
                        pltpu.VMEM((nt, 1, n), F32), pltpu.VMEM((nt, dvp, n), F32)],
        compiler_params=_cparams(3), name=name,
    )(qt, k, vt)


def _attn_ctx_body(qt_ref, k_ref, vt_ref, o_ref, *, g, tq, dv, ctx_rows):
    qt = _q_tile(qt_ref, g, 1, 0)
    tk = vt_ref.shape[-1]
    s = jnp.dot(k_ref[...], qt, preferred_element_type=F32)
    p = jnp.exp2(s - jnp.max(s, axis=0, keepdims=True)).astype(BF16)
    acc = jnp.dot(vt_ref[:, tk - ctx_rows:tk], p, preferred_element_type=F32)
    o_t = acc[0:dv] / acc[dv:dv + 1]
    for gi in range(g):
        o_ref[:, gi * dv:(gi + 1) * dv] = o_t[:, gi * tq:(gi + 1) * tq].T.astype(o_ref.dtype)


def _attention_ctx(qt, k, vt, *, dims, g, name):
    b, s_len, l_len = dims["B"], dims["S"], dims["L"]
    hk = k.shape[1]
    dk, tq = qt.shape[2:]
    nk, dvp, tk = vt.shape[2:]
    dv = dvp - V_PAD_ROWS
    cpb = l_len // tq
    lat_tiles = b * s_len // tq
    assert l_len % tq == 0 and l_len <= tk and s_len % l_len == 0
    return pl.pallas_call(
        functools.partial(_attn_ctx_body, g=g, tq=tq, dv=dv, ctx_rows=l_len),
        grid=(b, hk, cpb),
        in_specs=[pl.BlockSpec((g, 1, dk, tq), lambda bb, h, i: (h, lat_tiles + bb * cpb + i, 0, 0)),
                  pl.BlockSpec((None, None, l_len, dk), lambda bb, h, i: (bb, h, s_len // l_len, 0)),
                  pl.BlockSpec((None, None, None, dvp, tk), lambda bb, h, i: (bb, h, nk - 1, 0, 0))],
        out_specs=pl.BlockSpec((tq, g * dv), lambda bb, h, i: (bb * cpb + i, h)),
        out_shape=jax.ShapeDtypeStruct((b * l_len, hk * g * dv), BF16),
        compiler_params=_cparams(3), name=name,
    )(qt, k, vt)


def _na_body(q_ref, k_ref, v_ref, bias_ref, o_ref, *, rows_per_step, n_grid_rows, kh, s_len, l_len, lat_steps,
             with_ctx):
    w = GRID_W
    j = pl.program_id(2)
    kc = k_ref[s_len:s_len + l_len, :]
    vc = v_ref[s_len:s_len + l_len, :]
    nt = (((1,), (1,)), ((), ()))

    def latent_step():
        span = kh + rows_per_step
        r0 = j * rows_per_step
        u0 = jnp.clip(r0 - kh // 2, 0, n_grid_rows - span)
        koff = pl.multiple_of(u0 * w, w)
        kw = k_ref[pl.ds(koff, span * w), :]
        vw = v_ref[pl.ds(koff, span * w), :]
        q = q_ref[...]
        s_loc = lax.dot_general(q, kw, nt, preferred_element_type=F32) + bias_ref[(r0 - u0) // rows_per_step]
        s_ctx = lax.dot_general(q, kc, nt, preferred_element_type=F32)
        m = jnp.maximum(jnp.max(s_loc, axis=-1, keepdims=True), jnp.max(s_ctx, axis=-1, keepdims=True))
        p_loc = jnp.exp(s_loc - m)
        p_ctx = jnp.exp(s_ctx - m)
        denom = jnp.sum(p_loc, axis=-1, keepdims=True) + jnp.sum(p_ctx, axis=-1, keepdims=True)
        o = (jnp.dot(p_loc.astype(BF16), vw, preferred_element_type=F32)
             + jnp.dot(p_ctx.astype(BF16), vc, preferred_element_type=F32))
        o_ref[...] = (o / denom).astype(o_ref.dtype)

    def ctx_step():
        s = lax.dot_general(q_ref[...], kc, nt, preferred_element_type=F32)
        p = jnp.exp(s - jnp.max(s, axis=-1, keepdims=True))
        o = jnp.dot(p.astype(BF16), vc, preferred_element_type=F32)
        o_ref[...] = (o / jnp.sum(p, axis=-1, keepdims=True)).astype(o_ref.dtype)

    if with_ctx:
        pl.when(j < lat_steps)(latent_step)
        pl.when(j >= lat_steps)(ctx_step)
    else:
        latent_step()


def _na_bias(rpb, kh, n_grid_rows, rps):
    span = kh + rps
    col = np.arange(GRID_W)
    start_c = np.clip(col - NA_KW // 2, 0, GRID_W - NA_KW)
    col_mask = (col[None, :] >= start_c[:, None]) & (col[None, :] < start_c[:, None] + NA_KW)
    dc_idx = np.clip(col[None, :] - col[:, None] + NA_KW - 1, 0, 2 * NA_KW - 2)
    n_var = kh // rps + 1
    dr_idx = np.full((n_var, rps, span), -1)
    seen = set()
    for r0 in range(0, n_grid_rows, rps):
        u0 = int(np.clip(r0 - kh // 2, 0, n_grid_rows - span))
        var = (r0 - u0) // rps
        assert (r0 - u0) % rps == 0 and 0 <= var < n_var
        table = np.full((rps, span), -1)
        for qr in range(rps):
            r = r0 + qr
            start = int(np.clip(r - kh // 2, 0, n_grid_rows - kh))
            for kr in range(span):
                if start <= u0 + kr < start + kh:
                    table[qr, kr] = u0 + kr - r + NA_KH - 1
        assert var not in seen or np.array_equal(dr_idx[var], table)
        seen.add(var)
        dr_idx[var] = table
    row_mask = dr_idx >= 0
    oh_r = (dr_idx[..., None] == np.arange(2 * NA_KH - 1)).astype(np.float32)
    oh_c = (dc_idx[:, :, None] == np.arange(2 * NA_KW - 1)[None, None, :]).astype(np.float32)
    bias = jnp.einsum("vrja,hab,qkb->hvrqjk", oh_r, rpb.astype(F32), oh_c, precision=lax.Precision.HIGHEST)
    mask = row_mask[None, :, :, None, :, None] & col_mask[None, None, None, :, None, :]
    bias = jnp.where(mask, bias, NEG_INF)
    return bias.reshape(rpb.shape[0], n_var, rps * GRID_W, span * GRID_W)


def _na(q, k, v, rpb, *, dims, with_ctx):
    b, s, l, hn = dims["B"], dims["S"], dims["L"], dims["HN"]
    r_tot = s // GRID_W
    kh = min(NA_KH, r_tot)
    tq = ROW_TILE
    rps = tq // GRID_W
    assert r_tot >= kh + rps and kh % rps == 0 and (kh // 2) % rps == 0
    bias = _na_bias(rpb, kh, r_tot, rps)
    d = HEAD_DIM
    spb = r_tot // rps
    cpb = l // tq if with_ctx else 0
    lat_tiles = b * spb
    rows = b * (s + (l if with_ctx else 0))

    def qtile(bb, j):
        return jnp.where(j < spb, bb * spb + j, lat_tiles + bb * cpb + (j - spb))

    return pl.pallas_call(
        functools.partial(_na_body, rows_per_step=rps, n_grid_rows=r_tot, kh=kh, s_len=s, l_len=l, lat_steps=spb,
                          with_ctx=with_ctx),
        grid=(b, hn, spb + cpb),
        in_specs=[pl.BlockSpec((None, tq, d), lambda bb, h, j: (h, qtile(bb, j), 0)),
                  pl.BlockSpec((None, None, s + l, d), lambda bb, h, j: (bb, h, 0, 0)),
                  pl.BlockSpec((None, None, s + l, d), lambda bb, h, j: (bb, h, 0, 0)),
                  pl.BlockSpec((None,) + bias.shape[1:], lambda bb, h, j: (h, 0, 0, 0))],
        out_specs=pl.BlockSpec((tq, d), lambda bb, h, j: (qtile(bb, j), h)),
        out_shape=jax.ShapeDtypeStruct((rows, hn * d), BF16),
        compiler_params=_cparams(3), name="na",
    )(q, k, v, bias)


ROUTE_DST_BITS = 16


def _moe_body(te_ref, tv_ref, route_ref, h_ref, w1_ref, w3_ref, w2_ref, out_ref,
              xg, xb, acc_ref, ys, gsem, ssem, *, nf, tm):
    i, f = pl.program_id(0), pl.program_id(1)
    n_tiles = pl.num_programs(0)
    valid = tv_ref[i] > 0
    chunk = tm // nf
    dump1 = out_ref.shape[0] - tm

    def src_row(p):
        return lax.shift_right_logical(route_ref[p], ROUTE_DST_BITS)

    def dst_row(p):
        return route_ref[p] & ((1 << ROUTE_DST_BITS) - 1)

    def gather_copy(r, row):
        return pltpu.make_async_copy(h_ref.at[pl.ds(row, 1)], xg.at[pl.ds(r, 1)], gsem)

    def scatter_copy(slot, r, row):
        return pltpu.make_async_copy(ys.at[slot, pl.ds(r, 1)], out_ref.at[pl.ds(row, 1)], ssem.at[slot])

    def wait_gather():
        pltpu.make_async_copy(h_ref.at[pl.ds(0, tm)], xg, gsem).wait()

    def wait_scatter(slot):
        pltpu.make_async_copy(ys.at[slot], out_ref.at[pl.ds(0, tm)], ssem.at[slot]).wait()

    def issue_chunks():
        g_tile = jnp.minimum(i + 1, n_tiles - 1)
        s_tile = jnp.maximum(i - 1, 0)
        s_slot = (i + 1) % 2
        for r in range(chunk):
            rr = f * chunk + r
            gather_copy(rr, src_row(g_tile * tm + rr)).start()
            row = jnp.where(i > 0, dst_row(s_tile * tm + rr), dump1 + rr)
            scatter_copy(s_slot, rr, row).start(priority=r % 2)

    @pl.when(jnp.logical_and(i == 0, f == 0))
    def _():
        ys[...] = jnp.zeros_like(ys)

        def go(r, c):
            gather_copy(r, src_row(r)).start()
            return c
        lax.fori_loop(0, tm, go, 0)

    @pl.when(f == 0)
    def _():
        wait_gather()
        xb[...] = xg[...].astype(BF16)
        acc_ref[...] = jnp.zeros_like(acc_ref)

    @pl.when(valid)
    def _():
        issue_chunks()
        x = xb[...]
        h1 = jnp.dot(x, w1_ref[...], preferred_element_type=F32)
        h3 = jnp.dot(x, w3_ref[...], preferred_element_type=F32)
        hm = (_silu(h1) * h3).astype(BF16)
        acc_ref[...] += jnp.dot(hm, w2_ref[...], preferred_element_type=F32)

    @pl.when(jnp.logical_not(valid))
    def _():
        issue_chunks()

    @pl.when(f == nf - 1)
    def _():
        @pl.when(i > 0)
        def _():
            wait_scatter(i % 2)

        ys[i % 2] = acc_ref[...]

        @pl.when(i == n_tiles - 1)
        def _():
            def go(r, c):
                scatter_copy(i % 2, r, dst_row(i * tm + r)).start()
                return c
            lax.fori_loop(0, tm, go, 0)
            wait_scatter(i % 2)
            wait_scatter((i + 1) % 2)
            wait_gather()


def _moe_ffn(h, w1, w3, w2, plan, *, n_out_rows, tf):
    route, tile_expert, tile_valid = plan
    tm = MOE_TILE_M
    p_tot = route.shape[0]
    d = h.shape[1]
    f_dim = w1.shape[-1]
    nf = f_dim // tf
    assert tm % nf == 0 and n_out_rows + 2 * tm <= 1 << ROUTE_DST_BITS and h.shape[0] < 1 << (31 - ROUTE_DST_BITS)

    def fidx(i, f, tv):
        return jnp.where(tv[i] > 0, f, nf - 1)

    return pl.pallas_call(
        functools.partial(_moe_body, nf=nf, tm=tm),
        grid_spec=pltpu.PrefetchScalarGridSpec(
            num_scalar_prefetch=3, grid=(p_tot // tm, nf),
            in_specs=[pl.BlockSpec(memory_space=pl.ANY),
                      pl.BlockSpec((None, d, tf), lambda i, f, te, tv, rt: (te[i], 0, fidx(i, f, tv))),
                      pl.BlockSpec((None, d, tf), lambda i, f, te, tv, rt: (te[i], 0, fidx(i, f, tv))),
                      pl.BlockSpec((None, tf, d), lambda i, f, te, tv, rt: (te[i], fidx(i, f, tv), 0))],
            out_specs=pl.BlockSpec(memory_space=pl.ANY),
            scratch_shapes=[pltpu.VMEM((tm, d), F32), pltpu.VMEM((tm, d), BF16), pltpu.VMEM((tm, d), F32),
                            pltpu.VMEM((2, tm, d), F32), pltpu.SemaphoreType.DMA(()),
                            pltpu.SemaphoreType.DMA((2,))]),
        out_shape=jax.ShapeDtypeStruct((n_out_rows + 2 * tm, d), F32),
        compiler_params=_cparams(2), name="moe_ffn",
    )(tile_expert, tile_valid, route, h, w1, w3, w2)


def _moe_plan(e_idx, n_experts, tm):
    flat_e = e_idx.reshape(-1)
    n_slots = flat_e.shape[0]
    onehot = (flat_e[:, None] == jnp.arange(n_experts)[None, :]).astype(jnp.int32)
    csum = jnp.cumsum(onehot, axis=0)
    rank = jnp.sum(csum * onehot, axis=1) - 1
    counts = csum[-1]
    tiles_per = (counts + tm - 1) // tm
    cum_tiles = jnp.cumsum(tiles_per)
    row_start = (cum_tiles - tiles_per) * tm
    dest = jnp.sum(row_start[None, :] * onehot, axis=1) + rank
    n_tiles = n_slots // tm + n_experts
    n_tiles += n_tiles % 2 == 0
    p_tot = n_tiles * tm
    te = jnp.sum((jnp.arange(n_tiles)[:, None] >= cum_tiles[None, :]).astype(jnp.int32), axis=1)
    tile_valid = (te < n_experts).astype(jnp.int32)
    tile_expert = jnp.minimum(te, n_experts - 1)
    slot = jnp.arange(n_slots, dtype=jnp.int32)
    n_tok = n_slots // TOP_K
    pos = jnp.arange(p_tot, dtype=jnp.int32)
    dump = n_slots + ((pos // tm) % 2) * tm + pos % tm
    word = (slot // TOP_K) * (1 << ROUTE_DST_BITS) + (slot % TOP_K) * n_tok + slot // TOP_K
    route = dump.at[dest].set(word)
    return route, tile_expert, tile_valid


def _rope_tables(n_batch, s_len, l_len):
    t = np.arange(s_len)
    rows, cols = t // GRID_W, t % GRID_W

    def table(width):
        half = width // 2
        quarter = half // 2
        lane = np.arange(width)
        pos = np.where((lane // half)[None, :] == 0, rows[:, None], cols[:, None]).astype(np.float64)
        fi = (lane % half) % quarter
        inv = ROPE_THETA ** (-(2.0 * fi) / half)
        ang = (pos.astype(np.float32) * inv.astype(np.float32)[None, :]).astype(np.float32)
        sign = np.where((lane % half) < quarter, -1.0, 1.0)
        return np.cos(ang).astype(np.float32), (np.sin(ang) * sign[None, :]).astype(np.float32)

    cg, sg = table(HEAD_DIM)
    cm, sm = table(MLA_ROPE)
    cm, sm = np.tile(cm, (1, 2)), np.tile(sm, (1, 2))

    def flat(lat, fill):
        return np.concatenate([np.tile(lat, (n_batch, 1)),
                               np.full((n_batch * l_len, LANES), fill, np.float32)], axis=0)

    return tuple(jnp.asarray(a) for a in (flat(cg, 1.0), flat(sg, 0.0), flat(cm, 1.0), flat(sm, 0.0)))


def _round_up(a, m):
    return (a + m - 1) // m * m


def _pick_tile(n, pref):
    for t in pref:
        if n % t == 0:
            return t
    return n


def kernel(x, c, ctx, c_ctx, w_ada, b_ada, g_pre_mix, g_post_mix, g_pre_ffn, g_post_ffn, w_in, g_q_a, g_kv_a, w_q_up, w_kv_up, g_q_gqa, g_k_gqa, na_rpb, g_grp, w_out, ffn_w1, ffn_w3, ffn_w2, router, moe_w1, moe_w3, moe_w2):
    n_batch, s_len, d = x.shape
    l_len = ctx.shape[1]
    depth = w_ada.shape[0]
    q_lora, kv_lora = g_q_a.shape[-1], g_kv_a.shape[-1]
    dk_m = MLA_NOPE + MLA_ROPE
    hm = w_q_up.shape[-1] // dk_m
    hg = 3 * d // (8 * HEAD_DIM)
    hkv = hg // 3
    hn = d // (4 * HEAD_DIM)
    g_gqa = hg // hkv
    n_lat, n_ctx = n_batch * s_len, n_batch * l_len
    t_tot = n_lat + n_ctx
    assert s_len % MM_TILE_M == 0 and n_ctx % MM_TILE_M == 0 and l_len % ROW_TILE == 0
    assert hm % 2 == 0 and n_batch + 1 <= SUBLANES

    off_qg = q_lora + kv_lora
    off_kvg = off_qg + hg * HEAD_DIM
    off_n = off_kvg + 2 * hkv * HEAD_DIM
    n_main = off_n + 3 * hn * HEAD_DIM
    dims = dict(B=n_batch, S=s_len, L=l_len, HM=hm, HG=hg, HKV=hkv, HN=hn,
                OFF_QG=off_qg, OFF_KVG=off_kvg, OFF_N=off_n)

    tables = _rope_tables(n_batch, s_len, l_len)
    tk_all = s_len + l_len
    tk_m = _pick_tile(tk_all, (768, 512, 256))
    tk_g = tk_m

    mods = _ada(jnp.concatenate([c_ctx[None, :], c], axis=0), w_ada, b_ada)

    lat_tiles, all_tiles = n_lat // ROW_TILE, t_tot // ROW_TILE
    tiles_per_b = s_len // ROW_TILE

    def group_of(i):
        return jnp.where(i < lat_tiles, 1 + i // tiles_per_b, 0)

    def modtab_of(i):
        return mods[i, :n_batch + 1].reshape((n_batch + 1) * 6, 1, d)

    xs = (x.reshape(n_lat, d), ctx.reshape(n_ctx, d))
    modtab = modtab_of(0)
    _, h, _ = _row_call(xs, n_tiles=all_tiles, group_fn=group_of, modtab=modtab,
                        nxt=(g_pre_mix[0], 0, 1, modtab), name="row_in")

    for i in range(depth):
        last = i == depth - 1
        modtab = modtab_of(i)
        m_out = n_lat if last else t_tot
        out_tiles = m_out // ROW_TILE

        wi = w_in[i]
        c_kpe = q_lora + kv_lora
        w_main = jnp.concatenate([wi[:, :c_kpe], wi[:, c_kpe + MLA_ROPE:]], axis=1).astype(BF16)
        w_kpe = jnp.zeros((d, LANES), BF16).at[:, :MLA_ROPE].set(wi[:, c_kpe:c_kpe + MLA_ROPE].astype(BF16))
        u = _matmul(h, w_main, m_rows=t_tot, tn=_pick_tile(n_main, (1024, 512, 256, 128)), name="mm_in")
        kpe = _matmul(h, w_kpe, m_rows=t_tot, tn=LANES, name="mm_kpe")
        wq = w_q_up[i].reshape(q_lora, hm, dk_m)
        wq = jnp.concatenate([wq[:, :, :MLA_NOPE].reshape(q_lora, hm * MLA_NOPE),
                              wq[:, :, MLA_NOPE:].reshape(q_lora, hm * MLA_ROPE)], axis=1).astype(BF16)
        a_cols = _round_up(off_qg, LANES)
        qraw = _matmul(u, wq, m_rows=t_tot, tn=_pick_tile(hm * dk_m, (768, 384, 128)), a_block_cols=a_cols,
                       a_lo=0, a_hi=q_lora, rms_g=g_q_a[i], name="mm_qup")
        kvraw = _matmul(u, w_kv_up[i].astype(BF16), m_rows=t_tot,
                        tn=_pick_tile(hm * (MLA_NOPE + MLA_V), (1024, 768, 512, 256)), a_block_cols=a_cols,
                        a_lo=q_lora, a_hi=q_lora + kv_lora, rms_g=g_kv_a[i], name="mm_kvup")
        qm, km, vm, qg, kg, vg, qn, kn, vn = _prep(u, qraw, kvraw, kpe, tables, g_q_gqa[i], g_k_gqa[i],
                                                   dims=dims, tk_m=tk_m, tk_g=tk_g)

        q_tiles_per_b = s_len // ROW_TILE
        o_m = (_attention(qm, km, vm, dims=dims, g=1, tpi=MLA_TILES_PER_ITEM,
                          nt=q_tiles_per_b // MLA_TILES_PER_ITEM, unroll=8, name="attn_mla"),)
        o_g = (_attention(qg, kg, vg, dims=dims, g=g_gqa, tpi=1, nt=_pick_tile(q_tiles_per_b, (16, 8, 4, 2)),
                          unroll=1, name="attn_gqa"),)
        if not last:
            o_m += (_attention_ctx(qm, km, vm, dims=dims, g=1, name="attn_mla_ctx"),)
            o_g += (_attention_ctx(qg, kg, vg, dims=dims, g=g_gqa, name="attn_gqa_ctx"),)
        o_n = (_na(qn, kn, vn, na_rpb[i], dims=dims, with_ctx=not last),)

        y = _matmul_groups((o_m, o_g, o_n), g_grp[i], w_out[i].astype(BF16), m_rows=m_out,
                           tn=_pick_tile(d, (1024, 512, 256)), name="mm_out")

        j = i // 2
        moe = i % 2 == 1
        res = ([(y, 0)], 2, g_post_mix[i], None)
        nxt = (g_pre_ffn[i], 3, 4, modtab)
        if moe:
            x_all, hf, rt = _row_call(xs, n_tiles=out_tiles, group_fn=group_of, modtab=modtab, res=res, nxt=nxt,
                                      router=router[j], h_dtype=F32, name="row_mix")
            n_experts = router.shape[-1]
            e_idx = rt[:, TOP_K:2 * TOP_K].astype(jnp.int32)
            plan = _moe_plan(e_idx, n_experts, MOE_TILE_M)
            f_weights = rt
            f_out = _moe_ffn(hf, moe_w1[j].astype(BF16), moe_w3[j].astype(BF16), moe_w2[j].astype(BF16), plan,
                             n_out_rows=TOP_K * m_out, tf=_pick_tile(moe_w1.shape[-1], (256, 128)))
            f_blocks = [(f_out, k * out_tiles) for k in range(TOP_K)]
        else:
            x_all, h, _ = _row_call(xs, n_tiles=out_tiles, group_fn=group_of, modtab=modtab, res=res, nxt=nxt,
                                    name="row_mix")
            f_dim = ffn_w1.shape[-1]
            f_pad = _round_up(f_dim, 1024)
            pad = ((0, 0), (0, f_pad - f_dim))
            w1 = jnp.pad(ffn_w1[j].astype(BF16), pad)
            w3 = jnp.pad(ffn_w3[j].astype(BF16), pad)
            w2 = jnp.pad(ffn_w2[j].astype(BF16), (pad[1], pad[0]))
            gact = _glu(h, w1, w3, m_rows=m_out, tf=1024, name="ffn_glu")
            f_out = _matmul(gact, w2, m_rows=m_out, tn=_pick_tile(d, (512, 256)), name="ffn_down")
            f_blocks = [(f_out, 0)]
            f_weights = None
        xs = (x_all,)

        res = (f_blocks, 5, g_post_ffn[i], f_weights)
        if last:
            x_all, _, _ = _row_call(xs, n_tiles=out_tiles, group_fn=group_of, modtab=modtab, res=res, name="row_ffn")
        else:
            x_all, h, _ = _row_call(xs, n_tiles=out_tiles, group_fn=group_of, modtab=modtab, res=res,
                                    nxt=(g_pre_mix[i + 1], 0, 1, modtab_of(i + 1)), name="row_ffn")
        xs = (x_all,)
    return xs[0][:n_lat].reshape(n_batch, s_len, d)
```
